```python
import jax, jax.numpy as jnp
from jax import lax
import numpy as np

D_MODEL = 1024
BATCH = 8
SEQ = 16384
DEPTH = 4

D_MIX = 1024
MLA_HEADS = 8
QK_NOPE = 64
QK_ROPE = 32
QK_HEAD = QK_NOPE + QK_ROPE
V_HEAD = 64
Q_LORA = 384
KV_LORA = 256
MLA_WIDTH = MLA_HEADS * V_HEAD
ROPE_THETA = 10000.0
Q_BLOCK = 128
POOL_WINDOWS = (2, 4, 8, 16)
POOL_GROUPS = len(POOL_WINDOWS)
POOL_WIDTH = D_MIX - MLA_WIDTH
POOL_GROUP_DIM = POOL_WIDTH // POOL_GROUPS
D_IN = Q_LORA + KV_LORA + QK_ROPE + POOL_WIDTH
D_FF = 2816
EPS = 1e-6

kernel_name = "hybrid_mla_pool_macaron_encoder"


def rmsnorm(x, g):
    xf = x.astype(jnp.float32)
    y = xf * lax.rsqrt(jnp.mean(xf * xf, axis=-1, keepdims=True) + EPS)
    return (y * g.astype(jnp.float32)).astype(x.dtype)


def swiglu(h, w_gu, w_down):
    gu = h @ w_gu
    g, u = gu[..., :D_FF], gu[..., D_FF:]
    return (jax.nn.silu(g) * u) @ w_down


def rope_tables(seq):
    pos = jnp.arange(seq, dtype=jnp.float32)
    inv = ROPE_THETA ** (-jnp.arange(0, QK_ROPE, 2, dtype=jnp.float32) / QK_ROPE)
    ang = pos[:, None] * inv[None, :]
    return jnp.cos(ang), jnp.sin(ang)


def apply_rope(t, cos, sin):
    tf = t.astype(jnp.float32)
    half = QK_ROPE // 2
    t1, t2 = tf[..., :half], tf[..., half:]
    c, s = cos[None, :, None, :], sin[None, :, None, :]
    out = jnp.concatenate([t1 * c - t2 * s, t2 * c + t1 * s], axis=-1)
    return out.astype(t.dtype)


def bidir_attention(q, k, v):
    B, S, H, Dq = q.shape
    nb = S // Q_BLOCK
    qb = jnp.moveaxis((q * (Dq ** -0.5)).reshape(B, nb, Q_BLOCK, H, Dq), 1, 0)

    def block(q_blk):
        s = jnp.einsum('bqhd,bkhd->bhqk', q_blk, k, preferred_element_type=jnp.float32)
        p = jax.nn.softmax(s, axis=-1).astype(v.dtype)
        return jnp.einsum('bhqk,bkhd->bqhd', p, v)

    o = lax.map(block, qb)
    return jnp.moveaxis(o, 0, 1).reshape(B, S, H * V_HEAD)


def mla_mixer(c_q, c_kv, k_pe, q_lat_norm, kv_lat_norm, w_uq, w_uk, w_uv, q_norm, k_norm, cos, sin):
    B, S, _ = c_q.shape
    c_q = rmsnorm(c_q, q_lat_norm)
    c_kv = rmsnorm(c_kv, kv_lat_norm)
    q = (c_q @ w_uq).reshape(B, S, MLA_HEADS, QK_HEAD)
    k_nope = (c_kv @ w_uk).reshape(B, S, MLA_HEADS, QK_NOPE)
    v = (c_kv @ w_uv).reshape(B, S, MLA_HEADS, V_HEAD)
    k_pe_h = jnp.broadcast_to(k_pe[:, :, None, :], (B, S, MLA_HEADS, QK_ROPE))
    k = jnp.concatenate([k_nope, k_pe_h], axis=-1)
    q = rmsnorm(q, q_norm)
    k = rmsnorm(k, k_norm)
    q = jnp.concatenate([q[..., :QK_NOPE], apply_rope(q[..., QK_NOPE:], cos, sin)], axis=-1)
    k = jnp.concatenate([k[..., :QK_NOPE], apply_rope(k[..., QK_NOPE:], cos, sin)], axis=-1)
    return bidir_attention(q, k, v)


def pool_mixer(p, w_pool, pool_scale):
    B, S, C = p.shape
    pf = p.astype(jnp.float32)
    cs = jnp.concatenate([jnp.zeros((B, 1, C), jnp.float32), jnp.cumsum(pf, axis=1)], axis=1)
    idx = jnp.arange(S)
    outs = []
    for g, w in enumerate(POOL_WINDOWS):
        left = w // 2
        right = w - 1 - left
        lo = jnp.clip(idx - left, 0, S)
        hi = jnp.clip(idx + right + 1, 0, S)
        csg = cs[..., g * POOL_GROUP_DIM:(g + 1) * POOL_GROUP_DIM]
        wsum = jnp.take(csg, hi, axis=1) - jnp.take(csg, lo, axis=1)
        cnt = (hi - lo).astype(jnp.float32)[None, :, None]
        outs.append(wsum / cnt)
    pooled = jnp.stack(outs, axis=2)
    mixed = (pooled - pf.reshape(B, S, POOL_GROUPS, POOL_GROUP_DIM)).astype(p.dtype)
    y = jnp.einsum('bsgc,gcd->bsgd', mixed, w_pool).reshape(B, S, C)
    return y * pool_scale


def _fwd_setup_inputs(seed: int = 0) -> dict:
    key = jax.random.key(seed)
    ks = jax.random.split(key, 20)
    f32 = jnp.float32

    def w(k, shape, fan_in):
        return jax.random.normal(k, shape, f32) * (fan_in ** -0.5)

    def gain(k, shape):
        return 1.0 + 0.02 * jax.random.normal(k, shape, f32)

    return {
        "x": jax.random.normal(ks[0], (BATCH, SEQ, D_MODEL), f32),
        "ffn1_norm": gain(ks[1], (DEPTH, D_MODEL)),
        "ffn1_w_gu": w(ks[2], (DEPTH, D_MODEL, 2 * D_FF), D_MODEL),
        "ffn1_w_down": w(ks[3], (DEPTH, D_FF, D_MODEL), D_FF),
        "mix_norm": gain(ks[4], (DEPTH, D_MODEL)),
        "w_in": w(ks[5], (DEPTH, D_MODEL, D_IN), D_MODEL),
        "q_lat_norm": gain(ks[6], (DEPTH, Q_LORA)),
        "kv_lat_norm": gain(ks[7], (DEPTH, KV_LORA)),
        "w_uq": w(ks[8], (DEPTH, Q_LORA, MLA_HEADS * QK_HEAD), Q_LORA),
        "w_uk": w(ks[9], (DEPTH, KV_LORA, MLA_HEADS * QK_NOPE), KV_LORA),
        "w_uv": w(ks[10], (DEPTH, KV_LORA, MLA_HEADS * V_HEAD), KV_LORA),
        "q_norm": gain(ks[11], (DEPTH, QK_HEAD)),
        "k_norm": gain(ks[12], (DEPTH, QK_HEAD)),
        "w_pool": w(ks[13], (DEPTH, POOL_GROUPS, POOL_GROUP_DIM, POOL_GROUP_DIM), POOL_GROUP_DIM),
        "pool_scale": gain(ks[14], (DEPTH, POOL_WIDTH)),
        "w_out": w(ks[15], (DEPTH, D_MIX, D_MODEL), D_MIX),
        "ffn2_norm": gain(ks[16], (DEPTH, D_MODEL)),
        "ffn2_w_gu": w(ks[17], (DEPTH, D_MODEL, 2 * D_FF), D_MODEL),
        "ffn2_w_down": w(ks[18], (DEPTH, D_FF, D_MODEL), D_FF),
    }


def _fwd_reference(x, ffn1_norm, ffn1_w_gu, ffn1_w_down, mix_norm, w_in, q_lat_norm, kv_lat_norm,
              w_uq, w_uk, w_uv, q_norm, k_norm, w_pool, pool_scale, w_out,
              ffn2_norm, ffn2_w_gu, ffn2_w_down):
    S = x.shape[1]
    cos, sin = rope_tables(S)
    o_kv = Q_LORA
    o_pe = Q_LORA + KV_LORA
    o_pool = Q_LORA + KV_LORA + QK_ROPE
    for l in range(DEPTH):
        x = x + 0.5 * swiglu(rmsnorm(x, ffn1_norm[l]), ffn1_w_gu[l], ffn1_w_down[l])
        z = rmsnorm(x, mix_norm[l]) @ w_in[l]
        a = mla_mixer(z[..., :o_kv], z[..., o_kv:o_pe], z[..., o_pe:o_pool],
                      q_lat_norm[l], kv_lat_norm[l], w_uq[l], w_uk[l], w_uv[l],
                      q_norm[l], k_norm[l], cos, sin)
        b = pool_mixer(z[..., o_pool:], w_pool[l], pool_scale[l])
        x = x + jnp.concatenate([a, b], axis=-1) @ w_out[l]
        x = x + 0.5 * swiglu(rmsnorm(x, ffn2_norm[l]), ffn2_w_gu[l], ffn2_w_down[l])
    return x


import jax as _jax
import jax.numpy as _jnp

TWIN_FORMAT = 'train_step'
FWD_PARAMS = ['x', 'ffn1_norm', 'ffn1_w_gu', 'ffn1_w_down', 'mix_norm', 'w_in', 'q_lat_norm', 'kv_lat_norm', 'w_uq', 'w_uk', 'w_uv', 'q_norm', 'k_norm', 'w_pool', 'pool_scale', 'w_out', 'ffn2_norm', 'ffn2_w_gu', 'ffn2_w_down']
TWIN_WEIGHTS = ['ffn1_norm', 'ffn1_w_gu', 'ffn1_w_down', 'mix_norm', 'w_in', 'q_lat_norm', 'kv_lat_norm', 'w_uq', 'w_uk', 'w_uv', 'q_norm', 'k_norm', 'w_pool', 'pool_scale', 'w_out', 'ffn2_norm', 'ffn2_w_gu', 'ffn2_w_down']
TWIN_DIFF_INPUT = 'x'
TWIN_INPUTS = ['x', 'ffn1_norm', 'ffn1_w_gu', 'ffn1_w_down', 'mix_norm', 'w_in', 'q_lat_norm', 'kv_lat_norm', 'w_uq', 'w_uk', 'w_uv', 'q_norm', 'k_norm', 'w_pool', 'pool_scale', 'w_out', 'ffn2_norm', 'ffn2_w_gu', 'ffn2_w_down', 'loss_target', 'm_ffn1_norm', 'm_ffn1_w_gu', 'm_ffn1_w_down', 'm_mix_norm', 'm_w_in', 'm_q_lat_norm', 'm_kv_lat_norm', 'm_w_uq', 'm_w_uk', 'm_w_uv', 'm_q_norm', 'm_k_norm', 'm_w_pool', 'm_pool_scale', 'm_w_out', 'm_ffn2_norm', 'm_ffn2_w_gu', 'm_ffn2_w_down', 'v_ffn1_norm', 'v_ffn1_w_gu', 'v_ffn1_w_down', 'v_mix_norm', 'v_w_in', 'v_q_lat_norm', 'v_kv_lat_norm', 'v_w_uq', 'v_w_uk', 'v_w_uv', 'v_q_norm', 'v_k_norm', 'v_w_pool', 'v_pool_scale', 'v_w_out', 'v_ffn2_norm', 'v_ffn2_w_gu', 'v_ffn2_w_down']
TWIN_OUTPUTS = ['loss', 'grad_x', 'grad_ffn1_norm', 'grad_ffn1_w_gu', 'grad_ffn1_w_down', 'grad_mix_norm', 'grad_w_in', 'grad_q_lat_norm', 'grad_kv_lat_norm', 'grad_w_uq', 'grad_w_uk', 'grad_w_uv', 'grad_q_norm', 'grad_k_norm', 'grad_w_pool', 'grad_pool_scale', 'grad_w_out', 'grad_ffn2_norm', 'grad_ffn2_w_gu', 'grad_ffn2_w_down', 'delta_ffn1_norm', 'delta_ffn1_w_gu', 'delta_ffn1_w_down', 'delta_mix_norm', 'delta_w_in', 'delta_q_lat_norm', 'delta_kv_lat_norm', 'delta_w_uq', 'delta_w_uk', 'delta_w_uv', 'delta_q_norm', 'delta_k_norm', 'delta_w_pool', 'delta_pool_scale', 'delta_w_out', 'delta_ffn2_norm', 'delta_ffn2_w_gu', 'delta_ffn2_w_down', 'new_m_ffn1_norm', 'new_m_ffn1_w_gu', 'new_m_ffn1_w_down', 'new_m_mix_norm', 'new_m_w_in', 'new_m_q_lat_norm', 'new_m_kv_lat_norm', 'new_m_w_uq', 'new_m_w_uk', 'new_m_w_uv', 'new_m_q_norm', 'new_m_k_norm', 'new_m_w_pool', 'new_m_pool_scale', 'new_m_w_out', 'new_m_ffn2_norm', 'new_m_ffn2_w_gu', 'new_m_ffn2_w_down', 'new_v_ffn1_norm', 'new_v_ffn1_w_gu', 'new_v_ffn1_w_down', 'new_v_mix_norm', 'new_v_w_in', 'new_v_q_lat_norm', 'new_v_kv_lat_norm', 'new_v_w_uq', 'new_v_w_uk', 'new_v_w_uv', 'new_v_q_norm', 'new_v_k_norm', 'new_v_w_pool', 'new_v_pool_scale', 'new_v_w_out', 'new_v_ffn2_norm', 'new_v_ffn2_w_gu', 'new_v_ffn2_w_down']
TWIN_LEAF_KINDS = {'loss': 'loss', 'grad_x': 'grad_x', 'grad_ffn1_norm': 'grad_w', 'grad_ffn1_w_gu': 'grad_w', 'grad_ffn1_w_down': 'grad_w', 'grad_mix_norm': 'grad_w', 'grad_w_in': 'grad_w', 'grad_q_lat_norm': 'grad_w', 'grad_kv_lat_norm': 'grad_w', 'grad_w_uq': 'grad_w', 'grad_w_uk': 'grad_w', 'grad_w_uv': 'grad_w', 'grad_q_norm': 'grad_w', 'grad_k_norm': 'grad_w', 'grad_w_pool': 'grad_w', 'grad_pool_scale': 'grad_w', 'grad_w_out': 'grad_w', 'grad_ffn2_norm': 'grad_w', 'grad_ffn2_w_gu': 'grad_w', 'grad_ffn2_w_down': 'grad_w', 'delta_ffn1_norm': 'delta_w', 'delta_ffn1_w_gu': 'delta_w', 'delta_ffn1_w_down': 'delta_w', 'delta_mix_norm': 'delta_w', 'delta_w_in': 'delta_w', 'delta_q_lat_norm': 'delta_w', 'delta_kv_lat_norm': 'delta_w', 'delta_w_uq': 'delta_w', 'delta_w_uk': 'delta_w', 'delta_w_uv': 'delta_w', 'delta_q_norm': 'delta_w', 'delta_k_norm': 'delta_w', 'delta_w_pool': 'delta_w', 'delta_pool_scale': 'delta_w', 'delta_w_out': 'delta_w', 'delta_ffn2_norm': 'delta_w', 'delta_ffn2_w_gu': 'delta_w', 'delta_ffn2_w_down': 'delta_w', 'new_m_ffn1_norm': 'new_m', 'new_m_ffn1_w_gu': 'new_m', 'new_m_ffn1_w_down': 'new_m', 'new_m_mix_norm': 'new_m', 'new_m_w_in': 'new_m', 'new_m_q_lat_norm': 'new_m', 'new_m_kv_lat_norm': 'new_m', 'new_m_w_uq': 'new_m', 'new_m_w_uk': 'new_m', 'new_m_w_uv': 'new_m', 'new_m_q_norm': 'new_m', 'new_m_k_norm': 'new_m', 'new_m_w_pool': 'new_m', 'new_m_pool_scale': 'new_m', 'new_m_w_out': 'new_m', 'new_m_ffn2_norm': 'new_m', 'new_m_ffn2_w_gu': 'new_m', 'new_m_ffn2_w_down': 'new_m', 'new_v_ffn1_norm': 'new_v', 'new_v_ffn1_w_gu': 'new_v', 'new_v_ffn1_w_down': 'new_v', 'new_v_mix_norm': 'new_v', 'new_v_w_in': 'new_v', 'new_v_q_lat_norm': 'new_v', 'new_v_kv_lat_norm': 'new_v', 'new_v_w_uq': 'new_v', 'new_v_w_uk': 'new_v', 'new_v_w_uv': 'new_v', 'new_v_q_norm': 'new_v', 'new_v_k_norm': 'new_v', 'new_v_w_pool': 'new_v', 'new_v_pool_scale': 'new_v', 'new_v_w_out': 'new_v', 'new_v_ffn2_norm': 'new_v', 'new_v_ffn2_w_gu': 'new_v', 'new_v_ffn2_w_down': 'new_v'}


def _forward(args):
    return _fwd_reference(*[args[k] for k in FWD_PARAMS])


def _output_shape():
    def fwd():
        inp = _fwd_setup_inputs(0)
        return _fwd_reference(*[inp[k] for k in FWD_PARAMS])
    out = _jax.eval_shape(fwd)
    return out.shape, out.dtype

N_MICROBATCH = 1
ADAM_LR = 0.001
ADAM_B1 = 0.9
ADAM_B2 = 0.999
ADAM_EPS = 1e-08
ADAM_WD = 0.01
ADAM_STEP = 10
PER_EXAMPLE_BATCH_AXIS = {'x': 0, 'loss_target': 0}
SHARED_INPUTS = []
_WEIGHT_DTYPES = {'ffn1_norm': _jnp.float32, 'ffn1_w_gu': _jnp.float32, 'ffn1_w_down': _jnp.float32, 'mix_norm': _jnp.float32, 'w_in': _jnp.float32, 'q_lat_norm': _jnp.float32, 'kv_lat_norm': _jnp.float32, 'w_uq': _jnp.float32, 'w_uk': _jnp.float32, 'w_uv': _jnp.float32, 'q_norm': _jnp.float32, 'k_norm': _jnp.float32, 'w_pool': _jnp.float32, 'pool_scale': _jnp.float32, 'w_out': _jnp.float32, 'ffn2_norm': _jnp.float32, 'ffn2_w_gu': _jnp.float32, 'ffn2_w_down': _jnp.float32}
MOMENT_SCALE = {'ffn1_norm': 2.357146e+01, 'ffn1_w_gu': 3.289032e-01, 'ffn1_w_down': 6.995685e-01, 'mix_norm': 5.286975e+01, 'w_in': 5.354367e+00, 'q_lat_norm': 2.928116e-01, 'kv_lat_norm': 8.220088e-01, 'w_uq': 2.050722e-01, 'w_uk': 2.888514e-01, 'w_uv': 2.457744e-01, 'q_norm': 1.257169e+00, 'k_norm': 1.255400e+00, 'w_pool': 1.224090e+01, 'pool_scale': 1.083557e+02, 'w_out': 6.241816e+00, 'ffn2_norm': 2.448289e+01, 'ffn2_w_gu': 3.208535e-01, 'ffn2_w_down': 7.072934e-01}


def _to_microbatches(a, axis):
    t = _jnp.moveaxis(a, axis, 0)
    t = t.reshape((N_MICROBATCH, t.shape[0] // N_MICROBATCH) + t.shape[1:])
    return _jnp.moveaxis(t, 1, axis + 1)


def setup_inputs(seed: int = 0) -> dict:
    inp = _fwd_setup_inputs(seed)
    key = _jax.random.fold_in(_jax.random.key(seed), 7919)
    shape, _ = _output_shape()
    out = dict(inp)
    out["loss_target"] = _jax.random.normal(_jax.random.fold_in(key, 0), shape, _jnp.float32)
    for i, name in enumerate(TWIN_WEIGHTS):
        w = inp[name].astype(_jnp.float32)
        if MOMENT_SCALE is None:
            s = _jnp.sqrt(_jnp.mean(_jnp.square(w)) + 1e-30)
        else:
            s = MOMENT_SCALE[name]
        km, kv = _jax.random.split(_jax.random.fold_in(key, i + 1))
        out[name] = w
        out["m_" + name] = s * _jax.random.normal(km, w.shape, _jnp.float32)
        out["v_" + name] = (s * s) * _jax.random.uniform(kv, w.shape, _jnp.float32, 0.5, 1.5)
    if N_MICROBATCH > 1:
        for name, axis in PER_EXAMPLE_BATCH_AXIS.items():
            out[name] = _to_microbatches(out[name], axis)
    return {'x': out['x'], 'ffn1_norm': out['ffn1_norm'], 'ffn1_w_gu': out['ffn1_w_gu'], 'ffn1_w_down': out['ffn1_w_down'], 'mix_norm': out['mix_norm'], 'w_in': out['w_in'], 'q_lat_norm': out['q_lat_norm'], 'kv_lat_norm': out['kv_lat_norm'], 'w_uq': out['w_uq'], 'w_uk': out['w_uk'], 'w_uv': out['w_uv'], 'q_norm': out['q_norm'], 'k_norm': out['k_norm'], 'w_pool': out['w_pool'], 'pool_scale': out['pool_scale'], 'w_out': out['w_out'], 'ffn2_norm': out['ffn2_norm'], 'ffn2_w_gu': out['ffn2_w_gu'], 'ffn2_w_down': out['ffn2_w_down'], 'loss_target': out['loss_target'], 'm_ffn1_norm': out['m_ffn1_norm'], 'm_ffn1_w_gu': out['m_ffn1_w_gu'], 'm_ffn1_w_down': out['m_ffn1_w_down'], 'm_mix_norm': out['m_mix_norm'], 'm_w_in': out['m_w_in'], 'm_q_lat_norm': out['m_q_lat_norm'], 'm_kv_lat_norm': out['m_kv_lat_norm'], 'm_w_uq': out['m_w_uq'], 'm_w_uk': out['m_w_uk'], 'm_w_uv': out['m_w_uv'], 'm_q_norm': out['m_q_norm'], 'm_k_norm': out['m_k_norm'], 'm_w_pool': out['m_w_pool'], 'm_pool_scale': out['m_pool_scale'], 'm_w_out': out['m_w_out'], 'm_ffn2_norm': out['m_ffn2_norm'], 'm_ffn2_w_gu': out['m_ffn2_w_gu'], 'm_ffn2_w_down': out['m_ffn2_w_down'], 'v_ffn1_norm': out['v_ffn1_norm'], 'v_ffn1_w_gu': out['v_ffn1_w_gu'], 'v_ffn1_w_down': out['v_ffn1_w_down'], 'v_mix_norm': out['v_mix_norm'], 'v_w_in': out['v_w_in'], 'v_q_lat_norm': out['v_q_lat_norm'], 'v_kv_lat_norm': out['v_kv_lat_norm'], 'v_w_uq': out['v_w_uq'], 'v_w_uk': out['v_w_uk'], 'v_w_uv': out['v_w_uv'], 'v_q_norm': out['v_q_norm'], 'v_k_norm': out['v_k_norm'], 'v_w_pool': out['v_w_pool'], 'v_pool_scale': out['v_pool_scale'], 'v_w_out': out['v_w_out'], 'v_ffn2_norm': out['v_ffn2_norm'], 'v_ffn2_w_gu': out['v_ffn2_w_gu'], 'v_ffn2_w_down': out['v_ffn2_w_down']}


def _loss(weights, diff, rest, loss_target):
    with _jax.named_scope("forward"):
        args = {**rest, TWIN_DIFF_INPUT: diff, **{k: w.astype(_WEIGHT_DTYPES[k]) for k, w in weights.items()}}
        y = _forward(args)
    with _jax.named_scope("loss_head"):
        err = _jnp.square(y.astype(_jnp.float32) - loss_target)
        return 0.5 * _jnp.sum(_jnp.mean(err, axis=-1)) if err.ndim else 0.5 * err


def _adamw(w, g, m, v):
    m = ADAM_B1 * m + (1.0 - ADAM_B1) * g
    v = ADAM_B2 * v + (1.0 - ADAM_B2) * _jnp.square(g)
    m_hat = m / (1.0 - ADAM_B1 ** ADAM_STEP)
    v_hat = v / (1.0 - ADAM_B2 ** ADAM_STEP)
    delta = -ADAM_LR * (m_hat / (_jnp.sqrt(v_hat) + ADAM_EPS) + ADAM_WD * w)
    return delta, m, v


def reference(x, ffn1_norm, ffn1_w_gu, ffn1_w_down, mix_norm, w_in, q_lat_norm, kv_lat_norm, w_uq, w_uk, w_uv, q_norm, k_norm, w_pool, pool_scale, w_out, ffn2_norm, ffn2_w_gu, ffn2_w_down, loss_target, m_ffn1_norm, m_ffn1_w_gu, m_ffn1_w_down, m_mix_norm, m_w_in, m_q_lat_norm, m_kv_lat_norm, m_w_uq, m_w_uk, m_w_uv, m_q_norm, m_k_norm, m_w_pool, m_pool_scale, m_w_out, m_ffn2_norm, m_ffn2_w_gu, m_ffn2_w_down, v_ffn1_norm, v_ffn1_w_gu, v_ffn1_w_down, v_mix_norm, v_w_in, v_q_lat_norm, v_kv_lat_norm, v_w_uq, v_w_uk, v_w_uv, v_q_norm, v_k_norm, v_w_pool, v_pool_scale, v_w_out, v_ffn2_norm, v_ffn2_w_gu, v_ffn2_w_down):
    given = dict(x=x, ffn1_norm=ffn1_norm, ffn1_w_gu=ffn1_w_gu, ffn1_w_down=ffn1_w_down, mix_norm=mix_norm, w_in=w_in, q_lat_norm=q_lat_norm, kv_lat_norm=kv_lat_norm, w_uq=w_uq, w_uk=w_uk, w_uv=w_uv, q_norm=q_norm, k_norm=k_norm, w_pool=w_pool, pool_scale=pool_scale, w_out=w_out, ffn2_norm=ffn2_norm, ffn2_w_gu=ffn2_w_gu, ffn2_w_down=ffn2_w_down, loss_target=loss_target, m_ffn1_norm=m_ffn1_norm, m_ffn1_w_gu=m_ffn1_w_gu, m_ffn1_w_down=m_ffn1_w_down, m_mix_norm=m_mix_norm, m_w_in=m_w_in, m_q_lat_norm=m_q_lat_norm, m_kv_lat_norm=m_kv_lat_norm, m_w_uq=m_w_uq, m_w_uk=m_w_uk, m_w_uv=m_w_uv, m_q_norm=m_q_norm, m_k_norm=m_k_norm, m_w_pool=m_w_pool, m_pool_scale=m_pool_scale, m_w_out=m_w_out, m_ffn2_norm=m_ffn2_norm, m_ffn2_w_gu=m_ffn2_w_gu, m_ffn2_w_down=m_ffn2_w_down, v_ffn1_norm=v_ffn1_norm, v_ffn1_w_gu=v_ffn1_w_gu, v_ffn1_w_down=v_ffn1_w_down, v_mix_norm=v_mix_norm, v_w_in=v_w_in, v_q_lat_norm=v_q_lat_norm, v_kv_lat_norm=v_kv_lat_norm, v_w_uq=v_w_uq, v_w_uk=v_w_uk, v_w_uv=v_w_uv, v_q_norm=v_q_norm, v_k_norm=v_k_norm, v_w_pool=v_w_pool, v_pool_scale=v_pool_scale, v_w_out=v_w_out, v_ffn2_norm=v_ffn2_norm, v_ffn2_w_gu=v_ffn2_w_gu, v_ffn2_w_down=v_ffn2_w_down)
    weights = {n: given[n] for n in TWIN_WEIGHTS}
    shared = {n: given[n] for n in SHARED_INPUTS}
    per_example = {n: given[n] for n in ['x']}
    grad_fn = _jax.value_and_grad(_loss, argnums=(0, 1))

    def one_microbatch(ex, loss_target):
        ex = dict(ex)
        diff = ex.pop(TWIN_DIFF_INPUT)
        return grad_fn(weights, diff, {**shared, **ex}, loss_target)

    if N_MICROBATCH == 1:
        loss, (grad_w, grad_x) = one_microbatch(per_example, given["loss_target"])
    else:
        def body(carry, xs):
            loss_sum, grad_sum = carry
            l_k, (gw_k, gx_k) = one_microbatch(xs[0], xs[1])
            with _jax.named_scope("update"):
                return (loss_sum + l_k, _jax.tree.map(_jnp.add, grad_sum, gw_k)), gx_k

        init = (_jnp.zeros((), _jnp.float32), _jax.tree.map(_jnp.zeros_like, weights))
        (loss, grad_w), grad_x = _jax.lax.scan(body, init, (per_example, given["loss_target"]))
    with _jax.named_scope("update"):
        delta_w, new_m, new_v = {}, {}, {}
        for n in TWIN_WEIGHTS:
            delta_w[n], new_m[n], new_v[n] = _adamw(weights[n], grad_w[n], given["m_" + n], given["v_" + n])
    return (loss, grad_x, *[grad_w[n] for n in TWIN_WEIGHTS], *[delta_w[n] for n in TWIN_WEIGHTS],
            *[new_m[n] for n in TWIN_WEIGHTS], *[new_v[n] for n in TWIN_WEIGHTS])
```

```python
import functools

import jax
import jax.numpy as jnp
from jax import lax
from jax.experimental import pallas as pl
from jax.experimental.pallas import tpu as pltpu

F32 = jnp.float32
MXU_DTYPE = jnp.bfloat16
EPS = 1e-6
ROPE_THETA = 10000.0
POOL_WINDOWS = (2, 4, 8, 16)
ADAM_LR, ADAM_B1, ADAM_B2, ADAM_EPS, ADAM_WD, ADAM_STEP = 0.001, 0.9, 0.999, 1e-08, 0.01, 10

LANES = 128
HALO = 64
PACK_W = 1024
N_DEV = 8
VMEM_LIMIT = 56 * 1024 * 1024
MESH = pl.DeviceIdType.MESH

FFN_TM, FFN_TN = 1024, 256
FFN_BWD_TM = 512
MIX_TM = 256
ATTN_TQ, ATTN_TK = 512, 1024
WG_BK = 512
ADAM_TR = 128

SHARDED = ("ffn1_w_gu", "ffn1_w_down", "w_in", "w_uq", "w_uk", "w_uv", "w_out", "ffn2_w_gu", "ffn2_w_down")
ROW_SHARDED = ("ffn1_w_down", "w_out", "ffn2_w_down")
REPLICATED = ("ffn1_norm", "mix_norm", "q_lat_norm", "kv_lat_norm", "q_norm", "k_norm", "w_pool", "pool_scale",
              "ffn2_norm")
WEIGHTS = ("ffn1_norm", "ffn1_w_gu", "ffn1_w_down", "mix_norm", "w_in", "q_lat_norm", "kv_lat_norm", "w_uq", "w_uk",
           "w_uv", "q_norm", "k_norm", "w_pool", "pool_scale", "w_out", "ffn2_norm", "ffn2_w_gu", "ffn2_w_down")


def _tile(n, pref):
    if n <= pref:
        return n
    t = pref - pref % 8
    while n % t:
        t -= 8
    return t


def _params(*sem):
    return pltpu.CompilerParams(dimension_semantics=sem, vmem_limit_bytes=VMEM_LIMIT)


def _mm(a, b):
    return jnp.dot(a.astype(MXU_DTYPE), b.astype(MXU_DTYPE), preferred_element_type=F32)


def _mm_nt(a, b):
    return lax.dot_general(a.astype(MXU_DTYPE), b.astype(MXU_DTYPE), (((1,), (1,)), ((), ())),
                           preferred_element_type=F32)


def _mm_tn(a, b):
    return lax.dot_general(a.astype(MXU_DTYPE), b.astype(MXU_DTYPE), (((0,), (0,)), ((), ())),
                           preferred_element_type=F32)


def _rstd(x, n):
    return lax.rsqrt(jnp.sum(x * x, axis=-1, keepdims=True) / n + EPS)


def _rms_bwd(dy, xhat, r, gain, n):
    dxh = dy * gain
    return r * (dxh - xhat * (jnp.sum(dxh * xhat, axis=-1, keepdims=True) / n))


def _colsum(x):
    return jnp.sum(x, axis=0, keepdims=True)


def _split3(x):
    hi = x.astype(MXU_DTYPE)
    r1 = x - hi.astype(F32)
    mid = r1.astype(MXU_DTYPE)
    lo = (r1 - mid.astype(F32)).astype(MXU_DTYPE)
    return jnp.concatenate([hi, mid, lo], axis=1)


def _sum3(x):
    n = x.shape[1] // 3
    return (x[:, :n] + x[:, n:2 * n]) + x[:, 2 * n:]


def _pool_mixed(prev, main, nxt, row0, seq):
    tm = main.shape[0]
    k = tm + 2 * HALO
    ext = jnp.concatenate([prev, main, nxt], axis=0)
    s_i = lax.broadcasted_iota(jnp.int32, (tm, k), 0) + row0
    t_j = lax.broadcasted_iota(jnp.int32, (tm, k), 1) + (row0 - HALO)
    s_v = lax.broadcasted_iota(jnp.int32, (tm, 1), 0) + row0
    inside = (t_j >= 0) & (t_j < seq)
    outs = []
    for g, w in enumerate(POOL_WINDOWS):
        left, right = w // 2, w - 1 - w // 2
        band = jnp.where((t_j >= s_i - left) & (t_j <= s_i + right) & inside, 1.0, 0.0).astype(MXU_DTYPE)
        sl = slice(g * LANES, (g + 1) * LANES)
        wsum = _sum3(jnp.dot(band, _split3(ext[:, sl]), preferred_element_type=F32))
        cnt = (jnp.minimum(s_v + right + 1, seq) - jnp.maximum(s_v - left, 0)).astype(F32)
        outs.append(wsum / cnt - main[:, sl])
    return jnp.concatenate(outs, axis=1)


def _pool_mixed_t(prev, main, nxt, row0, seq):
    tm = main.shape[0]
    k = tm + 2 * HALO
    ext = jnp.concatenate([prev, main, nxt], axis=0)
    t_i = lax.broadcasted_iota(jnp.int32, (tm, k), 0) + row0
    s_j = lax.broadcasted_iota(jnp.int32, (tm, k), 1) + (row0 - HALO)
    s_v = lax.broadcasted_iota(jnp.int32, (k, 1), 0) + (row0 - HALO)
    inside = (s_j >= 0) & (s_j < seq)
    inside_v = (s_v >= 0) & (s_v < seq)
    outs = []
    for g, w in enumerate(POOL_WINDOWS):
        left, right = w // 2, w - 1 - w // 2
        band = jnp.where((s_j >= t_i - right) & (s_j <= t_i + left) & inside, 1.0, 0.0).astype(MXU_DTYPE)
        sl = slice(g * LANES, (g + 1) * LANES)
        cnt = (jnp.minimum(s_v + right + 1, seq) - jnp.maximum(s_v - left, 0)).astype(F32)
        scaled = jnp.where(inside_v, ext[:, sl] / jnp.maximum(cnt, 1.0), 0.0)
        outs.append(_sum3(jnp.dot(band, _split3(scaled), preferred_element_type=F32)) - main[:, sl])
    return jnp.concatenate(outs, axis=1)


def _halo_specs(tm, width, seq):
    per = tm // HALO
    last = seq // HALO - 1
    prev = pl.BlockSpec((HALO, width), lambda i: (jnp.maximum(i * per - 1, 0), 0))
    nxt = pl.BlockSpec((HALO, width), lambda i: (jnp.minimum((i + 1) * per, last), 0))
    return prev, nxt


def _rope(x, c, s1, s2, half):
    return x * c + pltpu.roll(x, half, 1) * s1 + pltpu.roll(x, LANES - half, 1) * s2


def _rope_t(d, c, s1, s2, half):
    return d * c + pltpu.roll(d * s1, LANES - half, 1) + pltpu.roll(d * s2, half, 1)


def _rope_tables(seq, nope, rope):
    half = rope // 2
    pos = jnp.arange(seq, dtype=F32)
    inv = ROPE_THETA ** (-jnp.arange(0, rope, 2, dtype=F32) / rope)
    ang = pos[:, None] * inv[None, :]
    cos, sin = jnp.cos(ang), jnp.sin(ang)
    zeros = lambda n: jnp.zeros((seq, n), F32)
    ones = lambda n: jnp.ones((seq, n), F32)
    tail = LANES - nope - rope
    c = jnp.concatenate([ones(nope), cos, cos, ones(tail)], axis=1)
    s1 = jnp.concatenate([zeros(nope + half), sin, zeros(tail)], axis=1)
    s2 = jnp.concatenate([zeros(nope), -sin, zeros(half + tail)], axis=1)
    return c, s1, s2


def _ffn_fwd(x, gain, wgu, wd):
    seq, d = x.shape
    f = wd.shape[0]
    tm, tn = _tile(seq, FFN_TM), _tile(f, FFN_TN)
    nk = f // tn

    def body(x_ref, g_ref, wg_ref, wu_ref, wd_ref, o_ref, h_sc, acc_sc):
        k = pl.program_id(1)

        @pl.when(k == 0)
        def _():
            xv = x_ref[...]
            h_sc[...] = (xv * _rstd(xv, d) * g_ref[...]).astype(MXU_DTYPE)
            acc_sc[...] = jnp.zeros_like(acc_sc)

        h = h_sc[...]
        g = jnp.dot(h, wg_ref[...], preferred_element_type=F32)
        u = jnp.dot(h, wu_ref[...], preferred_element_type=F32)
        a = g * (1.0 / (1.0 + jnp.exp(-g))) * u
        acc_sc[...] += _mm(a, wd_ref[...])

        @pl.when(k == nk - 1)
        def _():
            o_ref[...] = x_ref[...] + 0.5 * acc_sc[...]

    return pl.pallas_call(
        body, name="ffn_fwd", grid=(seq // tm, nk),
        in_specs=[pl.BlockSpec((tm, d), lambda i, k: (i, 0)),
                  pl.BlockSpec((1, d), lambda i, k: (0, 0)),
                  pl.BlockSpec((d, tn), lambda i, k: (0, k)),
                  pl.BlockSpec((d, tn), lambda i, k: (0, k + nk)),
                  pl.BlockSpec((tn, d), lambda i, k: (k, 0))],
        out_specs=pl.BlockSpec((tm, d), lambda i, k: (i, 0)),
        out_shape=jax.ShapeDtypeStruct((seq, d), F32),
        scratch_shapes=[pltpu.VMEM((tm, d), MXU_DTYPE), pltpu.VMEM((tm, d), F32)],
        compiler_params=_params("parallel", "arbitrary"),
    )(x, gain, wgu, wgu, wd)


def _ffn_bwd(x, gain, wgu, wd, dout):
    seq, d = x.shape
    f = wd.shape[0]
    tm, tn = _tile(seq, FFN_BWD_TM), _tile(f, FFN_TN)
    nk = f // tn

    def body(x_ref, g_ref, wg_ref, wu_ref, wd_ref, do_ref, dx_ref, dgain_ref, h_ref, act_ref, dg_ref, du_ref,
             h_sc, dy_sc, acc_sc):
        i, k = pl.program_id(0), pl.program_id(1)

        @pl.when(k == 0)
        def _():
            xv = x_ref[...]
            hv = (xv * _rstd(xv, d) * g_ref[...]).astype(MXU_DTYPE)
            h_sc[...] = hv
            h_ref[...] = hv
            dy_sc[...] = (0.5 * do_ref[...]).astype(MXU_DTYPE)
            acc_sc[...] = jnp.zeros_like(acc_sc)

        @pl.when((k == 0) & (i == 0))
        def _():
            dgain_ref[...] = jnp.zeros_like(dgain_ref)

        h = h_sc[...]
        g = jnp.dot(h, wg_ref[...], preferred_element_type=F32)
        u = jnp.dot(h, wu_ref[...], preferred_element_type=F32)
        sig = 1.0 / (1.0 + jnp.exp(-g))
        silu = g * sig
        act_ref[...] = (silu * u).astype(MXU_DTYPE)
        da = _mm_nt(dy_sc[...], wd_ref[...])
        du = (da * silu).astype(MXU_DTYPE)
        dg = (da * u * (sig * (1.0 + g * (1.0 - sig)))).astype(MXU_DTYPE)
        dg_ref[...] = dg
        du_ref[...] = du
        acc_sc[...] += _mm_nt(dg, wg_ref[...]) + _mm_nt(du, wu_ref[...])

        @pl.when(k == nk - 1)
        def _():
            xv = x_ref[...]
            r = _rstd(xv, d)
            xhat = xv * r
            dh = acc_sc[...]
            dgain_ref[...] += _colsum(dh * xhat)
            dx_ref[...] = do_ref[...] + _rms_bwd(dh, xhat, r, g_ref[...], d)

    row = lambda i, k: (i, 0)
    col = lambda i, k: (i, k)
    return pl.pallas_call(
        body, name="ffn_bwd", grid=(seq // tm, nk),
        in_specs=[pl.BlockSpec((tm, d), row),
                  pl.BlockSpec((1, d), lambda i, k: (0, 0)),
                  pl.BlockSpec((d, tn), lambda i, k: (0, k)),
                  pl.BlockSpec((d, tn), lambda i, k: (0, k + nk)),
                  pl.BlockSpec((tn, d), lambda i, k: (k, 0)),
                  pl.BlockSpec((tm, d), row)],
        out_specs=[pl.BlockSpec((tm, d), row),
                   pl.BlockSpec((1, d), lambda i, k: (0, 0)),
                   pl.BlockSpec((tm, d), row),
                   pl.BlockSpec((tm, tn), col),
                   pl.BlockSpec((tm, tn), col),
                   pl.BlockSpec((tm, tn), col)],
        out_shape=[jax.ShapeDtypeStruct((seq, d), F32), jax.ShapeDtypeStruct((1, d), F32),
                   jax.ShapeDtypeStruct((seq, d), MXU_DTYPE), jax.ShapeDtypeStruct((seq, f), MXU_DTYPE),
                   jax.ShapeDtypeStruct((seq, f), MXU_DTYPE), jax.ShapeDtypeStruct((seq, f), MXU_DTYPE)],
        scratch_shapes=[pltpu.VMEM((tm, d), MXU_DTYPE), pltpu.VMEM((tm, d), MXU_DTYPE), pltpu.VMEM((tm, d), F32)],
        compiler_params=_params("arbitrary", "arbitrary"),
    )(x, gain, wgu, wgu, wd, dout)


def _wgrad(a, b, scale, bm, bn):
    seq, m = a.shape
    n = b.shape[1]
    bm, bn, bk = _tile(m, bm), _tile(n, bn), _tile(seq, WG_BK)
    ns = seq // bk

    def body(a_ref, b_ref, o_ref):
        @pl.when(pl.program_id(2) == 0)
        def _():
            o_ref[...] = jnp.zeros_like(o_ref)

        o_ref[...] += scale * _mm_tn(a_ref[...], b_ref[...])

    return pl.pallas_call(
        body, name="wgrad", grid=(m // bm, n // bn, ns),
        in_specs=[pl.BlockSpec((bk, bm), lambda i, j, s: (s, i)),
                  pl.BlockSpec((bk, bn), lambda i, j, s: (s, j))],
        out_specs=pl.BlockSpec((bm, bn), lambda i, j, s: (i, j)),
        out_shape=jax.ShapeDtypeStruct((m, n), F32),
        compiler_params=_params("parallel", "parallel", "arbitrary"),
    )(a, b)


class _Dims:
    def __init__(self, d, ql, kvl, heads, head_dim, nope, pool_w):
        self.d, self.ql, self.kvl, self.heads, self.head_dim, self.nope, self.pool_w = (
            d, ql, kvl, heads, head_dim, nope, pool_w)
        self.rope = head_dim - nope
        self.half = self.rope // 2
        self.hw = heads * LANES
        self.o_kv = ql
        self.o_pe = ql + kvl
        self.o_pool = ql + kvl + LANES
        self.zw = self.o_pool + pool_w
        self.scale = head_dim ** -0.5


def _mixin_fwd(dm, x, gmix, win, gql, gkvl, wuq, wuk, wuv, gq, gk, rc, rs1, rs2):
    seq, d = x.shape
    tm = _tile(seq, MIX_TM)

    def body(x_ref, gmix_ref, win_ref, gql_ref, gkvl_ref, wuq_ref, wuk_ref, wuv_ref, gq_ref, gk_ref,
             rc_ref, rs1_ref, rs2_ref, q_ref, k_ref, v_ref, zp_ref):
        xv = x_ref[...]
        z = _mm(xv * _rstd(xv, d) * gmix_ref[...], win_ref[...])
        cq, ckv = z[:, :dm.o_kv], z[:, dm.o_kv:dm.o_pe]
        kpe = z[:, dm.o_pe:dm.o_pool]
        zp_ref[...] = z[:, dm.o_pool:]
        cqn = cq * _rstd(cq, dm.ql) * gql_ref[...]
        ckvn = ckv * _rstd(ckv, dm.kvl) * gkvl_ref[...]
        q = _mm(cqn, wuq_ref[...])
        kn = _mm(ckvn, wuk_ref[...])
        v_ref[...] = _mm(ckvn, wuv_ref[...]).astype(MXU_DTYPE)
        c, s1, s2 = rc_ref[...], rs1_ref[...], rs2_ref[...]
        for h in range(dm.heads):
            sl = slice(h * LANES, (h + 1) * LANES)
            qh = q[:, sl]
            qn = qh * _rstd(qh, dm.head_dim) * gq_ref[...]
            q_ref[:, sl] = (_rope(qn, c, s1, s2, dm.half) * dm.scale).astype(MXU_DTYPE)
            kh = kn[:, sl] + kpe
            kk = kh * _rstd(kh, dm.head_dim) * gk_ref[...]
            k_ref[:, sl] = _rope(kk, c, s1, s2, dm.half).astype(MXU_DTYPE)

    row = lambda i: (i, 0)
    full = lambda a: pl.BlockSpec(a.shape, lambda i: (0,) * a.ndim)
    return pl.pallas_call(
        body, name="mixin_fwd", grid=(seq // tm,),
        in_specs=[pl.BlockSpec((tm, d), row), full(gmix), full(win), full(gql), full(gkvl), full(wuq), full(wuk),
                  full(wuv), full(gq), full(gk),
                  pl.BlockSpec((tm, LANES), row), pl.BlockSpec((tm, LANES), row), pl.BlockSpec((tm, LANES), row)],
        out_specs=[pl.BlockSpec((tm, dm.hw), row), pl.BlockSpec((tm, dm.hw), row), pl.BlockSpec((tm, dm.hw), row),
                   pl.BlockSpec((tm, dm.pool_w), row)],
        out_shape=[jax.ShapeDtypeStruct((seq, dm.hw), MXU_DTYPE), jax.ShapeDtypeStruct((seq, dm.hw), MXU_DTYPE),
                   jax.ShapeDtypeStruct((seq, dm.hw), MXU_DTYPE), jax.ShapeDtypeStruct((seq, dm.pool_w), F32)],
        compiler_params=_params("parallel"),
    )(x, gmix, win, gql, gkvl, wuq, wuk, wuv, gq, gk, rc, rs1, rs2)


def _mla_in_bwd(dm, x, gmix, win, gql, gkvl, wuq, wuk, wuv, gq, gk, rc, rs1, rs2, dq, dk, dv, dmixed):
    seq, d = x.shape
    tm = _tile(seq, MIX_TM)

    def body(x_ref, gmix_ref, win_ref, gql_ref, gkvl_ref, wuq_ref, wuk_ref, wuv_ref, gq_ref, gk_ref,
             rc_ref, rs1_ref, rs2_ref, dq_ref, dk_ref, dv_ref, dmp_ref, dm_ref, dmn_ref,
             dz_ref, dwuq_ref, dwuk_ref, dwuv_ref, dgql_ref, dgkvl_ref, dgq_ref, dgk_ref):
        i = pl.program_id(0)

        @pl.when(i == 0)
        def _():
            for ref in (dwuq_ref, dwuk_ref, dwuv_ref, dgql_ref, dgkvl_ref, dgq_ref, dgk_ref):
                ref[...] = jnp.zeros_like(ref)

        xv = x_ref[...]
        z = _mm(xv * _rstd(xv, d) * gmix_ref[...], win_ref[...])
        cq, ckv = z[:, :dm.o_kv], z[:, dm.o_kv:dm.o_pe]
        kpe = z[:, dm.o_pe:dm.o_pool]
        r_q, r_kv = _rstd(cq, dm.ql), _rstd(ckv, dm.kvl)
        cqh, ckvh = cq * r_q, ckv * r_kv
        cqn = (cqh * gql_ref[...]).astype(MXU_DTYPE)
        ckvn = (ckvh * gkvl_ref[...]).astype(MXU_DTYPE)
        q = _mm(cqn, wuq_ref[...])
        kn = _mm(ckvn, wuk_ref[...])
        c, s1, s2 = rc_ref[...], rs1_ref[...], rs2_ref[...]
        dq_pre, dk_pre = [], []
        dkpe = jnp.zeros((tm, LANES), F32)
        dgq = jnp.zeros((1, LANES), F32)
        dgk = jnp.zeros((1, LANES), F32)
        for h in range(dm.heads):
            sl = slice(h * LANES, (h + 1) * LANES)
            qh = q[:, sl]
            rq = _rstd(qh, dm.head_dim)
            xq = qh * rq
            dqn = _rope_t(dq_ref[:, sl] * dm.scale, c, s1, s2, dm.half)
            dgq += _colsum(dqn * xq)
            dq_pre.append(_rms_bwd(dqn, xq, rq, gq_ref[...], dm.head_dim))
            kh = kn[:, sl] + kpe
            rk = _rstd(kh, dm.head_dim)
            xk = kh * rk
            dkn = _rope_t(dk_ref[:, sl], c, s1, s2, dm.half)
            dgk += _colsum(dkn * xk)
            dkh = _rms_bwd(dkn, xk, rk, gk_ref[...], dm.head_dim)
            dk_pre.append(dkh)
            dkpe += dkh
        dgq_ref[...] += dgq
        dgk_ref[...] += dgk
        dq_pre = jnp.concatenate(dq_pre, axis=1).astype(MXU_DTYPE)
        dk_pre = jnp.concatenate(dk_pre, axis=1).astype(MXU_DTYPE)
        dvv = dv_ref[...].astype(MXU_DTYPE)
        dwuq_ref[...] += _mm_tn(cqn, dq_pre)
        dwuk_ref[...] += _mm_tn(ckvn, dk_pre)
        dwuv_ref[...] += _mm_tn(ckvn, dvv)
        dcqn = _mm_nt(dq_pre, wuq_ref[...])
        dckvn = _mm_nt(dk_pre, wuk_ref[...]) + _mm_nt(dvv, wuv_ref[...])
        dgql_ref[...] += _colsum(dcqn * cqh)
        dgkvl_ref[...] += _colsum(dckvn * ckvh)
        dcq = _rms_bwd(dcqn, cqh, r_q, gql_ref[...], dm.ql)
        dckv = _rms_bwd(dckvn, ckvh, r_kv, gkvl_ref[...], dm.kvl)
        dzp = _pool_mixed_t(dmp_ref[...], dm_ref[...], dmn_ref[...], i * tm, seq)
        dz_ref[...] = jnp.concatenate([dcq, dckv, dkpe, dzp], axis=1).astype(MXU_DTYPE)

    row = lambda i: (i, 0)
    full = lambda a: pl.BlockSpec(a.shape, lambda i: (0,) * a.ndim)
    acc = lambda shape: pl.BlockSpec(shape, lambda i: (0, 0))
    prev, nxt = _halo_specs(tm, dm.pool_w, seq)
    shapes = [(seq, dm.zw), wuq.shape, wuk.shape, wuv.shape, (1, dm.ql), (1, dm.kvl), (1, LANES), (1, LANES)]
    return pl.pallas_call(
        body, name="mla_in_bwd", grid=(seq // tm,),
        in_specs=[pl.BlockSpec((tm, d), row), full(gmix), full(win), full(gql), full(gkvl), full(wuq), full(wuk),
                  full(wuv), full(gq), full(gk),
                  pl.BlockSpec((tm, LANES), row), pl.BlockSpec((tm, LANES), row), pl.BlockSpec((tm, LANES), row),
                  pl.BlockSpec((tm, dm.hw), row), pl.BlockSpec((tm, dm.hw), row), pl.BlockSpec((tm, dm.hw), row),
                  prev, pl.BlockSpec((tm, dm.pool_w), row), nxt],
        out_specs=[pl.BlockSpec((tm, dm.zw), row)] + [acc(s) for s in shapes[1:]],
        out_shape=[jax.ShapeDtypeStruct(shapes[0], MXU_DTYPE)] + [jax.ShapeDtypeStruct(s, F32) for s in shapes[1:]],
        compiler_params=_params("arbitrary"),
    )(x, gmix, win, gql, gkvl, wuq, wuk, wuv, gq, gk, rc, rs1, rs2, dq, dk, dv, dmixed, dmixed, dmixed)


def _rms_proj_bwd(x, gain, w, dz, gin):
    seq, d = x.shape
    n = w.shape[1]
    tm = _tile(seq, MIX_TM)

    def body(x_ref, g_ref, w_ref, dz_ref, gin_ref, gout_ref, dw_ref, dgain_ref):
        @pl.when(pl.program_id(0) == 0)
        def _():
            dw_ref[...] = jnp.zeros_like(dw_ref)
            dgain_ref[...] = jnp.zeros_like(dgain_ref)

        xv = x_ref[...]
        r = _rstd(xv, d)
        xhat = xv * r
        dzv = dz_ref[...]
        dh = _mm_nt(dzv, w_ref[...])
        dw_ref[...] += _mm_tn(xhat * g_ref[...], dzv)
        dgain_ref[...] += _colsum(dh * xhat)
        gout_ref[...] = gin_ref[...] + _rms_bwd(dh, xhat, r, g_ref[...], d)

    row = lambda i: (i, 0)
    return pl.pallas_call(
        body, name="rms_proj_bwd", grid=(seq // tm,),
        in_specs=[pl.BlockSpec((tm, d), row), pl.BlockSpec((1, d), lambda i: (0, 0)),
                  pl.BlockSpec((d, n), lambda i: (0, 0)), pl.BlockSpec((tm, n), row), pl.BlockSpec((tm, d), row)],
        out_specs=[pl.BlockSpec((tm, d), row), pl.BlockSpec((d, n), lambda i: (0, 0)),
                   pl.BlockSpec((1, d), lambda i: (0, 0))],
        out_shape=[jax.ShapeDtypeStruct((seq, d), F32), jax.ShapeDtypeStruct((d, n), F32),
                   jax.ShapeDtypeStruct((1, d), F32)],
        compiler_params=_params("arbitrary"),
    )(x, gain, w, dz, gin)


def _pool_branch(mixed, wpool_ref):
    groups = mixed.shape[1] // LANES
    return jnp.concatenate(
        [_mm(mixed[:, g * LANES:(g + 1) * LANES], wpool_ref[g]) for g in range(groups)], axis=1)


def _mixout_fwd(dm, x, zp, o, wo, wpool, pscale):
    seq, d = x.shape
    tm = _tile(seq, MIX_TM)

    def body(x_ref, zpp_ref, zp_ref, zpn_ref, o_ref, wo_ref, wpool_ref, ps_ref, out_ref):
        mixed = _pool_mixed(zpp_ref[...], zp_ref[...], zpn_ref[...], pl.program_id(0) * tm, seq)
        b = _pool_branch(mixed, wpool_ref) * ps_ref[...]
        out_ref[...] = x_ref[...] + _mm(o_ref[...], wo_ref[:dm.hw, :]) + _mm(b, wo_ref[dm.hw:, :])

    row = lambda i: (i, 0)
    full = lambda a: pl.BlockSpec(a.shape, lambda i: (0,) * a.ndim)
    prev, nxt = _halo_specs(tm, dm.pool_w, seq)
    return pl.pallas_call(
        body, name="mixout_fwd", grid=(seq // tm,),
        in_specs=[pl.BlockSpec((tm, d), row), prev, pl.BlockSpec((tm, dm.pool_w), row), nxt,
                  pl.BlockSpec((tm, dm.hw), row), full(wo), full(wpool), full(pscale)],
        out_specs=pl.BlockSpec((tm, d), row),
        out_shape=jax.ShapeDtypeStruct((seq, d), F32),
        compiler_params=_params("parallel"),
    )(x, zp, zp, zp, o, wo, wpool, pscale)


def _mixout_bwd(dm, g, zp, o, wo, wpool, pscale):
    seq, d = g.shape
    tm = _tile(seq, MIX_TM)
    groups = dm.pool_w // LANES

    def body(g_ref, zpp_ref, zp_ref, zpn_ref, o_ref, wo_ref, wpool_ref, ps_ref,
             do_ref, delta_ref, dmixed_ref, dwo_ref, dwpool_ref, dps_ref):
        i = pl.program_id(0)

        @pl.when(i == 0)
        def _():
            for ref in (dwo_ref, dwpool_ref, dps_ref):
                ref[...] = jnp.zeros_like(ref)

        gv = g_ref[...].astype(MXU_DTYPE)
        ov = o_ref[...]
        dcat = _mm_nt(gv, wo_ref[...])
        da, db = dcat[:, :dm.hw], dcat[:, dm.hw:]
        mixed = _pool_mixed(zpp_ref[...], zp_ref[...], zpn_ref[...], i * tm, seq).astype(MXU_DTYPE)
        y = _pool_branch(mixed, wpool_ref)
        b = (y * ps_ref[...]).astype(MXU_DTYPE)
        dwo_ref[:dm.hw, :] += _mm_tn(ov, gv)
        dwo_ref[dm.hw:, :] += _mm_tn(b, gv)
        dps_ref[...] += _colsum(db * y)
        dy = (db * ps_ref[...]).astype(MXU_DTYPE)
        dmx = []
        for gi in range(groups):
            sl = slice(gi * LANES, (gi + 1) * LANES)
            dmx.append(_mm_nt(dy[:, sl], wpool_ref[gi]))
            dwpool_ref[gi] += _mm_tn(mixed[:, sl], dy[:, sl])
        dmixed_ref[...] = jnp.concatenate(dmx, axis=1)
        do_ref[...] = da.astype(MXU_DTYPE)
        prod = da * ov.astype(F32)
        for h in range(dm.heads):
            sl = slice(h * LANES, (h + 1) * LANES)
            delta_ref[:, sl] = jnp.broadcast_to(jnp.sum(prod[:, sl], axis=-1, keepdims=True), (tm, LANES))

    row = lambda i: (i, 0)
    full = lambda a: pl.BlockSpec(a.shape, lambda i: (0,) * a.ndim)
    prev, nxt = _halo_specs(tm, dm.pool_w, seq)
    return pl.pallas_call(
        body, name="mixout_bwd", grid=(seq // tm,),
        in_specs=[pl.BlockSpec((tm, d), row), prev, pl.BlockSpec((tm, dm.pool_w), row), nxt,
                  pl.BlockSpec((tm, dm.hw), row), full(wo), full(wpool), full(pscale)],
        out_specs=[pl.BlockSpec((tm, dm.hw), row), pl.BlockSpec((tm, dm.hw), row), pl.BlockSpec((tm, dm.pool_w), row),
                   full(wo), full(wpool), full(pscale)],
        out_shape=[jax.ShapeDtypeStruct((seq, dm.hw), MXU_DTYPE), jax.ShapeDtypeStruct((seq, dm.hw), F32),
                   jax.ShapeDtypeStruct((seq, dm.pool_w), F32), jax.ShapeDtypeStruct(wo.shape, F32),
                   jax.ShapeDtypeStruct(wpool.shape, F32), jax.ShapeDtypeStruct(pscale.shape, F32)],
        compiler_params=_params("arbitrary"),
    )(g, zp, zp, zp, o, wo, wpool, pscale)


def _attn_fwd(q, k, v, heads):
    seq = q.shape[0]
    tq, tk = _tile(seq, ATTN_TQ), _tile(seq, ATTN_TK)
    nj = seq // tk

    def body(q_ref, k_ref, v_ref, o_ref, lse_ref, m_sc, l_sc, acc_sc):
        j = pl.program_id(2)

        @pl.when(j == 0)
        def _():
            m_sc[...] = jnp.full_like(m_sc, -jnp.inf)
            l_sc[...] = jnp.zeros_like(l_sc)
            acc_sc[...] = jnp.zeros_like(acc_sc)

        s = _mm_nt(q_ref[...], k_ref[...])
        m_prev = m_sc[...]
        m_new = jnp.maximum(m_prev, jnp.max(s, axis=-1, keepdims=True))
        alpha = jnp.exp(m_prev - m_new)
        p = jnp.exp(s - m_new[:, :1])
        l_sc[...] = alpha * l_sc[...] + jnp.sum(p, axis=-1, keepdims=True)
        acc_sc[...] = alpha * acc_sc[...] + _mm(p, v_ref[...])
        m_sc[...] = m_new

        @pl.when(j == nj - 1)
        def _():
            o_ref[...] = (acc_sc[...] / l_sc[...]).astype(MXU_DTYPE)
            lse_ref[...] = m_sc[...] + jnp.log(l_sc[...])

    qspec = pl.BlockSpec((tq, LANES), lambda h, i, j: (i, h))
    kspec = pl.BlockSpec((tk, LANES), lambda h, i, j: (j, h))
    return pl.pallas_call(
        body, name="attn_fwd", grid=(heads, seq // tq, nj),
        in_specs=[qspec, kspec, kspec],
        out_specs=[qspec, qspec],
        out_shape=[jax.ShapeDtypeStruct(q.shape, MXU_DTYPE), jax.ShapeDtypeStruct(q.shape, F32)],
        scratch_shapes=[pltpu.VMEM((tq, LANES), F32), pltpu.VMEM((tq, LANES), F32), pltpu.VMEM((tq, LANES), F32)],
        compiler_params=_params("parallel", "parallel", "arbitrary"),
    )(q, k, v)


def _attn_bwd(q, k, v, do, lse, delta, heads):
    seq = q.shape[0]
    tq, tk = _tile(seq, ATTN_TQ), _tile(seq, ATTN_TK)
    ni = seq // tq

    def body(q_ref, k_ref, v_ref, do_ref, lse_ref, delta_ref, dq_ref, dk_ref, dv_ref, dk_sc, dv_sc):
        j, i = pl.program_id(1), pl.program_id(2)

        @pl.when(i == 0)
        def _():
            dk_sc[...] = jnp.zeros_like(dk_sc)
            dv_sc[...] = jnp.zeros_like(dv_sc)

        qv, kv, dov = q_ref[...], k_ref[...], do_ref[...]
        p = jnp.exp(_mm_nt(qv, kv) - lse_ref[:, :1])
        dp = _mm_nt(dov, v_ref[...])
        ds = (p * (dp - delta_ref[:, :1])).astype(MXU_DTYPE)
        dv_sc[...] += _mm_tn(p, dov)
        dk_sc[...] += _mm_tn(ds, qv)
        dq_part = _mm(ds, kv)
        rows = pl.ds(pl.multiple_of(i * tq, tq), tq)

        @pl.when(j == 0)
        def _():
            dq_ref[rows, :] = dq_part

        @pl.when(j > 0)
        def _():
            dq_ref[rows, :] += dq_part

        @pl.when(i == ni - 1)
        def _():
            dk_ref[...] = dk_sc[...]
            dv_ref[...] = dv_sc[...]

    qspec = pl.BlockSpec((tq, LANES), lambda h, j, i: (i, h))
    kspec = pl.BlockSpec((tk, LANES), lambda h, j, i: (j, h))
    return pl.pallas_call(
        body, name="attn_bwd", grid=(heads, seq // tk, ni),
        in_specs=[qspec, kspec, kspec, qspec, qspec, qspec],
        out_specs=[pl.BlockSpec((seq, LANES), lambda h, j, i: (0, h)), kspec, kspec],
        out_shape=[jax.ShapeDtypeStruct(q.shape, F32)] * 3,
        scratch_shapes=[pltpu.VMEM((tk, LANES), F32), pltpu.VMEM((tk, LANES), F32)],
        compiler_params=_params("parallel", "arbitrary", "arbitrary"),
    )(q, k, v, do, lse, delta)


def _loss_head(y, target):
    seq, d = y.shape
    tm = _tile(seq, FFN_TM)

    def body(y_ref, t_ref, part_ref, dy_ref):
        @pl.when(pl.program_id(0) == 0)
        def _():
            part_ref[...] = jnp.zeros_like(part_ref)

        err = y_ref[...] - t_ref[...]
        part_ref[...] += _colsum(err * err)
        dy_ref[...] = err / d

    row = lambda i: (i, 0)
    return pl.pallas_call(
        body, name="loss_head", grid=(seq // tm,),
        in_specs=[pl.BlockSpec((tm, d), row), pl.BlockSpec((tm, d), row)],
        out_specs=[pl.BlockSpec((1, d), lambda i: (0, 0)), pl.BlockSpec((tm, d), row)],
        out_shape=[jax.ShapeDtypeStruct((1, d), F32), jax.ShapeDtypeStruct((seq, d), F32)],
        compiler_params=_params("arbitrary"),
    )(y, target)


def _my_place():
    return lax.axis_index("x"), lax.axis_index("y"), lax.axis_index("c")


def _all_gather(shard):
    def body(x_ref, out_ref, send_sems, recv_sems, local_sem):
        x, y, c = _my_place()
        me, sibling = (x, y, c), (x, y, 1 - c)
        chips = [(1 - x, y), (x, 1 - y), (1 - x, 1 - y)]

        def slot(px, py, pc):
            return out_ref.at[4 * px + 2 * py + pc]

        def copy(k, block, to, src=None):
            return pltpu.make_async_remote_copy(
                src_ref=slot(*block) if src is None else src, dst_ref=slot(*block),
                send_sem=send_sems.at[k], recv_sem=recv_sems.at[k], device_id=to, device_id_type=MESH)

        mine = pltpu.make_async_copy(x_ref, slot(*me), local_sem)
        mine.start()
        first = [copy(0, me, sibling, src=x_ref)]
        first += [copy(1 + j, me, (*chip, c), src=x_ref) for j, chip in enumerate(chips)]
        for cp in first:
            cp.start()
        passed = [copy(4 + j, (*chip, c), sibling) for j, chip in enumerate(chips)]
        for j, chip in enumerate(chips):
            copy(1 + j, (*chip, c), me).wait_recv()
            passed[j].start()
        copy(0, sibling, me).wait_recv()
        for j, chip in enumerate(chips):
            copy(4 + j, (*chip, 1 - c), me).wait_recv()
        for cp in first + passed:
            cp.wait_send()
        mine.wait()

    return pl.pallas_call(
        body, name="weights_all_gather",
        out_shape=jax.ShapeDtypeStruct((N_DEV,) + shard.shape, shard.dtype),
        in_specs=[pl.BlockSpec(memory_space=pl.ANY)],
        out_specs=pl.BlockSpec(memory_space=pl.ANY),
        scratch_shapes=[pltpu.SemaphoreType.DMA((7,)), pltpu.SemaphoreType.DMA((7,)), pltpu.SemaphoreType.DMA],
    )(shard)


def _grad_exchange(blocks):
    def body(g_ref, out_ref, send_sems, recv_sems, local_sem):
        x, y, c = _my_place()
        me = 4 * x + 2 * y + c
        mine = pltpu.make_async_copy(g_ref.at[me], out_ref.at[me], local_sem)
        mine.start()
        copies = []
        for k in range(1, N_DEV):
            px = 1 - x if k & 4 else x
            py = 1 - y if k & 2 else y
            pc = 1 - c if k & 1 else c
            copies.append(pltpu.make_async_remote_copy(
                src_ref=g_ref.at[4 * px + 2 * py + pc], dst_ref=out_ref.at[me],
                send_sem=send_sems.at[k - 1], recv_sem=recv_sems.at[k - 1],
                device_id=(px, py, pc), device_id_type=MESH))
        for cp in copies:
            cp.start()
        for cp in copies:
            cp.wait_recv()
        for cp in copies:
            cp.wait_send()
        mine.wait()

    return pl.pallas_call(
        body, name="grad_exchange",
        out_shape=jax.ShapeDtypeStruct(blocks.shape, blocks.dtype),
        in_specs=[pl.BlockSpec(memory_space=pl.ANY)],
        out_specs=pl.BlockSpec(memory_space=pl.ANY),
        scratch_shapes=[pltpu.SemaphoreType.DMA((7,)), pltpu.SemaphoreType.DMA((7,)), pltpu.SemaphoreType.DMA],
    )(blocks)


def _adamw(parts, w, m, v):
    rows, width = w.shape
    tr = _tile(rows, ADAM_TR)

    def body(p_ref, w_ref, m_ref, v_ref, g_ref, d_ref, nm_ref, nv_ref):
        g = p_ref[0]
        for s in range(1, N_DEV):
            g = g + p_ref[s]
        nm = ADAM_B1 * m_ref[...] + (1.0 - ADAM_B1) * g
        nv = ADAM_B2 * v_ref[...] + (1.0 - ADAM_B2) * (g * g)
        m_hat = nm / (1.0 - ADAM_B1 ** ADAM_STEP)
        v_hat = nv / (1.0 - ADAM_B2 ** ADAM_STEP)
        g_ref[...] = g
        d_ref[...] = -ADAM_LR * (m_hat / (jnp.sqrt(v_hat) + ADAM_EPS) + ADAM_WD * w_ref[...])
        nm_ref[...] = nm
        nv_ref[...] = nv

    row = pl.BlockSpec((tr, width), lambda i: (i, 0))
    return pl.pallas_call(
        body, name="adamw", grid=(rows // tr,),
        in_specs=[pl.BlockSpec((N_DEV, tr, width), lambda i: (0, i, 0)), row, row, row],
        out_specs=[row] * 4,
        out_shape=[jax.ShapeDtypeStruct(w.shape, F32)] * 4,
        compiler_params=_params("parallel"),
    )(parts, w, m, v)


def _pack_rows(flat_parts, multiple):
    flat = jnp.concatenate([p.reshape(-1) for p in flat_parts])
    chunk = multiple * PACK_W
    pad = (-flat.shape[0]) % chunk
    if pad:
        flat = jnp.concatenate([flat, jnp.zeros((pad,), flat.dtype)])
    return flat.reshape(-1, PACK_W)


def _unpack(packed, shapes):
    lead = packed.shape[:-2]
    flat = packed.reshape(lead + (-1,))
    out, off = [], 0
    for shape in shapes:
        size = 1
        for s in shape:
            size *= s
        out.append(flat[..., off:off + size].reshape(lead + tuple(shape)))
        off += size
    return out


def _to_full(name, g):
    n, l, a, b = g.shape
    if name in ROW_SHARDED:
        return jnp.transpose(g, (1, 0, 2, 3)).reshape(l, n * a, b)
    return jnp.transpose(g, (1, 2, 0, 3)).reshape(l, a, n * b)


def _to_shards(name, full):
    l, a, b = full.shape
    if name in ROW_SHARDED:
        return jnp.transpose(full.reshape(l, N_DEV, a // N_DEV, b), (1, 0, 2, 3))
    return jnp.transpose(full.reshape(l, a, N_DEV, b // N_DEV), (2, 0, 1, 3))


def _pad_heads(w, heads, real):
    lead = w.shape[:-1]
    w = w.reshape(lead + (heads, real))
    w = jnp.concatenate([w, jnp.zeros(lead + (heads, LANES - real), w.dtype)], axis=-1)
    return w.reshape(lead + (heads * LANES,))


def _unpad_heads(w, heads, real):
    lead = w.shape[:-1]
    return w.reshape(lead + (heads, LANES))[..., :real].reshape(lead + (heads * real,))


def _pad_lanes(v, before):
    l, n = v.shape
    return jnp.concatenate([jnp.zeros((l, before), v.dtype), v, jnp.zeros((l, LANES - before - n), v.dtype)], axis=1)


def kernel(x, ffn1_norm, ffn1_w_gu, ffn1_w_down, mix_norm, w_in, q_lat_norm, kv_lat_norm, w_uq, w_uk, w_uv, q_norm, k_norm, w_pool, pool_scale, w_out, ffn2_norm, ffn2_w_gu, ffn2_w_down, loss_target, m_ffn1_norm, m_ffn1_w_gu, m_ffn1_w_down, m_mix_norm, m_w_in, m_q_lat_norm, m_kv_lat_norm, m_w_uq, m_w_uk, m_w_uv, m_q_norm, m_k_norm, m_w_pool, m_pool_scale, m_w_out, m_ffn2_norm, m_ffn2_w_gu, m_ffn2_w_down, v_ffn1_norm, v_ffn1_w_gu, v_ffn1_w_down, v_mix_norm, v_w_in, v_q_lat_norm, v_kv_lat_norm, v_w_uq, v_w_uk, v_w_uv, v_q_norm, v_k_norm, v_w_pool, v_pool_scale, v_w_out, v_ffn2_norm, v_ffn2_w_gu, v_ffn2_w_down):
    given = dict(locals())
    wts = {n: given[n] for n in WEIGHTS}
    mom1 = {n: given["m_" + n] for n in WEIGHTS}
    mom2 = {n: given["v_" + n] for n in WEIGHTS}

    depth, d = ffn1_norm.shape
    seq = x.shape[1]
    dff = ffn1_w_down.shape[1] * N_DEV
    ql, kvl, head_dim = q_lat_norm.shape[1], kv_lat_norm.shape[1], q_norm.shape[1]
    heads = w_uq.shape[2] * N_DEV // head_dim
    nope = w_uk.shape[2] * N_DEV // heads
    vh = w_uv.shape[2] * N_DEV // heads
    groups, gdim = w_pool.shape[1], w_pool.shape[2]
    pool_w = groups * gdim
    assert gdim == LANES and groups == len(POOL_WINDOWS) and head_dim <= LANES and vh <= LANES
    assert d % LANES == 0 and ql % LANES == 0 and kvl % LANES == 0 and seq % HALO == 0
    dm = _Dims(d, ql, kvl, heads, head_dim, nope, pool_w)

    shard_shapes = [wts[n].shape for n in SHARDED]
    gathered = _all_gather(_pack_rows([wts[n].astype(MXU_DTYPE) for n in SHARDED], 16))
    full = {n: _to_full(n, g) for n, g in zip(SHARDED, _unpack(gathered, shard_shapes))}

    zpad = jnp.zeros((depth, d, LANES), MXU_DTYPE)
    win_p = jnp.concatenate(
        [full["w_in"][..., :dm.o_pe], zpad[..., :nope], full["w_in"][..., dm.o_pe:dm.o_pe + dm.rope],
         zpad[..., :LANES - nope - dm.rope], full["w_in"][..., dm.o_pe + dm.rope:]], axis=-1)
    wuq_p = _pad_heads(full["w_uq"], heads, head_dim)
    wuk_p = _pad_heads(full["w_uk"], heads, nope)
    wuv_p = _pad_heads(full["w_uv"], heads, vh)
    wo_a = full["w_out"][:, :heads * vh].reshape(depth, heads, vh, d)
    wo_a = jnp.concatenate([wo_a, jnp.zeros((depth, heads, LANES - vh, d), MXU_DTYPE)], axis=2)
    wo_p = jnp.concatenate([wo_a.reshape(depth, dm.hw, d), full["w_out"][:, heads * vh:]], axis=1)
    gq_p, gk_p = _pad_lanes(q_norm, 0), _pad_lanes(k_norm, 0)
    wpool_c = w_pool.astype(MXU_DTYPE)
    rc, rs1, rs2 = _rope_tables(seq, nope, dm.rope)
    row = lambda a, l: a[l][None, :]

    h = x[0]
    saved = []
    for l in range(depth):
        x0 = h
        x1 = _ffn_fwd(x0, row(ffn1_norm, l), full["ffn1_w_gu"][l], full["ffn1_w_down"][l])
        q, k, v, zp = _mixin_fwd(dm, x1, row(mix_norm, l), win_p[l], row(q_lat_norm, l), row(kv_lat_norm, l),
                                 wuq_p[l], wuk_p[l], wuv_p[l], row(gq_p, l), row(gk_p, l), rc, rs1, rs2)
        o, lse = _attn_fwd(q, k, v, heads)
        x2 = _mixout_fwd(dm, x1, zp, o, wo_p[l], wpool_c[l], row(pool_scale, l))
        h = _ffn_fwd(x2, row(ffn2_norm, l), full["ffn2_w_gu"][l], full["ffn2_w_down"][l])
        saved.append((x0, x1, x2, q, k, v, zp, o, lse))

    part, g = _loss_head(h, loss_target[0])
    loss = lax.psum(0.5 / d * jnp.sum(part), ("x", "y", "c"))

    grads = {n: [None] * depth for n in WEIGHTS}

    def ffn_grads(prefix, l, xin, gout):
        wgu, wd = full[prefix + "_w_gu"][l], full[prefix + "_w_down"][l]
        gin, dgain, hh, act, dg, du = _ffn_bwd(xin, row(given[prefix + "_norm"], l), wgu, wd, gout)
        grads[prefix + "_norm"][l] = dgain[0]
        grads[prefix + "_w_gu"][l] = jnp.concatenate(
            [_wgrad(hh, dg, 1.0, 1024, 1408), _wgrad(hh, du, 1.0, 1024, 1408)], axis=1)
        grads[prefix + "_w_down"][l] = _wgrad(act, gout, 0.5, 1408, 1024)
        return gin

    for l in reversed(range(depth)):
        x0, x1, x2, q, k, v, zp, o, lse = saved[l]
        g = ffn_grads("ffn2", l, x2, g)
        do, delta, dmixed, dwo, dwpool, dps = _mixout_bwd(dm, g, zp, o, wo_p[l], wpool_c[l], row(pool_scale, l))
        dq, dk, dv = _attn_bwd(q, k, v, do, lse, delta, heads)
        dz, dwuq, dwuk, dwuv, dgql, dgkvl, dgq, dgk = _mla_in_bwd(
            dm, x1, row(mix_norm, l), win_p[l], row(q_lat_norm, l), row(kv_lat_norm, l), wuq_p[l], wuk_p[l],
            wuv_p[l], row(gq_p, l), row(gk_p, l), rc, rs1, rs2, dq, dk, dv, dmixed)
        g, dwin, dgmix = _rms_proj_bwd(x1, row(mix_norm, l), win_p[l], dz, g)
        grads["w_out"][l] = jnp.concatenate(
            [dwo[:dm.hw].reshape(heads, LANES, d)[:, :vh].reshape(heads * vh, d), dwo[dm.hw:]], axis=0)
        grads["w_pool"][l] = dwpool
        grads["pool_scale"][l] = dps[0]
        grads["w_uq"][l] = _unpad_heads(dwuq, heads, head_dim)
        grads["w_uk"][l] = _unpad_heads(dwuk, heads, nope)
        grads["w_uv"][l] = _unpad_heads(dwuv, heads, vh)
        grads["q_lat_norm"][l] = dgql[0]
        grads["kv_lat_norm"][l] = dgkvl[0]
        grads["q_norm"][l] = dgq[0, :head_dim]
        grads["k_norm"][l] = dgk[0, :head_dim]
        grads["w_in"][l] = jnp.concatenate(
            [dwin[:, :dm.o_pe], dwin[:, dm.o_pe + nope:dm.o_pe + nope + dm.rope], dwin[:, dm.o_pool:]], axis=1)
        grads["mix_norm"][l] = dgmix[0]
        g = ffn_grads("ffn1", l, x0, g)

    grads = {n: jnp.stack(grads[n]) for n in WEIGHTS}

    shard_parts = [_to_shards(n, grads[n]).reshape(N_DEV, -1) for n in SHARDED]
    repl = jnp.concatenate([grads[n].reshape(-1) for n in REPLICATED])
    flat = jnp.concatenate(shard_parts + [jnp.broadcast_to(repl[None], (N_DEV, repl.shape[0]))], axis=1)
    pad = (-flat.shape[1]) % (ADAM_TR * PACK_W)
    flat = jnp.concatenate([flat, jnp.zeros((N_DEV, pad), F32)], axis=1)
    received = _grad_exchange(flat.reshape(N_DEV, -1, PACK_W))

    order = SHARDED + REPLICATED
    packed = [_pack_rows([src[n] for n in order], ADAM_TR) for src in (wts, mom1, mom2)]
    results = _adamw(received, *packed)
    shapes = [wts[n].shape for n in order]
    outs = [dict(zip(order, _unpack(r, shapes))) for r in results]

    return (loss, g[None], *[o[n] for o in outs for n in WEIGHTS])
```

```python
import functools

import jax
import jax.numpy as jnp
from jax import lax
from jax.experimental import pallas as pl
from jax.experimental.pallas import tpu as pltpu

F32 = jnp.float32
MXU_DTYPE = jnp.bfloat16
WIRE_DTYPE = jnp.bfloat16
EPS = 1e-6
ROPE_THETA = 10000.0
POOL_WINDOWS = (2, 4, 8, 16)
ADAM_LR, ADAM_B1, ADAM_B2, ADAM_EPS, ADAM_WD, ADAM_STEP = 0.001, 0.9, 0.999, 1e-08, 0.01, 10

LANES = 128
HALO = 64
PACK_W = 1024
N_DEV = 8
VMEM_LIMIT = 56 * 1024 * 1024
MESH = pl.DeviceIdType.MESH

FFN_TM, FFN_TN = 1024, 256
FFN_BWD_TM = 512
MIX_TM = 256
ATTN_CH = 512
WG_BK = 512
ADAM_BLOCK = 128 * 1024

SHARDED = ("ffn1_w_gu", "ffn1_w_down", "w_in", "w_uq", "w_uk", "w_uv", "w_out", "ffn2_w_gu", "ffn2_w_down")
ROW_SHARDED = ("ffn1_w_down", "w_out", "ffn2_w_down")
REPLICATED = ("ffn1_norm", "mix_norm", "q_lat_norm", "kv_lat_norm", "q_norm", "k_norm", "w_pool", "pool_scale",
              "ffn2_norm")
WEIGHTS = ("ffn1_norm", "ffn1_w_gu", "ffn1_w_down", "mix_norm", "w_in", "q_lat_norm", "kv_lat_norm", "w_uq", "w_uk",
           "w_uv", "q_norm", "k_norm", "w_pool", "pool_scale", "w_out", "ffn2_norm", "ffn2_w_gu", "ffn2_w_down")


def _tile(n, pref):
    if n <= pref:
        return n
    t = pref - pref % 8
    while n % t:
        t -= 8
    return t


def _params(*sem):
    return pltpu.CompilerParams(dimension_semantics=sem, vmem_limit_bytes=VMEM_LIMIT)


def _mm(a, b):
    return jnp.dot(a.astype(MXU_DTYPE), b.astype(MXU_DTYPE), preferred_element_type=F32)


def _mm_nt(a, b):
    return lax.dot_general(a.astype(MXU_DTYPE), b.astype(MXU_DTYPE), (((1,), (1,)), ((), ())),
                           preferred_element_type=F32)


def _mm_tn(a, b):
    return lax.dot_general(a.astype(MXU_DTYPE), b.astype(MXU_DTYPE), (((0,), (0,)), ((), ())),
                           preferred_element_type=F32)


def _rstd(x, n):
    return lax.rsqrt(jnp.sum(x * x, axis=-1, keepdims=True) / n + EPS)


def _rms_bwd(dy, xhat, r, gain, n):
    dxh = dy * gain
    return r * (dxh - xhat * (jnp.sum(dxh * xhat, axis=-1, keepdims=True) / n))


def _colsum(x):
    return jnp.sum(x, axis=0, keepdims=True)


def _split3(x):
    hi = x.astype(MXU_DTYPE)
    r1 = x - hi.astype(F32)
    mid = r1.astype(MXU_DTYPE)
    lo = (r1 - mid.astype(F32)).astype(MXU_DTYPE)
    return jnp.concatenate([hi, mid, lo], axis=1)


def _sum3(x):
    n = x.shape[1] // 3
    return (x[:, :n] + x[:, n:2 * n]) + x[:, 2 * n:]


def _pool_mixed(prev, main, nxt, row0, seq):
    tm = main.shape[0]
    k = tm + 2 * HALO
    ext = jnp.concatenate([prev, main, nxt], axis=0)
    s_i = lax.broadcasted_iota(jnp.int32, (tm, k), 0) + row0
    t_j = lax.broadcasted_iota(jnp.int32, (tm, k), 1) + (row0 - HALO)
    s_v = lax.broadcasted_iota(jnp.int32, (tm, 1), 0) + row0
    inside = (t_j >= 0) & (t_j < seq)
    outs = []
    for g, w in enumerate(POOL_WINDOWS):
        left, right = w // 2, w - 1 - w // 2
        band = jnp.where((t_j >= s_i - left) & (t_j <= s_i + right) & inside, 1.0, 0.0).astype(MXU_DTYPE)
        sl = slice(g * LANES, (g + 1) * LANES)
        wsum = _sum3(jnp.dot(band, _split3(ext[:, sl]), preferred_element_type=F32))
        cnt = (jnp.minimum(s_v + right + 1, seq) - jnp.maximum(s_v - left, 0)).astype(F32)
        outs.append(wsum / cnt - main[:, sl])
    return jnp.concatenate(outs, axis=1)


def _pool_mixed_t(prev, main, nxt, row0, seq):
    tm = main.shape[0]
    k = tm + 2 * HALO
    ext = jnp.concatenate([prev, main, nxt], axis=0)
    t_i = lax.broadcasted_iota(jnp.int32, (tm, k), 0) + row0
    s_j = lax.broadcasted_iota(jnp.int32, (tm, k), 1) + (row0 - HALO)
    s_v = lax.broadcasted_iota(jnp.int32, (k, 1), 0) + (row0 - HALO)
    inside = (s_j >= 0) & (s_j < seq)
    inside_v = (s_v >= 0) & (s_v < seq)
    outs = []
    for g, w in enumerate(POOL_WINDOWS):
        left, right = w // 2, w - 1 - w // 2
        band = jnp.where((s_j >= t_i - right) & (s_j <= t_i + left) & inside, 1.0, 0.0).astype(MXU_DTYPE)
        sl = slice(g * LANES, (g + 1) * LANES)
        cnt = (jnp.minimum(s_v + right + 1, seq) - jnp.maximum(s_v - left, 0)).astype(F32)
        scaled = jnp.where(inside_v, ext[:, sl] / jnp.maximum(cnt, 1.0), 0.0)
        outs.append(_sum3(jnp.dot(band, _split3(scaled), preferred_element_type=F32)) - main[:, sl])
    return jnp.concatenate(outs, axis=1)


def _halo_specs(tm, width, seq):
    per = tm // HALO
    last = seq // HALO - 1
    prev = pl.BlockSpec((HALO, width), lambda i: (jnp.maximum(i * per - 1, 0), 0))
    nxt = pl.BlockSpec((HALO, width), lambda i: (jnp.minimum((i + 1) * per, last), 0))
    return prev, nxt


def _rope(x, c, s1, s2, half):
    return x * c + pltpu.roll(x, half, 1) * s1 + pltpu.roll(x, LANES - half, 1) * s2


def _rope_t(d, c, s1, s2, half):
    return d * c + pltpu.roll(d * s1, LANES - half, 1) + pltpu.roll(d * s2, half, 1)


def _rope_tables(seq, nope, rope):
    half = rope // 2
    pos = jnp.arange(seq, dtype=F32)
    inv = ROPE_THETA ** (-jnp.arange(0, rope, 2, dtype=F32) / rope)
    ang = pos[:, None] * inv[None, :]
    cos, sin = jnp.cos(ang), jnp.sin(ang)
    zeros = lambda n: jnp.zeros((seq, n), F32)
    ones = lambda n: jnp.ones((seq, n), F32)
    tail = LANES - nope - rope
    c = jnp.concatenate([ones(nope), cos, cos, ones(tail)], axis=1)
    s1 = jnp.concatenate([zeros(nope + half), sin, zeros(tail)], axis=1)
    s2 = jnp.concatenate([zeros(nope), -sin, zeros(half + tail)], axis=1)
    return c, s1, s2


def _ffn_fwd(x, gain, wgu, wd):
    seq, d = x.shape
    f = wd.shape[0]
    tm, tn = _tile(seq, FFN_TM), _tile(f, FFN_TN)
    nk = f // tn

    def body(x_ref, g_ref, wg_ref, wu_ref, wd_ref, o_ref, h_sc, acc_sc):
        k = pl.program_id(1)

        @pl.when(k == 0)
        def _():
            xv = x_ref[...]
            h_sc[...] = (xv * _rstd(xv, d) * g_ref[...]).astype(MXU_DTYPE)
            acc_sc[...] = jnp.zeros_like(acc_sc)

        h = h_sc[...]
        g = jnp.dot(h, wg_ref[...], preferred_element_type=F32)
        u = jnp.dot(h, wu_ref[...], preferred_element_type=F32)
        a = g * (1.0 / (1.0 + jnp.exp(-g))) * u
        acc_sc[...] += _mm(a, wd_ref[...])

        @pl.when(k == nk - 1)
        def _():
            o_ref[...] = x_ref[...] + 0.5 * acc_sc[...]

    return pl.pallas_call(
        body, name="ffn_fwd", grid=(seq // tm, nk),
        in_specs=[pl.BlockSpec((tm, d), lambda i, k: (i, 0)),
                  pl.BlockSpec((1, d), lambda i, k: (0, 0)),
                  pl.BlockSpec((d, tn), lambda i, k: (0, k)),
                  pl.BlockSpec((d, tn), lambda i, k: (0, k + nk)),
                  pl.BlockSpec((tn, d), lambda i, k: (k, 0))],
        out_specs=pl.BlockSpec((tm, d), lambda i, k: (i, 0)),
        out_shape=jax.ShapeDtypeStruct((seq, d), F32),
        scratch_shapes=[pltpu.VMEM((tm, d), MXU_DTYPE), pltpu.VMEM((tm, d), F32)],
        compiler_params=_params("parallel", "arbitrary"),
    )(x, gain, wgu, wgu, wd)


def _ffn_bwd(x, gain, wgu, wd, dout):
    seq, d = x.shape
    f = wd.shape[0]
    tm, tn = _tile(seq, FFN_BWD_TM), _tile(f, FFN_TN)
    nk = f // tn

    def body(x_ref, g_ref, wg_ref, wu_ref, wd_ref, do_ref, dx_ref, dgain_ref, h_ref, act_ref, dg_ref, du_ref,
             h_sc, dy_sc, acc_sc):
        i, k = pl.program_id(0), pl.program_id(1)

        @pl.when(k == 0)
        def _():
            xv = x_ref[...]
            hv = (xv * _rstd(xv, d) * g_ref[...]).astype(MXU_DTYPE)
            h_sc[...] = hv
            h_ref[...] = hv
            dy_sc[...] = (0.5 * do_ref[...]).astype(MXU_DTYPE)
            acc_sc[...] = jnp.zeros_like(acc_sc)

        @pl.when((k == 0) & (i == 0))
        def _():
            dgain_ref[...] = jnp.zeros_like(dgain_ref)

        h = h_sc[...]
        g = jnp.dot(h, wg_ref[...], preferred_element_type=F32)
        u = jnp.dot(h, wu_ref[...], preferred_element_type=F32)
        sig = 1.0 / (1.0 + jnp.exp(-g))
        silu = g * sig
        act_ref[...] = (silu * u).astype(MXU_DTYPE)
        da = _mm_nt(dy_sc[...], wd_ref[...])
        du = (da * silu).astype(MXU_DTYPE)
        dg = (da * u * (sig * (1.0 + g * (1.0 - sig)))).astype(MXU_DTYPE)
        dg_ref[...] = dg
        du_ref[...] = du
        acc_sc[...] += _mm_nt(dg, wg_ref[...]) + _mm_nt(du, wu_ref[...])

        @pl.when(k == nk - 1)
        def _():
            xv = x_ref[...]
            r = _rstd(xv, d)
            xhat = xv * r
            dh = acc_sc[...]
            dgain_ref[...] += _colsum(dh * xhat)
            dx_ref[...] = do_ref[...] + _rms_bwd(dh, xhat, r, g_ref[...], d)

    row = lambda i, k: (i, 0)
    col = lambda i, k: (i, k)
    return pl.pallas_call(
        body, name="ffn_bwd", grid=(seq // tm, nk),
        in_specs=[pl.BlockSpec((tm, d), row),
                  pl.BlockSpec((1, d), lambda i, k: (0, 0)),
                  pl.BlockSpec((d, tn), lambda i, k: (0, k)),
                  pl.BlockSpec((d, tn), lambda i, k: (0, k + nk)),
                  pl.BlockSpec((tn, d), lambda i, k: (k, 0)),
                  pl.BlockSpec((tm, d), row)],
        out_specs=[pl.BlockSpec((tm, d), row),
                   pl.BlockSpec((1, d), lambda i, k: (0, 0)),
                   pl.BlockSpec((tm, d), row),
                   pl.BlockSpec((tm, tn), col),
                   pl.BlockSpec((tm, tn), col),
                   pl.BlockSpec((tm, tn), col)],
        out_shape=[jax.ShapeDtypeStruct((seq, d), F32), jax.ShapeDtypeStruct((1, d), F32),
                   jax.ShapeDtypeStruct((seq, d), MXU_DTYPE), jax.ShapeDtypeStruct((seq, f), MXU_DTYPE),
                   jax.ShapeDtypeStruct((seq, f), MXU_DTYPE), jax.ShapeDtypeStruct((seq, f), MXU_DTYPE)],
        scratch_shapes=[pltpu.VMEM((tm, d), MXU_DTYPE), pltpu.VMEM((tm, d), MXU_DTYPE), pltpu.VMEM((tm, d), F32)],
        compiler_params=_params("arbitrary", "arbitrary"),
    )(x, gain, wgu, wgu, wd, dout)


def _wgrad(a, b, scale, bm, bn):
    seq, m = a.shape
    n = b.shape[1]
    bm, bn, bk = _tile(m, bm), _tile(n, bn), _tile(seq, WG_BK)
    ns = seq // bk

    def body(a_ref, b_ref, o_ref):
        @pl.when(pl.program_id(2) == 0)
        def _():
            o_ref[...] = jnp.zeros_like(o_ref)

        o_ref[...] += scale * _mm_tn(a_ref[...], b_ref[...])

    return pl.pallas_call(
        body, name="wgrad", grid=(m // bm, n // bn, ns),
        in_specs=[pl.BlockSpec((bk, bm), lambda i, j, s: (s, i)),
                  pl.BlockSpec((bk, bn), lambda i, j, s: (s, j))],
        out_specs=pl.BlockSpec((bm, bn), lambda i, j, s: (i, j)),
        out_shape=jax.ShapeDtypeStruct((m, n), F32),
        compiler_params=_params("parallel", "parallel", "arbitrary"),
    )(a, b)


class _Dims:
    def __init__(self, d, ql, kvl, heads, head_dim, nope, vh, pool_w, seq):
        self.d, self.ql, self.kvl, self.heads, self.head_dim, self.nope, self.vh, self.pool_w = (
            d, ql, kvl, heads, head_dim, nope, vh, pool_w)
        self.rope = head_dim - nope
        self.half = self.rope // 2
        self.hw = heads * LANES
        self.vw = heads * vh
        self.ch = _tile(seq, ATTN_CH)
        self.nch = seq // self.ch
        self.o_kv = ql
        self.o_pe = ql + kvl
        self.o_pool = ql + kvl + LANES
        self.zw = self.o_pool + pool_w
        self.scale = head_dim ** -0.5


def _chunk_spec(dm, rows, tm):
    per = dm.ch // tm
    return pl.BlockSpec((dm.heads, 1, rows, tm), lambda i: (0, i // per, 0, i % per))


def _mixin_fwd(dm, x, gmix, win, gql, gkvl, wuq, wuk, wuv, gq, gk, rc, rs1, rs2):
    seq, d = x.shape
    tm = _tile(seq, MIX_TM)

    def body(x_ref, gmix_ref, win_ref, gql_ref, gkvl_ref, wuq_ref, wuk_ref, wuv_ref, gq_ref, gk_ref,
             rc_ref, rs1_ref, rs2_ref, q_ref, k_ref, v_ref, zp_ref):
        xv = x_ref[...]
        z = _mm(xv * _rstd(xv, d) * gmix_ref[...], win_ref[...])
        cq, ckv = z[:, :dm.o_kv], z[:, dm.o_kv:dm.o_pe]
        kpe = z[:, dm.o_pe:dm.o_pool]
        zp_ref[...] = z[:, dm.o_pool:]
        cqn = cq * _rstd(cq, dm.ql) * gql_ref[...]
        ckvn = ckv * _rstd(ckv, dm.kvl) * gkvl_ref[...]
        q = _mm(cqn, wuq_ref[...])
        kn = _mm(ckvn, wuk_ref[...])
        v_ref[...] = _mm(ckvn, wuv_ref[...]).T.reshape(dm.heads, 1, dm.vh, tm).astype(MXU_DTYPE)
        c, s1, s2 = rc_ref[...], rs1_ref[...], rs2_ref[...]
        for h in range(dm.heads):
            sl = slice(h * LANES, (h + 1) * LANES)
            qh = q[:, sl]
            qn = qh * _rstd(qh, dm.head_dim) * gq_ref[...]
            q_ref[:, sl] = (_rope(qn, c, s1, s2, dm.half) * dm.scale).astype(MXU_DTYPE)
            kh = kn[:, sl] + kpe
            kk = kh * _rstd(kh, dm.head_dim) * gk_ref[...]
            k_ref[:, sl] = _rope(kk, c, s1, s2, dm.half).astype(MXU_DTYPE)

    row = lambda i: (i, 0)
    full = lambda a: pl.BlockSpec(a.shape, lambda i: (0,) * a.ndim)
    return pl.pallas_call(
        body, name="mixin_fwd", grid=(seq // tm,),
        in_specs=[pl.BlockSpec((tm, d), row), full(gmix), full(win), full(gql), full(gkvl), full(wuq), full(wuk),
                  full(wuv), full(gq), full(gk),
                  pl.BlockSpec((tm, LANES), row), pl.BlockSpec((tm, LANES), row), pl.BlockSpec((tm, LANES), row)],
        out_specs=[pl.BlockSpec((tm, dm.hw), row), pl.BlockSpec((tm, dm.hw), row), _chunk_spec(dm, dm.vh, tm),
                   pl.BlockSpec((tm, dm.pool_w), row)],
        out_shape=[jax.ShapeDtypeStruct((seq, dm.hw), MXU_DTYPE), jax.ShapeDtypeStruct((seq, dm.hw), MXU_DTYPE),
                   jax.ShapeDtypeStruct((dm.heads, dm.nch, dm.vh, dm.ch), MXU_DTYPE),
                   jax.ShapeDtypeStruct((seq, dm.pool_w), F32)],
        compiler_params=_params("parallel"),
    )(x, gmix, win, gql, gkvl, wuq, wuk, wuv, gq, gk, rc, rs1, rs2)


def _mla_in_bwd(dm, x, gmix, win, gql, gkvl, wuq, wuk, wuv, gq, gk, rc, rs1, rs2, dq, dk, dv, dmixed):
    seq, d = x.shape
    tm = _tile(seq, MIX_TM)

    def body(x_ref, gmix_ref, win_ref, gql_ref, gkvl_ref, wuq_ref, wuk_ref, wuv_ref, gq_ref, gk_ref,
             rc_ref, rs1_ref, rs2_ref, dq_ref, dk_ref, dv_ref, dmp_ref, dm_ref, dmn_ref,
             dz_ref, dwuq_ref, dwuk_ref, dwuv_ref, dgql_ref, dgkvl_ref, dgq_ref, dgk_ref):
        i = pl.program_id(0)

        @pl.when(i == 0)
        def _():
            for ref in (dwuq_ref, dwuk_ref, dwuv_ref, dgql_ref, dgkvl_ref, dgq_ref, dgk_ref):
                ref[...] = jnp.zeros_like(ref)

        xv = x_ref[...]
        z = _mm(xv * _rstd(xv, d) * gmix_ref[...], win_ref[...])
        cq, ckv = z[:, :dm.o_kv], z[:, dm.o_kv:dm.o_pe]
        kpe = z[:, dm.o_pe:dm.o_pool]
        r_q, r_kv = _rstd(cq, dm.ql), _rstd(ckv, dm.kvl)
        cqh, ckvh = cq * r_q, ckv * r_kv
        cqn = (cqh * gql_ref[...]).astype(MXU_DTYPE)
        ckvn = (ckvh * gkvl_ref[...]).astype(MXU_DTYPE)
        q = _mm(cqn, wuq_ref[...])
        kn = _mm(ckvn, wuk_ref[...])
        c, s1, s2 = rc_ref[...], rs1_ref[...], rs2_ref[...]
        dqv = dq_ref[...].reshape(dm.hw, tm).T
        dq_pre, dk_pre = [], []
        dkpe = jnp.zeros((tm, LANES), F32)
        dgq = jnp.zeros((1, LANES), F32)
        dgk = jnp.zeros((1, LANES), F32)
        for h in range(dm.heads):
            sl = slice(h * LANES, (h + 1) * LANES)
            qh = q[:, sl]
            rq = _rstd(qh, dm.head_dim)
            xq = qh * rq
            dqn = _rope_t(dqv[:, sl] * dm.scale, c, s1, s2, dm.half)
            dgq += _colsum(dqn * xq)
            dq_pre.append(_rms_bwd(dqn, xq, rq, gq_ref[...], dm.head_dim))
            kh = kn[:, sl] + kpe
            rk = _rstd(kh, dm.head_dim)
            xk = kh * rk
            dkn = _rope_t(dk_ref[:, sl], c, s1, s2, dm.half)
            dgk += _colsum(dkn * xk)
            dkh = _rms_bwd(dkn, xk, rk, gk_ref[...], dm.head_dim)
            dk_pre.append(dkh)
            dkpe += dkh
        dgq_ref[...] += dgq
        dgk_ref[...] += dgk
        dq_pre = jnp.concatenate(dq_pre, axis=1).astype(MXU_DTYPE)
        dk_pre = jnp.concatenate(dk_pre, axis=1).astype(MXU_DTYPE)
        dvv = dv_ref[...].reshape(dm.vw, tm).T.astype(MXU_DTYPE)
        dwuq_ref[...] += _mm_tn(cqn, dq_pre)
        dwuk_ref[...] += _mm_tn(ckvn, dk_pre)
        dwuv_ref[...] += _mm_tn(ckvn, dvv)
        dcqn = _mm_nt(dq_pre, wuq_ref[...])
        dckvn = _mm_nt(dk_pre, wuk_ref[...]) + _mm_nt(dvv, wuv_ref[...])
        dgql_ref[...] += _colsum(dcqn * cqh)
        dgkvl_ref[...] += _colsum(dckvn * ckvh)
        dcq = _rms_bwd(dcqn, cqh, r_q, gql_ref[...], dm.ql)
        dckv = _rms_bwd(dckvn, ckvh, r_kv, gkvl_ref[...], dm.kvl)
        dzp = _pool_mixed_t(dmp_ref[...], dm_ref[...], dmn_ref[...], i * tm, seq)
        dz_ref[...] = jnp.concatenate([dcq, dckv, dkpe, dzp], axis=1).astype(MXU_DTYPE)

    row = lambda i: (i, 0)
    full = lambda a: pl.BlockSpec(a.shape, lambda i: (0,) * a.ndim)
    acc = lambda shape: pl.BlockSpec(shape, lambda i: (0, 0))
    prev, nxt = _halo_specs(tm, dm.pool_w, seq)
    shapes = [(seq, dm.zw), wuq.shape, wuk.shape, wuv.shape, (1, dm.ql), (1, dm.kvl), (1, LANES), (1, LANES)]
    return pl.pallas_call(
        body, name="mla_in_bwd", grid=(seq // tm,),
        in_specs=[pl.BlockSpec((tm, d), row), full(gmix), full(win), full(gql), full(gkvl), full(wuq), full(wuk),
                  full(wuv), full(gq), full(gk),
                  pl.BlockSpec((tm, LANES), row), pl.BlockSpec((tm, LANES), row), pl.BlockSpec((tm, LANES), row),
                  _chunk_spec(dm, LANES, tm), pl.BlockSpec((tm, dm.hw), row), _chunk_spec(dm, dm.vh, tm),
                  prev, pl.BlockSpec((tm, dm.pool_w), row), nxt],
        out_specs=[pl.BlockSpec((tm, dm.zw), row)] + [acc(s) for s in shapes[1:]],
        out_shape=[jax.ShapeDtypeStruct(shapes[0], MXU_DTYPE)] + [jax.ShapeDtypeStruct(s, F32) for s in shapes[1:]],
        compiler_params=_params("arbitrary"),
    )(x, gmix, win, gql, gkvl, wuq, wuk, wuv, gq, gk, rc, rs1, rs2, dq, dk, dv, dmixed, dmixed, dmixed)


def _rms_proj_bwd(x, gain, w, dz, gin):
    seq, d = x.shape
    n = w.shape[1]
    tm = _tile(seq, MIX_TM)

    def body(x_ref, g_ref, w_ref, dz_ref, gin_ref, gout_ref, dw_ref, dgain_ref):
        @pl.when(pl.program_id(0) == 0)
        def _():
            dw_ref[...] = jnp.zeros_like(dw_ref)
            dgain_ref[...] = jnp.zeros_like(dgain_ref)

        xv = x_ref[...]
        r = _rstd(xv, d)
        xhat = xv * r
        dzv = dz_ref[...]
        dh = _mm_nt(dzv, w_ref[...])
        dw_ref[...] += _mm_tn(xhat * g_ref[...], dzv)
        dgain_ref[...] += _colsum(dh * xhat)
        gout_ref[...] = gin_ref[...] + _rms_bwd(dh, xhat, r, g_ref[...], d)

    row = lambda i: (i, 0)
    return pl.pallas_call(
        body, name="rms_proj_bwd", grid=(seq // tm,),
        in_specs=[pl.BlockSpec((tm, d), row), pl.BlockSpec((1, d), lambda i: (0, 0)),
                  pl.BlockSpec((d, n), lambda i: (0, 0)), pl.BlockSpec((tm, n), row), pl.BlockSpec((tm, d), row)],
        out_specs=[pl.BlockSpec((tm, d), row), pl.BlockSpec((d, n), lambda i: (0, 0)),
                   pl.BlockSpec((1, d), lambda i: (0, 0))],
        out_shape=[jax.ShapeDtypeStruct((seq, d), F32), jax.ShapeDtypeStruct((d, n), F32),
                   jax.ShapeDtypeStruct((1, d), F32)],
        compiler_params=_params("arbitrary"),
    )(x, gain, w, dz, gin)


def _pool_branch(mixed, wpool_ref):
    groups = mixed.shape[1] // LANES
    return jnp.concatenate(
        [_mm(mixed[:, g * LANES:(g + 1) * LANES], wpool_ref[g]) for g in range(groups)], axis=1)


def _mixout_fwd(dm, x, zp, ot, wo, wpool, pscale):
    seq, d = x.shape
    tm = _tile(seq, MIX_TM)

    def body(x_ref, zpp_ref, zp_ref, zpn_ref, ot_ref, wo_ref, wpool_ref, ps_ref, out_ref):
        mixed = _pool_mixed(zpp_ref[...], zp_ref[...], zpn_ref[...], pl.program_id(0) * tm, seq)
        b = _pool_branch(mixed, wpool_ref) * ps_ref[...]
        a = _mm_tn(ot_ref[...].reshape(dm.vw, tm), wo_ref[:dm.vw, :])
        out_ref[...] = x_ref[...] + a + _mm(b, wo_ref[dm.vw:, :])

    row = lambda i: (i, 0)
    full = lambda a: pl.BlockSpec(a.shape, lambda i: (0,) * a.ndim)
    prev, nxt = _halo_specs(tm, dm.pool_w, seq)
    return pl.pallas_call(
        body, name="mixout_fwd", grid=(seq // tm,),
        in_specs=[pl.BlockSpec((tm, d), row), prev, pl.BlockSpec((tm, dm.pool_w), row), nxt,
                  _chunk_spec(dm, dm.vh, tm), full(wo), full(wpool), full(pscale)],
        out_specs=pl.BlockSpec((tm, d), row),
        out_shape=jax.ShapeDtypeStruct((seq, d), F32),
        compiler_params=_params("parallel"),
    )(x, zp, zp, zp, ot, wo, wpool, pscale)


def _mixout_bwd(dm, g, zp, ot, wo, wpool, pscale):
    seq, d = g.shape
    tm = _tile(seq, MIX_TM)
    groups = dm.pool_w // LANES

    def body(g_ref, zpp_ref, zp_ref, zpn_ref, ot_ref, wo_ref, wpool_ref, ps_ref,
             dot_ref, delta_ref, dmixed_ref, dwo_ref, dwpool_ref, dps_ref):
        i = pl.program_id(0)

        @pl.when(i == 0)
        def _():
            for ref in (dwo_ref, dwpool_ref, dps_ref):
                ref[...] = jnp.zeros_like(ref)

        gv = g_ref[...].astype(MXU_DTYPE)
        otv = ot_ref[...].reshape(dm.vw, tm)
        dat = _mm_nt(wo_ref[:dm.vw, :], gv)
        db = _mm_nt(gv, wo_ref[dm.vw:, :])
        mixed = _pool_mixed(zpp_ref[...], zp_ref[...], zpn_ref[...], i * tm, seq).astype(MXU_DTYPE)
        y = _pool_branch(mixed, wpool_ref)
        b = (y * ps_ref[...]).astype(MXU_DTYPE)
        dwo_ref[:dm.vw, :] += _mm(otv, gv)
        dwo_ref[dm.vw:, :] += _mm_tn(b, gv)
        dps_ref[...] += _colsum(db * y)
        dy = (db * ps_ref[...]).astype(MXU_DTYPE)
        dmx = []
        for gi in range(groups):
            sl = slice(gi * LANES, (gi + 1) * LANES)
            dmx.append(_mm_nt(dy[:, sl], wpool_ref[gi]))
            dwpool_ref[gi] += _mm_tn(mixed[:, sl], dy[:, sl])
        dmixed_ref[...] = jnp.concatenate(dmx, axis=1)
        dot_ref[...] = dat.reshape(dm.heads, 1, dm.vh, tm).astype(MXU_DTYPE)
        prod = dat * otv.astype(F32)
        for h in range(dm.heads):
            delta_ref[h, 0] = jnp.broadcast_to(_colsum(prod[h * dm.vh:(h + 1) * dm.vh]), (8, tm))

    row = lambda i: (i, 0)
    full = lambda a: pl.BlockSpec(a.shape, lambda i: (0,) * a.ndim)
    prev, nxt = _halo_specs(tm, dm.pool_w, seq)
    return pl.pallas_call(
        body, name="mixout_bwd", grid=(seq // tm,),
        in_specs=[pl.BlockSpec((tm, d), row), prev, pl.BlockSpec((tm, dm.pool_w), row), nxt,
                  _chunk_spec(dm, dm.vh, tm), full(wo), full(wpool), full(pscale)],
        out_specs=[_chunk_spec(dm, dm.vh, tm), _chunk_spec(dm, 8, tm), pl.BlockSpec((tm, dm.pool_w), row),
                   full(wo), full(wpool), full(pscale)],
        out_shape=[jax.ShapeDtypeStruct((dm.heads, dm.nch, dm.vh, dm.ch), MXU_DTYPE),
                   jax.ShapeDtypeStruct((dm.heads, dm.nch, 8, dm.ch), F32),
                   jax.ShapeDtypeStruct((seq, dm.pool_w), F32), jax.ShapeDtypeStruct(wo.shape, F32),
                   jax.ShapeDtypeStruct(wpool.shape, F32), jax.ShapeDtypeStruct(pscale.shape, F32)],
        compiler_params=_params("arbitrary"),
    )(g, zp, zp, zp, ot, wo, wpool, pscale)


def _attn_fwd(dm, q, k, vt):
    seq, ch, nch = q.shape[0], dm.ch, dm.nch

    def body(q_ref, k_ref, vt_ref, ot_ref, lse_ref, m_sc, l_sc, acc_sc, st0, st1, pt0, pt1):
        m_sc[...] = jnp.full_like(m_sc, -jnp.inf)
        l_sc[...] = jnp.zeros_like(l_sc)
        acc_sc[...] = jnp.zeros_like(acc_sc)
        qv = q_ref[...]

        def scores(j):
            return _mm_nt(k_ref[pl.ds(pl.multiple_of(j * ch, ch), ch), :], qv)

        def stage(j, st_cur, st_nxt, pt_cur, pt_prev):
            st_nxt[...] = scores(jnp.minimum(j + 1, nch - 1))
            acc = acc_sc[...] + _mm(vt_ref[0, jnp.maximum(j - 1, 0)], pt_prev[...])
            st = st_cur[...]
            m_prev = m_sc[...]
            m_new = jnp.maximum(m_prev, jnp.max(st, axis=0, keepdims=True))
            alpha = jnp.exp(m_prev - m_new)
            pt = jnp.exp(st - m_new)
            pt_cur[...] = pt.astype(MXU_DTYPE)
            l_sc[...] = alpha * l_sc[...] + _colsum(pt)
            acc_sc[...] = alpha * acc
            m_sc[...] = m_new

        st0[...] = scores(0)
        pt1[...] = jnp.zeros_like(pt1)

        def pair(jj, carry):
            stage(2 * jj, st0, st1, pt0, pt1)
            stage(2 * jj + 1, st1, st0, pt1, pt0)
            return carry

        lax.fori_loop(0, nch // 2, pair, 0)
        acc = acc_sc[...] + _mm(vt_ref[0, nch - 1], pt1[...])
        ot_ref[0, 0] = (acc / l_sc[...]).astype(MXU_DTYPE)
        lse_ref[0, 0] = jnp.broadcast_to(m_sc[...] + jnp.log(l_sc[...]), (8, ch))

    assert nch % 2 == 0
    chunk = lambda rows: pl.BlockSpec((1, 1, rows, ch), lambda h, i: (h, i, 0, 0))
    return pl.pallas_call(
        body, name="attn_fwd", grid=(dm.heads, nch),
        in_specs=[pl.BlockSpec((ch, LANES), lambda h, i: (i, h)),
                  pl.BlockSpec((seq, LANES), lambda h, i: (0, h)),
                  pl.BlockSpec((1, nch, dm.vh, ch), lambda h, i: (h, 0, 0, 0))],
        out_specs=[chunk(dm.vh), chunk(8)],
        out_shape=[jax.ShapeDtypeStruct((dm.heads, nch, dm.vh, ch), MXU_DTYPE),
                   jax.ShapeDtypeStruct((dm.heads, nch, 8, ch), F32)],
        scratch_shapes=[pltpu.VMEM((1, ch), F32), pltpu.VMEM((1, ch), F32), pltpu.VMEM((dm.vh, ch), F32),
                        pltpu.VMEM((ch, ch), F32), pltpu.VMEM((ch, ch), F32),
                        pltpu.VMEM((ch, ch), MXU_DTYPE), pltpu.VMEM((ch, ch), MXU_DTYPE)],
        compiler_params=_params("parallel", "parallel"),
    )(q, k, vt)


def _attn_bwd(dm, q, k, vt, dot, lse, delta):
    seq, ch, nch = q.shape[0], dm.ch, dm.nch

    def body(k_ref, vt_ref, q_ref, dot_ref, lse_ref, delta_ref, dqt_ref, dk_ref, dvt_ref):
        j = pl.program_id(1)
        kv = k_ref[...]
        kt = kv.astype(F32).T.astype(MXU_DTYPE)
        vtv = vt_ref[0, 0]
        dk_ref[...] = jnp.zeros_like(dk_ref)
        dvt_ref[...] = jnp.zeros_like(dvt_ref)

        def q_chunk(i):
            return q_ref[pl.ds(pl.multiple_of(i * ch, ch), ch), :]

        def stage(i, carry):
            qi, doti = q_chunk(i), dot_ref[0, i]
            pt = jnp.exp(_mm_nt(kv, qi) - lse_ref[0, i][:1])
            dst = (pt * (_mm_tn(vtv, doti) - delta_ref[0, i][:1])).astype(MXU_DTYPE)
            dvt_ref[0, 0] += _mm_nt(doti, pt)
            dk_ref[...] += _mm(dst, qi)
            dq_part = _mm(kt, dst)

            @pl.when(j == 0)
            def _():
                dqt_ref[0, i] = dq_part

            @pl.when(j > 0)
            def _():
                dqt_ref[0, i] += dq_part

            return carry

        lax.fori_loop(0, nch, stage, 0)

    whole = lambda rows: pl.BlockSpec((1, nch, rows, ch), lambda h, j: (h, 0, 0, 0))
    chunk = lambda rows: pl.BlockSpec((1, 1, rows, ch), lambda h, j: (h, j, 0, 0))
    return pl.pallas_call(
        body, name="attn_bwd", grid=(dm.heads, nch),
        in_specs=[pl.BlockSpec((ch, LANES), lambda h, j: (j, h)), chunk(dm.vh),
                  pl.BlockSpec((seq, LANES), lambda h, j: (0, h)), whole(dm.vh), whole(8), whole(8)],
        out_specs=[whole(LANES), pl.BlockSpec((ch, LANES), lambda h, j: (j, h)), chunk(dm.vh)],
        out_shape=[jax.ShapeDtypeStruct((dm.heads, nch, LANES, ch), F32), jax.ShapeDtypeStruct(q.shape, F32),
                   jax.ShapeDtypeStruct((dm.heads, nch, dm.vh, ch), F32)],
        compiler_params=_params("parallel", "arbitrary"),
    )(k, vt, q, dot, lse, delta)


def _loss_head(y, target):
    seq, d = y.shape
    tm = _tile(seq, FFN_TM)

    def body(y_ref, t_ref, part_ref, dy_ref):
        @pl.when(pl.program_id(0) == 0)
        def _():
            part_ref[...] = jnp.zeros_like(part_ref)

        err = y_ref[...] - t_ref[...]
        part_ref[...] += _colsum(err * err)
        dy_ref[...] = err / d

    row = lambda i: (i, 0)
    return pl.pallas_call(
        body, name="loss_head", grid=(seq // tm,),
        in_specs=[pl.BlockSpec((tm, d), row), pl.BlockSpec((tm, d), row)],
        out_specs=[pl.BlockSpec((1, d), lambda i: (0, 0)), pl.BlockSpec((tm, d), row)],
        out_shape=[jax.ShapeDtypeStruct((1, d), F32), jax.ShapeDtypeStruct((seq, d), F32)],
        compiler_params=_params("arbitrary"),
    )(y, target)


def _my_place():
    return lax.axis_index("x"), lax.axis_index("y"), lax.axis_index("c")


def _all_gather(shards):
    n = len(shards)

    def body(*refs):
        x_refs, out_refs = refs[:n], refs[n:2 * n]
        send_sems, recv_sems, local_sems = refs[2 * n:]
        x, y, c = _my_place()
        me, sibling = (x, y, c), (x, y, 1 - c)
        chips = [(1 - x, y), (x, 1 - y), (1 - x, 1 - y)]

        def slot(t, px, py, pc):
            return out_refs[t].at[4 * px + 2 * py + pc]

        def copy(t, k, block, to, src=None):
            return pltpu.make_async_remote_copy(
                src_ref=slot(t, *block) if src is None else src, dst_ref=slot(t, *block),
                send_sem=send_sems.at[t, k], recv_sem=recv_sems.at[t, k], device_id=to, device_id_type=MESH)

        mine = [pltpu.make_async_copy(x_refs[t], slot(t, *me), local_sems.at[t]) for t in range(n)]
        started = []
        for t in range(n):
            mine[t].start()
            first = [copy(t, 0, me, sibling, src=x_refs[t])]
            first += [copy(t, 1 + j, me, (*chip, c), src=x_refs[t]) for j, chip in enumerate(chips)]
            for cp in first:
                cp.start()
            started += first
        for j, chip in enumerate(chips):
            for t in range(n):
                copy(t, 1 + j, (*chip, c), me).wait_recv()
                passed = copy(t, 4 + j, (*chip, c), sibling)
                passed.start()
                started.append(passed)
        for t in range(n):
            copy(t, 0, sibling, me).wait_recv()
            for j, chip in enumerate(chips):
                copy(t, 4 + j, (*chip, 1 - c), me).wait_recv()
        for cp in started:
            cp.wait_send()
        for cp in mine:
            cp.wait()

    return pl.pallas_call(
        body, name="weights_all_gather",
        out_shape=[jax.ShapeDtypeStruct((N_DEV,) + s.shape, s.dtype) for s in shards],
        in_specs=[pl.BlockSpec(memory_space=pl.ANY)] * n,
        out_specs=[pl.BlockSpec(memory_space=pl.ANY)] * n,
        scratch_shapes=[pltpu.SemaphoreType.DMA((n, 7)), pltpu.SemaphoreType.DMA((n, 7)),
                        pltpu.SemaphoreType.DMA((n,))],
    )(*shards)


def _grad_exchange(blocks, common):
    n, nc = len(blocks), len(common)

    def body(*refs):
        g_refs, c_refs = refs[:n], refs[n:n + nc]
        out_refs = refs[n + nc:2 * (n + nc)]
        send_sems, recv_sems, local_sems = refs[2 * (n + nc):]
        x, y, c = _my_place()
        me = 4 * x + 2 * y + c
        srcs = [lambda dev, r=r: r.at[dev] for r in g_refs] + [lambda dev, r=r: r for r in c_refs]
        local = [pltpu.make_async_copy(srcs[t](me), out_refs[t].at[me], local_sems.at[t]) for t in range(n + nc)]
        for cp in local:
            cp.start()
        copies = []
        for k in range(1, N_DEV):
            px = 1 - x if k & 4 else x
            py = 1 - y if k & 2 else y
            pc = 1 - c if k & 1 else c
            for t in range(n + nc):
                copies.append(pltpu.make_async_remote_copy(
                    src_ref=srcs[t](4 * px + 2 * py + pc), dst_ref=out_refs[t].at[me],
                    send_sem=send_sems.at[t, k - 1], recv_sem=recv_sems.at[t, k - 1],
                    device_id=(px, py, pc), device_id_type=MESH))
        for cp in copies:
            cp.start()
        for cp in copies:
            cp.wait_recv()
        for cp in copies:
            cp.wait_send()
        for cp in local:
            cp.wait()

    return pl.pallas_call(
        body, name="grad_exchange",
        out_shape=[jax.ShapeDtypeStruct(b.shape, b.dtype) for b in blocks]
        + [jax.ShapeDtypeStruct((N_DEV,) + a.shape, a.dtype) for a in common],
        in_specs=[pl.BlockSpec(memory_space=pl.ANY)] * (n + nc),
        out_specs=[pl.BlockSpec(memory_space=pl.ANY)] * (n + nc),
        scratch_shapes=[pltpu.SemaphoreType.DMA((n + nc, 7)), pltpu.SemaphoreType.DMA((n + nc, 7)),
                        pltpu.SemaphoreType.DMA((n + nc,))],
    )(*blocks, *common)


def _adamw(parts, w, m, v):
    rows, width = w.shape
    tr = _tile(rows, max(8, ADAM_BLOCK // width // 8 * 8))

    def body(p_ref, w_ref, m_ref, v_ref, g_ref, d_ref, nm_ref, nv_ref):
        g = p_ref[0].astype(F32)
        for s in range(1, N_DEV):
            g = g + p_ref[s].astype(F32)
        nm = ADAM_B1 * m_ref[...] + (1.0 - ADAM_B1) * g
        nv = ADAM_B2 * v_ref[...] + (1.0 - ADAM_B2) * (g * g)
        m_hat = nm / (1.0 - ADAM_B1 ** ADAM_STEP)
        v_hat = nv / (1.0 - ADAM_B2 ** ADAM_STEP)
        g_ref[...] = g
        d_ref[...] = -ADAM_LR * (m_hat / (jnp.sqrt(v_hat) + ADAM_EPS) + ADAM_WD * w_ref[...])
        nm_ref[...] = nm
        nv_ref[...] = nv

    row = pl.BlockSpec((tr, width), lambda i: (i, 0))
    return pl.pallas_call(
        body, name="adamw", grid=(rows // tr,),
        in_specs=[pl.BlockSpec((N_DEV, tr, width), lambda i: (0, i, 0)), row, row, row],
        out_specs=[row] * 4,
        out_shape=[jax.ShapeDtypeStruct(w.shape, F32)] * 4,
        compiler_params=_params("parallel"),
    )(parts, w, m, v)


def _pack_rows(flat_parts, multiple):
    flat = jnp.concatenate([p.reshape(-1) for p in flat_parts])
    chunk = multiple * PACK_W
    pad = (-flat.shape[0]) % chunk
    if pad:
        flat = jnp.concatenate([flat, jnp.zeros((pad,), flat.dtype)])
    return flat.reshape(-1, PACK_W)


def _unpack(packed, shapes):
    flat = packed.reshape(-1)
    out, off = [], 0
    for shape in shapes:
        size = 1
        for s in shape:
            size *= s
        out.append(flat[off:off + size].reshape(shape))
        off += size
    return out


def _to_full(name, g):
    n, l, a, b = g.shape
    if name in ROW_SHARDED:
        return jnp.transpose(g, (1, 0, 2, 3)).reshape(l, n * a, b)
    return jnp.transpose(g, (1, 2, 0, 3)).reshape(l, a, n * b)


def _to_shards(name, full):
    l, a, b = full.shape
    if name in ROW_SHARDED:
        return jnp.transpose(full.reshape(l, N_DEV, a // N_DEV, b), (1, 0, 2, 3))
    return jnp.transpose(full.reshape(l, a, N_DEV, b // N_DEV), (2, 0, 1, 3))


def _pad_heads(w, heads, real):
    lead = w.shape[:-1]
    w = w.reshape(lead + (heads, real))
    w = jnp.concatenate([w, jnp.zeros(lead + (heads, LANES - real), w.dtype)], axis=-1)
    return w.reshape(lead + (heads * LANES,))


def _unpad_heads(w, heads, real):
    lead = w.shape[:-1]
    return w.reshape(lead + (heads, LANES))[..., :real].reshape(lead + (heads * real,))


def _pad_lanes(v, before):
    l, n = v.shape
    return jnp.concatenate([jnp.zeros((l, before), v.dtype), v, jnp.zeros((l, LANES - before - n), v.dtype)], axis=1)


def kernel(x, ffn1_norm, ffn1_w_gu, ffn1_w_down, mix_norm, w_in, q_lat_norm, kv_lat_norm, w_uq, w_uk, w_uv, q_norm, k_norm, w_pool, pool_scale, w_out, ffn2_norm, ffn2_w_gu, ffn2_w_down, loss_target, m_ffn1_norm, m_ffn1_w_gu, m_ffn1_w_down, m_mix_norm, m_w_in, m_q_lat_norm, m_kv_lat_norm, m_w_uq, m_w_uk, m_w_uv, m_q_norm, m_k_norm, m_w_pool, m_pool_scale, m_w_out, m_ffn2_norm, m_ffn2_w_gu, m_ffn2_w_down, v_ffn1_norm, v_ffn1_w_gu, v_ffn1_w_down, v_mix_norm, v_w_in, v_q_lat_norm, v_kv_lat_norm, v_w_uq, v_w_uk, v_w_uv, v_q_norm, v_k_norm, v_w_pool, v_pool_scale, v_w_out, v_ffn2_norm, v_ffn2_w_gu, v_ffn2_w_down):
    given = dict(locals())
    wts = {n: given[n] for n in WEIGHTS}
    mom1 = {n: given["m_" + n] for n in WEIGHTS}
    mom2 = {n: given["v_" + n] for n in WEIGHTS}

    depth, d = ffn1_norm.shape
    seq = x.shape[1]
    dff = ffn1_w_down.shape[1] * N_DEV
    ql, kvl, head_dim = q_lat_norm.shape[1], kv_lat_norm.shape[1], q_norm.shape[1]
    heads = w_uq.shape[2] * N_DEV // head_dim
    nope = w_uk.shape[2] * N_DEV // heads
    vh = w_uv.shape[2] * N_DEV // heads
    groups, gdim = w_pool.shape[1], w_pool.shape[2]
    pool_w = groups * gdim
    assert gdim == LANES and groups == len(POOL_WINDOWS) and head_dim <= LANES and vh <= LANES
    assert d % LANES == 0 and ql % LANES == 0 and kvl % LANES == 0 and seq % HALO == 0
    dm = _Dims(d, ql, kvl, heads, head_dim, nope, vh, pool_w, seq)

    gathered = _all_gather([wts[n].astype(MXU_DTYPE) for n in SHARDED])
    full = {n: _to_full(n, g) for n, g in zip(SHARDED, gathered)}

    zpad = jnp.zeros((depth, d, LANES), MXU_DTYPE)
    win_p = jnp.concatenate(
        [full["w_in"][..., :dm.o_pe], zpad[..., :nope], full["w_in"][..., dm.o_pe:dm.o_pe + dm.rope],
         zpad[..., :LANES - nope - dm.rope], full["w_in"][..., dm.o_pe + dm.rope:]], axis=-1)
    wuq_p = _pad_heads(full["w_uq"], heads, head_dim)
    wuk_p = _pad_heads(full["w_uk"], heads, nope)
    wuv, wo = full["w_uv"], full["w_out"]
    gq_p, gk_p = _pad_lanes(q_norm, 0), _pad_lanes(k_norm, 0)
    wpool_c = w_pool.astype(MXU_DTYPE)
    rc, rs1, rs2 = _rope_tables(seq, nope, dm.rope)
    row = lambda a, l: a[l][None, :]

    h = x[0]
    saved = []
    for l in range(depth):
        x0 = h
        x1 = _ffn_fwd(x0, row(ffn1_norm, l), full["ffn1_w_gu"][l], full["ffn1_w_down"][l])
        q, k, v, zp = _mixin_fwd(dm, x1, row(mix_norm, l), win_p[l], row(q_lat_norm, l), row(kv_lat_norm, l),
                                 wuq_p[l], wuk_p[l], wuv[l], row(gq_p, l), row(gk_p, l), rc, rs1, rs2)
        o, lse = _attn_fwd(dm, q, k, v)
        x2 = _mixout_fwd(dm, x1, zp, o, wo[l], wpool_c[l], row(pool_scale, l))
        h = _ffn_fwd(x2, row(ffn2_norm, l), full["ffn2_w_gu"][l], full["ffn2_w_down"][l])
        saved.append((x0, x1, x2, q, k, v, zp, o, lse))

    part, g = _loss_head(h, loss_target[0])
    loss = lax.psum(0.5 / d * jnp.sum(part), ("x", "y", "c"))

    grads = {n: [None] * depth for n in WEIGHTS}

    def ffn_grads(prefix, l, xin, gout):
        wgu, wd = full[prefix + "_w_gu"][l], full[prefix + "_w_down"][l]
        gin, dgain, hh, act, dg, du = _ffn_bwd(xin, row(given[prefix + "_norm"], l), wgu, wd, gout)
        grads[prefix + "_norm"][l] = dgain[0]
        grads[prefix + "_w_gu"][l] = jnp.concatenate(
            [_wgrad(hh, dg, 1.0, 1024, 1408), _wgrad(hh, du, 1.0, 1024, 1408)], axis=1)
        grads[prefix + "_w_down"][l] = _wgrad(act, gout, 0.5, 1408, 1024)
        return gin

    for l in reversed(range(depth)):
        x0, x1, x2, q, k, v, zp, o, lse = saved[l]
        g = ffn_grads("ffn2", l, x2, g)
        do, delta, dmixed, dwo, dwpool, dps = _mixout_bwd(dm, g, zp, o, wo[l], wpool_c[l], row(pool_scale, l))
        dq, dk, dv = _attn_bwd(dm, q, k, v, do, lse, delta)
        dz, dwuq, dwuk, dwuv, dgql, dgkvl, dgq, dgk = _mla_in_bwd(
            dm, x1, row(mix_norm, l), win_p[l], row(q_lat_norm, l), row(kv_lat_norm, l), wuq_p[l], wuk_p[l],
            wuv[l], row(gq_p, l), row(gk_p, l), rc, rs1, rs2, dq, dk, dv, dmixed)
        g, dwin, dgmix = _rms_proj_bwd(x1, row(mix_norm, l), win_p[l], dz, g)
        grads["w_out"][l] = dwo
        grads["w_pool"][l] = dwpool
        grads["pool_scale"][l] = dps[0]
        grads["w_uq"][l] = _unpad_heads(dwuq, heads, head_dim)
        grads["w_uk"][l] = _unpad_heads(dwuk, heads, nope)
        grads["w_uv"][l] = dwuv
        grads["q_lat_norm"][l] = dgql[0]
        grads["kv_lat_norm"][l] = dgkvl[0]
        grads["q_norm"][l] = dgq[0, :head_dim]
        grads["k_norm"][l] = dgk[0, :head_dim]
        grads["w_in"][l] = jnp.concatenate(
            [dwin[:, :dm.o_pe], dwin[:, dm.o_pe + nope:dm.o_pe + nope + dm.rope], dwin[:, dm.o_pool:]], axis=1)
        grads["mix_norm"][l] = dgmix[0]
        g = ffn_grads("ffn1", l, x0, g)

    grads = {n: jnp.stack(grads[n]) for n in WEIGHTS}

    repl_shapes = [wts[n].shape for n in REPLICATED]
    received = _grad_exchange([_to_shards(n, grads[n]).astype(WIRE_DTYPE) for n in SHARDED],
                              [_pack_rows([grads[n] for n in REPLICATED], 8)])
    outs = [{}, {}, {}, {}]
    for n, parts in zip(SHARDED, received):
        l, a, b = wts[n].shape
        results = _adamw(parts.reshape(N_DEV, l * a, b),
                         *[src[n].reshape(l * a, b) for src in (wts, mom1, mom2)])
        for o, r in zip(outs, results):
            o[n] = r.reshape(l, a, b)
    results = _adamw(received[-1], *[_pack_rows([src[n] for n in REPLICATED], 8) for src in (wts, mom1, mom2)])
    for o, r in zip(outs, results):
        o.update(zip(REPLICATED, _unpack(r, repl_shapes)))

    return (loss, g[None], *[o[n] for o in outs for n in WEIGHTS])
```

```python
import functools

import jax
import jax.numpy as jnp
from jax import lax
from jax.experimental import pallas as pl
from jax.experimental.pallas import tpu as pltpu

F32 = jnp.float32
MXU_DTYPE = jnp.bfloat16
WIRE_DTYPE = jnp.bfloat16
EPS = 1e-6
ROPE_THETA = 10000.0
POOL_WINDOWS = (2, 4, 8, 16)
ADAM_LR, ADAM_B1, ADAM_B2, ADAM_EPS, ADAM_WD, ADAM_STEP = 0.001, 0.9, 0.999, 1e-08, 0.01, 10

LANES = 128
HALO = 64
PACK_W = 1024
N_DEV = 8
VMEM_LIMIT = 56 * 1024 * 1024
MESH = pl.DeviceIdType.MESH

LOSS_TM = 1024
FFN_FWD_TM, FFN_FWD_TN = 512, 1408
FFN_BWD_TM, FFN_BWD_TN = 256, 1408
FFN_IN_TM = 256
MIX_TM = 256
ATTN_CH = 1024
WG_BK = 512
ADAM_BLOCK = 128 * 1024

SHARDED = ("ffn1_w_gu", "ffn1_w_down", "w_in", "w_uq", "w_uk", "w_uv", "w_out", "ffn2_w_gu", "ffn2_w_down")
ROW_SHARDED = ("ffn1_w_down", "w_out", "ffn2_w_down")
REPLICATED = ("ffn1_norm", "mix_norm", "q_lat_norm", "kv_lat_norm", "q_norm", "k_norm", "w_pool", "pool_scale",
              "ffn2_norm")
WEIGHTS = ("ffn1_norm", "ffn1_w_gu", "ffn1_w_down", "mix_norm", "w_in", "q_lat_norm", "kv_lat_norm", "w_uq", "w_uk",
           "w_uv", "q_norm", "k_norm", "w_pool", "pool_scale", "w_out", "ffn2_norm", "ffn2_w_gu", "ffn2_w_down")


def _tile(n, pref):
    if n <= pref:
        return n
    t = pref - pref % 8
    while n % t:
        t -= 8
    return t


def _params(*sem):
    return pltpu.CompilerParams(dimension_semantics=sem, vmem_limit_bytes=VMEM_LIMIT)


def _mm(a, b):
    return jnp.dot(a.astype(MXU_DTYPE), b.astype(MXU_DTYPE), preferred_element_type=F32)


def _mm_nt(a, b):
    return lax.dot_general(a.astype(MXU_DTYPE), b.astype(MXU_DTYPE), (((1,), (1,)), ((), ())),
                           preferred_element_type=F32)


def _mm_tn(a, b):
    return lax.dot_general(a.astype(MXU_DTYPE), b.astype(MXU_DTYPE), (((0,), (0,)), ((), ())),
                           preferred_element_type=F32)


def _rstd(x, n):
    return lax.rsqrt(jnp.sum(x * x, axis=-1, keepdims=True) / n + EPS)


def _rms_bwd(dy, xhat, r, gain, n):
    dxh = dy * gain
    return r * (dxh - xhat * (jnp.sum(dxh * xhat, axis=-1, keepdims=True) / n))


def _colsum(x):
    return jnp.sum(x, axis=0, keepdims=True)


def _split3(x):
    hi = x.astype(MXU_DTYPE)
    r1 = x - hi.astype(F32)
    mid = r1.astype(MXU_DTYPE)
    lo = (r1 - mid.astype(F32)).astype(MXU_DTYPE)
    return jnp.concatenate([hi, mid, lo], axis=1)


def _sum3(x):
    n = x.shape[1] // 3
    return (x[:, :n] + x[:, n:2 * n]) + x[:, 2 * n:]


def _pool_mixed(prev, main, nxt, row0, seq):
    tm = main.shape[0]
    k = tm + 2 * HALO
    ext = jnp.concatenate([prev, main, nxt], axis=0)
    s_i = lax.broadcasted_iota(jnp.int32, (tm, k), 0) + row0
    t_j = lax.broadcasted_iota(jnp.int32, (tm, k), 1) + (row0 - HALO)
    s_v = lax.broadcasted_iota(jnp.int32, (tm, 1), 0) + row0
    inside = (t_j >= 0) & (t_j < seq)
    outs = []
    for g, w in enumerate(POOL_WINDOWS):
        left, right = w // 2, w - 1 - w // 2
        band = jnp.where((t_j >= s_i - left) & (t_j <= s_i + right) & inside, 1.0, 0.0).astype(MXU_DTYPE)
        sl = slice(g * LANES, (g + 1) * LANES)
        wsum = _sum3(jnp.dot(band, _split3(ext[:, sl]), preferred_element_type=F32))
        cnt = (jnp.minimum(s_v + right + 1, seq) - jnp.maximum(s_v - left, 0)).astype(F32)
        outs.append(wsum / cnt - main[:, sl])
    return jnp.concatenate(outs, axis=1)


def _pool_mixed_t(prev, main, nxt, row0, seq):
    tm = main.shape[0]
    k = tm + 2 * HALO
    ext = jnp.concatenate([prev, main, nxt], axis=0)
    t_i = lax.broadcasted_iota(jnp.int32, (tm, k), 0) + row0
    s_j = lax.broadcasted_iota(jnp.int32, (tm, k), 1) + (row0 - HALO)
    s_v = lax.broadcasted_iota(jnp.int32, (k, 1), 0) + (row0 - HALO)
    inside = (s_j >= 0) & (s_j < seq)
    inside_v = (s_v >= 0) & (s_v < seq)
    outs = []
    for g, w in enumerate(POOL_WINDOWS):
        left, right = w // 2, w - 1 - w // 2
        band = jnp.where((s_j >= t_i - right) & (s_j <= t_i + left) & inside, 1.0, 0.0).astype(MXU_DTYPE)
        sl = slice(g * LANES, (g + 1) * LANES)
        cnt = (jnp.minimum(s_v + right + 1, seq) - jnp.maximum(s_v - left, 0)).astype(F32)
        scaled = jnp.where(inside_v, ext[:, sl] / jnp.maximum(cnt, 1.0), 0.0)
        outs.append(_sum3(jnp.dot(band, _split3(scaled), preferred_element_type=F32)) - main[:, sl])
    return jnp.concatenate(outs, axis=1)


def _halo_specs(tm, width, seq):
    per = tm // HALO
    last = seq // HALO - 1
    prev = pl.BlockSpec((HALO, width), lambda i: (jnp.maximum(i * per - 1, 0), 0))
    nxt = pl.BlockSpec((HALO, width), lambda i: (jnp.minimum((i + 1) * per, last), 0))
    return prev, nxt


def _rope(x, c, s1, s2, half):
    return x * c + pltpu.roll(x, half, 1) * s1 + pltpu.roll(x, LANES - half, 1) * s2


def _rope_t(d, c, s1, s2, half):
    return d * c + pltpu.roll(d * s1, LANES - half, 1) + pltpu.roll(d * s2, half, 1)


def _rope_tables(seq, nope, rope):
    half = rope // 2
    pos = jnp.arange(seq, dtype=F32)
    inv = ROPE_THETA ** (-jnp.arange(0, rope, 2, dtype=F32) / rope)
    ang = pos[:, None] * inv[None, :]
    cos, sin = jnp.cos(ang), jnp.sin(ang)
    zeros = lambda n: jnp.zeros((seq, n), F32)
    ones = lambda n: jnp.ones((seq, n), F32)
    tail = LANES - nope - rope
    c = jnp.concatenate([ones(nope), cos, cos, ones(tail)], axis=1)
    s1 = jnp.concatenate([zeros(nope + half), sin, zeros(tail)], axis=1)
    s2 = jnp.concatenate([zeros(nope), -sin, zeros(half + tail)], axis=1)
    return c, s1, s2


def _ffn_fwd(x, gain, wgu, wd):
    seq, d = x.shape
    f = wd.shape[0]
    tm, tn = _tile(seq, FFN_FWD_TM), _tile(f, FFN_FWD_TN)
    nk = f // tn

    def body(x_ref, g_ref, wg_ref, wu_ref, wd_ref, o_ref, h_sc, acc_sc):
        k = pl.program_id(1)

        @pl.when(k == 0)
        def _():
            xv = x_ref[...]
            h_sc[...] = (xv * _rstd(xv, d) * g_ref[...]).astype(MXU_DTYPE)
            acc_sc[...] = jnp.zeros_like(acc_sc)

        h = h_sc[...]
        g = jnp.dot(h, wg_ref[...], preferred_element_type=F32)
        u = jnp.dot(h, wu_ref[...], preferred_element_type=F32)
        a = g * (1.0 / (1.0 + jnp.exp(-g))) * u
        acc_sc[...] += _mm(a, wd_ref[...])

        @pl.when(k == nk - 1)
        def _():
            o_ref[...] = x_ref[...] + 0.5 * acc_sc[...]

    return pl.pallas_call(
        body, name="ffn_fwd", grid=(seq // tm, nk),
        in_specs=[pl.BlockSpec((tm, d), lambda i, k: (i, 0)),
                  pl.BlockSpec((1, d), lambda i, k: (0, 0)),
                  pl.BlockSpec((d, tn), lambda i, k: (0, k)),
                  pl.BlockSpec((d, tn), lambda i, k: (0, k + nk)),
                  pl.BlockSpec((tn, d), lambda i, k: (k, 0))],
        out_specs=pl.BlockSpec((tm, d), lambda i, k: (i, 0)),
        out_shape=jax.ShapeDtypeStruct((seq, d), F32),
        scratch_shapes=[pltpu.VMEM((tm, d), MXU_DTYPE), pltpu.VMEM((tm, d), F32)],
        compiler_params=_params("parallel", "arbitrary"),
    )(x, gain, wgu, wgu, wd)


def _ffn_bwd_act(x, gain, wgu, wdt, dout):
    seq, d = x.shape
    f = wdt.shape[1]
    tm, tn = _tile(seq, FFN_BWD_TM), _tile(f, FFN_BWD_TN)
    nk = f // tn

    def body(x_ref, g_ref, wg_ref, wu_ref, wdt_ref, do_ref, h_ref, act_ref, dg_ref, du_ref, dy_sc):
        @pl.when(pl.program_id(1) == 0)
        def _():
            xv = x_ref[...]
            h_ref[...] = (xv * _rstd(xv, d) * g_ref[...]).astype(MXU_DTYPE)
            dy_sc[...] = (0.5 * do_ref[...]).astype(MXU_DTYPE)

        h = h_ref[...]
        g = jnp.dot(h, wg_ref[...], preferred_element_type=F32)
        u = jnp.dot(h, wu_ref[...], preferred_element_type=F32)
        sig = 1.0 / (1.0 + jnp.exp(-g))
        silu = g * sig
        act_ref[...] = (silu * u).astype(MXU_DTYPE)
        da = jnp.dot(dy_sc[...], wdt_ref[...], preferred_element_type=F32)
        du_ref[...] = (da * silu).astype(MXU_DTYPE)
        dg_ref[...] = (da * u * (sig * (1.0 + g * (1.0 - sig)))).astype(MXU_DTYPE)

    row = lambda i, k: (i, 0)
    col = lambda i, k: (i, k)
    return pl.pallas_call(
        body, name="ffn_bwd_act", grid=(seq // tm, nk),
        in_specs=[pl.BlockSpec((tm, d), row),
                  pl.BlockSpec((1, d), lambda i, k: (0, 0)),
                  pl.BlockSpec((d, tn), lambda i, k: (0, k)),
                  pl.BlockSpec((d, tn), lambda i, k: (0, k + nk)),
                  pl.BlockSpec((d, tn), lambda i, k: (0, k)),
                  pl.BlockSpec((tm, d), row)],
        out_specs=[pl.BlockSpec((tm, d), row), pl.BlockSpec((tm, tn), col), pl.BlockSpec((tm, tn), col),
                   pl.BlockSpec((tm, tn), col)],
        out_shape=[jax.ShapeDtypeStruct((seq, d), MXU_DTYPE), jax.ShapeDtypeStruct((seq, f), MXU_DTYPE),
                   jax.ShapeDtypeStruct((seq, f), MXU_DTYPE), jax.ShapeDtypeStruct((seq, f), MXU_DTYPE)],
        scratch_shapes=[pltpu.VMEM((tm, d), MXU_DTYPE)],
        compiler_params=_params("parallel", "arbitrary"),
    )(x, gain, wgu, wgu, wdt, dout)


def _ffn_bwd_in(x, gain, wgut, dg, du, dout):
    seq, d = x.shape
    f = dg.shape[1]
    tm = _tile(seq, FFN_IN_TM)

    def body(x_ref, g_ref, wgut_ref, dg_ref, du_ref, do_ref, dx_ref, dgain_ref):
        @pl.when(pl.program_id(0) == 0)
        def _():
            dgain_ref[...] = jnp.zeros_like(dgain_ref)

        dh = (jnp.dot(dg_ref[...], wgut_ref[:f, :], preferred_element_type=F32)
              + jnp.dot(du_ref[...], wgut_ref[f:, :], preferred_element_type=F32))
        xv = x_ref[...]
        r = _rstd(xv, d)
        xhat = xv * r
        dgain_ref[...] += _colsum(dh * xhat)
        dx_ref[...] = do_ref[...] + _rms_bwd(dh, xhat, r, g_ref[...], d)

    row = lambda i: (i, 0)
    return pl.pallas_call(
        body, name="ffn_bwd_in", grid=(seq // tm,),
        in_specs=[pl.BlockSpec((tm, d), row), pl.BlockSpec((1, d), lambda i: (0, 0)),
                  pl.BlockSpec((2 * f, d), lambda i: (0, 0)), pl.BlockSpec((tm, f), row), pl.BlockSpec((tm, f), row),
                  pl.BlockSpec((tm, d), row)],
        out_specs=[pl.BlockSpec((tm, d), row), pl.BlockSpec((1, d), lambda i: (0, 0))],
        out_shape=[jax.ShapeDtypeStruct((seq, d), F32), jax.ShapeDtypeStruct((1, d), F32)],
        compiler_params=_params("arbitrary"),
    )(x, gain, wgut, dg, du, dout)


def _wgrad(a, b, scale, bm, bn):
    seq, m = a.shape
    n = b.shape[1]
    bm, bn, bk = _tile(m, bm), _tile(n, bn), _tile(seq, WG_BK)
    ns = seq // bk

    def body(a_ref, b_ref, o_ref):
        @pl.when(pl.program_id(2) == 0)
        def _():
            o_ref[...] = jnp.zeros_like(o_ref)

        o_ref[...] += scale * _mm_tn(a_ref[...], b_ref[...])

    return pl.pallas_call(
        body, name="wgrad", grid=(m // bm, n // bn, ns),
        in_specs=[pl.BlockSpec((bk, bm), lambda i, j, s: (s, i)),
                  pl.BlockSpec((bk, bn), lambda i, j, s: (s, j))],
        out_specs=pl.BlockSpec((bm, bn), lambda i, j, s: (i, j)),
        out_shape=jax.ShapeDtypeStruct((m, n), F32),
        compiler_params=_params("parallel", "parallel", "arbitrary"),
    )(a, b)


class _Dims:
    def __init__(self, d, ql, kvl, heads, head_dim, nope, vh, pool_w, seq):
        self.d, self.ql, self.kvl, self.heads, self.head_dim, self.nope, self.vh, self.pool_w = (
            d, ql, kvl, heads, head_dim, nope, vh, pool_w)
        self.rope = head_dim - nope
        self.half = self.rope // 2
        self.hw = heads * LANES
        self.vw = heads * vh
        self.ch = _tile(seq, ATTN_CH)
        self.nch = seq // self.ch
        self.o_kv = ql
        self.o_pe = ql + kvl
        self.o_pool = ql + kvl + LANES
        self.zw = self.o_pool + pool_w
        self.scale = head_dim ** -0.5


def _chunk_spec(dm, rows, tm):
    per = dm.ch // tm
    return pl.BlockSpec((dm.heads, 1, rows, tm), lambda i: (0, i // per, 0, i % per))


def _mixin_fwd(dm, x, gmix, win, gql, gkvl, wuq, wuk, wuv, gq, gk, rc, rs1, rs2):
    seq, d = x.shape
    tm = _tile(seq, MIX_TM)

    def body(x_ref, gmix_ref, win_ref, gql_ref, gkvl_ref, wuq_ref, wuk_ref, wuv_ref, gq_ref, gk_ref,
             rc_ref, rs1_ref, rs2_ref, q_ref, k_ref, v_ref, zp_ref):
        xv = x_ref[...]
        z = _mm(xv * _rstd(xv, d) * gmix_ref[...], win_ref[...])
        cq, ckv = z[:, :dm.o_kv], z[:, dm.o_kv:dm.o_pe]
        kpe = z[:, dm.o_pe:dm.o_pool]
        zp_ref[...] = z[:, dm.o_pool:]
        cqn = cq * _rstd(cq, dm.ql) * gql_ref[...]
        ckvn = ckv * _rstd(ckv, dm.kvl) * gkvl_ref[...]
        q = _mm(cqn, wuq_ref[...])
        kn = _mm(ckvn, wuk_ref[...])
        v_ref[...] = _mm(ckvn, wuv_ref[...]).T.reshape(dm.heads, 1, dm.vh, tm).astype(MXU_DTYPE)
        c, s1, s2 = rc_ref[...], rs1_ref[...], rs2_ref[...]
        for h in range(dm.heads):
            sl = slice(h * LANES, (h + 1) * LANES)
            qh = q[:, sl]
            qn = qh * _rstd(qh, dm.head_dim) * gq_ref[...]
            q_ref[:, sl] = (_rope(qn, c, s1, s2, dm.half) * dm.scale).astype(MXU_DTYPE)
            kh = kn[:, sl] + kpe
            kk = kh * _rstd(kh, dm.head_dim) * gk_ref[...]
            k_ref[:, sl] = _rope(kk, c, s1, s2, dm.half).astype(MXU_DTYPE)

    row = lambda i: (i, 0)
    full = lambda a: pl.BlockSpec(a.shape, lambda i: (0,) * a.ndim)
    return pl.pallas_call(
        body, name="mixin_fwd", grid=(seq // tm,),
        in_specs=[pl.BlockSpec((tm, d), row), full(gmix), full(win), full(gql), full(gkvl), full(wuq), full(wuk),
                  full(wuv), full(gq), full(gk),
                  pl.BlockSpec((tm, LANES), row), pl.BlockSpec((tm, LANES), row), pl.BlockSpec((tm, LANES), row)],
        out_specs=[pl.BlockSpec((tm, dm.hw), row), pl.BlockSpec((tm, dm.hw), row), _chunk_spec(dm, dm.vh, tm),
                   pl.BlockSpec((tm, dm.pool_w), row)],
        out_shape=[jax.ShapeDtypeStruct((seq, dm.hw), MXU_DTYPE), jax.ShapeDtypeStruct((seq, dm.hw), MXU_DTYPE),
                   jax.ShapeDtypeStruct((dm.heads, dm.nch, dm.vh, dm.ch), MXU_DTYPE),
                   jax.ShapeDtypeStruct((seq, dm.pool_w), F32)],
        compiler_params=_params("parallel"),
    )(x, gmix, win, gql, gkvl, wuq, wuk, wuv, gq, gk, rc, rs1, rs2)


def _mla_in_bwd(dm, x, gmix, win, gql, gkvl, wuq, wuk, wuv, gq, gk, rc, rs1, rs2, dq, dk, dv, dmixed):
    seq, d = x.shape
    tm = _tile(seq, MIX_TM)

    def body(x_ref, gmix_ref, win_ref, gql_ref, gkvl_ref, wuq_ref, wuk_ref, wuv_ref, gq_ref, gk_ref,
             rc_ref, rs1_ref, rs2_ref, dq_ref, dk_ref, dv_ref, dmp_ref, dm_ref, dmn_ref,
             dz_ref, dwuq_ref, dwuk_ref, dwuv_ref, dgql_ref, dgkvl_ref, dgq_ref, dgk_ref):
        i = pl.program_id(0)

        @pl.when(i == 0)
        def _():
            for ref in (dwuq_ref, dwuk_ref, dwuv_ref, dgql_ref, dgkvl_ref, dgq_ref, dgk_ref):
                ref[...] = jnp.zeros_like(ref)

        xv = x_ref[...]
        z = _mm(xv * _rstd(xv, d) * gmix_ref[...], win_ref[...])
        cq, ckv = z[:, :dm.o_kv], z[:, dm.o_kv:dm.o_pe]
        kpe = z[:, dm.o_pe:dm.o_pool]
        r_q, r_kv = _rstd(cq, dm.ql), _rstd(ckv, dm.kvl)
        cqh, ckvh = cq * r_q, ckv * r_kv
        cqn = (cqh * gql_ref[...]).astype(MXU_DTYPE)
        ckvn = (ckvh * gkvl_ref[...]).astype(MXU_DTYPE)
        q = _mm(cqn, wuq_ref[...])
        kn = _mm(ckvn, wuk_ref[...])
        c, s1, s2 = rc_ref[...], rs1_ref[...], rs2_ref[...]
        dqv = dq_ref[...].reshape(dm.hw, tm).T
        dq_pre, dk_pre = [], []
        dkpe = jnp.zeros((tm, LANES), F32)
        dgq = jnp.zeros((1, LANES), F32)
        dgk = jnp.zeros((1, LANES), F32)
        for h in range(dm.heads):
            sl = slice(h * LANES, (h + 1) * LANES)
            qh = q[:, sl]
            rq = _rstd(qh, dm.head_dim)
            xq = qh * rq
            dqn = _rope_t(dqv[:, sl] * dm.scale, c, s1, s2, dm.half)
            dgq += _colsum(dqn * xq)
            dq_pre.append(_rms_bwd(dqn, xq, rq, gq_ref[...], dm.head_dim))
            kh = kn[:, sl] + kpe
            rk = _rstd(kh, dm.head_dim)
            xk = kh * rk
            dkn = _rope_t(dk_ref[:, sl], c, s1, s2, dm.half)
            dgk += _colsum(dkn * xk)
            dkh = _rms_bwd(dkn, xk, rk, gk_ref[...], dm.head_dim)
            dk_pre.append(dkh)
            dkpe += dkh
        dgq_ref[...] += dgq
        dgk_ref[...] += dgk
        dq_pre = jnp.concatenate(dq_pre, axis=1).astype(MXU_DTYPE)
        dk_pre = jnp.concatenate(dk_pre, axis=1).astype(MXU_DTYPE)
        dvv = dv_ref[...].reshape(dm.vw, tm).T.astype(MXU_DTYPE)
        dwuq_ref[...] += _mm_tn(cqn, dq_pre)
        dwuk_ref[...] += _mm_tn(ckvn, dk_pre)
        dwuv_ref[...] += _mm_tn(ckvn, dvv)
        dcqn = _mm_nt(dq_pre, wuq_ref[...])
        dckvn = _mm_nt(dk_pre, wuk_ref[...]) + _mm_nt(dvv, wuv_ref[...])
        dgql_ref[...] += _colsum(dcqn * cqh)
        dgkvl_ref[...] += _colsum(dckvn * ckvh)
        dcq = _rms_bwd(dcqn, cqh, r_q, gql_ref[...], dm.ql)
        dckv = _rms_bwd(dckvn, ckvh, r_kv, gkvl_ref[...], dm.kvl)
        dzp = _pool_mixed_t(dmp_ref[...], dm_ref[...], dmn_ref[...], i * tm, seq)
        dz_ref[...] = jnp.concatenate([dcq, dckv, dkpe, dzp], axis=1).astype(MXU_DTYPE)

    row = lambda i: (i, 0)
    full = lambda a: pl.BlockSpec(a.shape, lambda i: (0,) * a.ndim)
    acc = lambda shape: pl.BlockSpec(shape, lambda i: (0, 0))
    prev, nxt = _halo_specs(tm, dm.pool_w, seq)
    shapes = [(seq, dm.zw), wuq.shape, wuk.shape, wuv.shape, (1, dm.ql), (1, dm.kvl), (1, LANES), (1, LANES)]
    return pl.pallas_call(
        body, name="mla_in_bwd", grid=(seq // tm,),
        in_specs=[pl.BlockSpec((tm, d), row), full(gmix), full(win), full(gql), full(gkvl), full(wuq), full(wuk),
                  full(wuv), full(gq), full(gk),
                  pl.BlockSpec((tm, LANES), row), pl.BlockSpec((tm, LANES), row), pl.BlockSpec((tm, LANES), row),
                  _chunk_spec(dm, LANES, tm), pl.BlockSpec((tm, dm.hw), row), _chunk_spec(dm, dm.vh, tm),
                  prev, pl.BlockSpec((tm, dm.pool_w), row), nxt],
        out_specs=[pl.BlockSpec((tm, dm.zw), row)] + [acc(s) for s in shapes[1:]],
        out_shape=[jax.ShapeDtypeStruct(shapes[0], MXU_DTYPE)] + [jax.ShapeDtypeStruct(s, F32) for s in shapes[1:]],
        compiler_params=_params("arbitrary"),
    )(x, gmix, win, gql, gkvl, wuq, wuk, wuv, gq, gk, rc, rs1, rs2, dq, dk, dv, dmixed, dmixed, dmixed)


def _rms_proj_bwd(x, gain, w, dz, gin):
    seq, d = x.shape
    n = w.shape[1]
    tm = _tile(seq, MIX_TM)

    def body(x_ref, g_ref, w_ref, dz_ref, gin_ref, gout_ref, dw_ref, dgain_ref):
        @pl.when(pl.program_id(0) == 0)
        def _():
            dw_ref[...] = jnp.zeros_like(dw_ref)
            dgain_ref[...] = jnp.zeros_like(dgain_ref)

        xv = x_ref[...]
        r = _rstd(xv, d)
        xhat = xv * r
        dzv = dz_ref[...]
        dh = _mm_nt(dzv, w_ref[...])
        dw_ref[...] += _mm_tn(xhat * g_ref[...], dzv)
        dgain_ref[...] += _colsum(dh * xhat)
        gout_ref[...] = gin_ref[...] + _rms_bwd(dh, xhat, r, g_ref[...], d)

    row = lambda i: (i, 0)
    return pl.pallas_call(
        body, name="rms_proj_bwd", grid=(seq // tm,),
        in_specs=[pl.BlockSpec((tm, d), row), pl.BlockSpec((1, d), lambda i: (0, 0)),
                  pl.BlockSpec((d, n), lambda i: (0, 0)), pl.BlockSpec((tm, n), row), pl.BlockSpec((tm, d), row)],
        out_specs=[pl.BlockSpec((tm, d), row), pl.BlockSpec((d, n), lambda i: (0, 0)),
                   pl.BlockSpec((1, d), lambda i: (0, 0))],
        out_shape=[jax.ShapeDtypeStruct((seq, d), F32), jax.ShapeDtypeStruct((d, n), F32),
                   jax.ShapeDtypeStruct((1, d), F32)],
        compiler_params=_params("arbitrary"),
    )(x, gain, w, dz, gin)


def _pool_branch(mixed, wpool_ref):
    groups = mixed.shape[1] // LANES
    return jnp.concatenate(
        [_mm(mixed[:, g * LANES:(g + 1) * LANES], wpool_ref[g]) for g in range(groups)], axis=1)


def _mixout_fwd(dm, x, zp, ot, wo, wpool, pscale):
    seq, d = x.shape
    tm = _tile(seq, MIX_TM)

    def body(x_ref, zpp_ref, zp_ref, zpn_ref, ot_ref, wo_ref, wpool_ref, ps_ref, out_ref):
        mixed = _pool_mixed(zpp_ref[...], zp_ref[...], zpn_ref[...], pl.program_id(0) * tm, seq)
        b = _pool_branch(mixed, wpool_ref) * ps_ref[...]
        a = _mm_tn(ot_ref[...].reshape(dm.vw, tm), wo_ref[:dm.vw, :])
        out_ref[...] = x_ref[...] + a + _mm(b, wo_ref[dm.vw:, :])

    row = lambda i: (i, 0)
    full = lambda a: pl.BlockSpec(a.shape, lambda i: (0,) * a.ndim)
    prev, nxt = _halo_specs(tm, dm.pool_w, seq)
    return pl.pallas_call(
        body, name="mixout_fwd", grid=(seq // tm,),
        in_specs=[pl.BlockSpec((tm, d), row), prev, pl.BlockSpec((tm, dm.pool_w), row), nxt,
                  _chunk_spec(dm, dm.vh, tm), full(wo), full(wpool), full(pscale)],
        out_specs=pl.BlockSpec((tm, d), row),
        out_shape=jax.ShapeDtypeStruct((seq, d), F32),
        compiler_params=_params("parallel"),
    )(x, zp, zp, zp, ot, wo, wpool, pscale)


def _mixout_bwd(dm, g, zp, ot, wo, wpool, pscale):
    seq, d = g.shape
    tm = _tile(seq, MIX_TM)
    groups = dm.pool_w // LANES

    def body(g_ref, zpp_ref, zp_ref, zpn_ref, ot_ref, wo_ref, wpool_ref, ps_ref,
             dot_ref, delta_ref, dmixed_ref, dwo_ref, dwpool_ref, dps_ref):
        i = pl.program_id(0)

        @pl.when(i == 0)
        def _():
            for ref in (dwo_ref, dwpool_ref, dps_ref):
                ref[...] = jnp.zeros_like(ref)

        gv = g_ref[...].astype(MXU_DTYPE)
        otv = ot_ref[...].reshape(dm.vw, tm)
        dat = _mm_nt(wo_ref[:dm.vw, :], gv)
        db = _mm_nt(gv, wo_ref[dm.vw:, :])
        mixed = _pool_mixed(zpp_ref[...], zp_ref[...], zpn_ref[...], i * tm, seq).astype(MXU_DTYPE)
        y = _pool_branch(mixed, wpool_ref)
        b = (y * ps_ref[...]).astype(MXU_DTYPE)
        dwo_ref[:dm.vw, :] += _mm(otv, gv)
        dwo_ref[dm.vw:, :] += _mm_tn(b, gv)
        dps_ref[...] += _colsum(db * y)
        dy = (db * ps_ref[...]).astype(MXU_DTYPE)
        dmx = []
        for gi in range(groups):
            sl = slice(gi * LANES, (gi + 1) * LANES)
            dmx.append(_mm_nt(dy[:, sl], wpool_ref[gi]))
            dwpool_ref[gi] += _mm_tn(mixed[:, sl], dy[:, sl])
        dmixed_ref[...] = jnp.concatenate(dmx, axis=1)
        dot_ref[...] = dat.reshape(dm.heads, 1, dm.vh, tm).astype(MXU_DTYPE)
        prod = dat * otv.astype(F32)
        for h in range(dm.heads):
            delta_ref[h, 0] = jnp.broadcast_to(_colsum(prod[h * dm.vh:(h + 1) * dm.vh]), (8, tm))

    row = lambda i: (i, 0)
    full = lambda a: pl.BlockSpec(a.shape, lambda i: (0,) * a.ndim)
    prev, nxt = _halo_specs(tm, dm.pool_w, seq)
    return pl.pallas_call(
        body, name="mixout_bwd", grid=(seq // tm,),
        in_specs=[pl.BlockSpec((tm, d), row), prev, pl.BlockSpec((tm, dm.pool_w), row), nxt,
                  _chunk_spec(dm, dm.vh, tm), full(wo), full(wpool), full(pscale)],
        out_specs=[_chunk_spec(dm, dm.vh, tm), _chunk_spec(dm, 8, tm), pl.BlockSpec((tm, dm.pool_w), row),
                   full(wo), full(wpool), full(pscale)],
        out_shape=[jax.ShapeDtypeStruct((dm.heads, dm.nch, dm.vh, dm.ch), MXU_DTYPE),
                   jax.ShapeDtypeStruct((dm.heads, dm.nch, 8, dm.ch), F32),
                   jax.ShapeDtypeStruct((seq, dm.pool_w), F32), jax.ShapeDtypeStruct(wo.shape, F32),
                   jax.ShapeDtypeStruct(wpool.shape, F32), jax.ShapeDtypeStruct(pscale.shape, F32)],
        compiler_params=_params("arbitrary"),
    )(g, zp, zp, zp, ot, wo, wpool, pscale)


def _attn_fwd(dm, q, k, vt):
    seq, ch, nch = q.shape[0], dm.ch, dm.nch

    def body(q_ref, k_ref, vt_ref, ot_ref, lse_ref, m_sc, l_sc, acc_sc, st0, st1, pt0, pt1):
        m_sc[...] = jnp.full_like(m_sc, -jnp.inf)
        l_sc[...] = jnp.zeros_like(l_sc)
        acc_sc[...] = jnp.zeros_like(acc_sc)
        qv = q_ref[...]

        def scores(j):
            return _mm_nt(k_ref[pl.ds(pl.multiple_of(j * ch, ch), ch), :], qv)

        def stage(j, st_cur, st_nxt, pt_cur, pt_prev):
            st_nxt[...] = scores(jnp.minimum(j + 1, nch - 1))
            acc = acc_sc[...] + _mm(vt_ref[0, jnp.maximum(j - 1, 0)], pt_prev[...])
            st = st_cur[...]
            m_prev = m_sc[...]
            m_new = jnp.maximum(m_prev, jnp.max(st, axis=0, keepdims=True))
            alpha = jnp.exp(m_prev - m_new)
            pt = jnp.exp(st - m_new)
            pt_cur[...] = pt.astype(MXU_DTYPE)
            l_sc[...] = alpha * l_sc[...] + _colsum(pt)
            acc_sc[...] = alpha * acc
            m_sc[...] = m_new

        st0[...] = scores(0)
        pt1[...] = jnp.zeros_like(pt1)

        def pair(jj, carry):
            stage(2 * jj, st0, st1, pt0, pt1)
            stage(2 * jj + 1, st1, st0, pt1, pt0)
            return carry

        lax.fori_loop(0, nch // 2, pair, 0)
        acc = acc_sc[...] + _mm(vt_ref[0, nch - 1], pt1[...])
        ot_ref[0, 0] = (acc / l_sc[...]).astype(MXU_DTYPE)
        lse_ref[0, 0] = jnp.broadcast_to(m_sc[...] + jnp.log(l_sc[...]), (8, ch))

    assert nch % 2 == 0
    chunk = lambda rows: pl.BlockSpec((1, 1, rows, ch), lambda h, i: (h, i, 0, 0))
    return pl.pallas_call(
        body, name="attn_fwd", grid=(dm.heads, nch),
        in_specs=[pl.BlockSpec((ch, LANES), lambda h, i: (i, h)),
                  pl.BlockSpec((seq, LANES), lambda h, i: (0, h)),
                  pl.BlockSpec((1, nch, dm.vh, ch), lambda h, i: (h, 0, 0, 0))],
        out_specs=[chunk(dm.vh), chunk(8)],
        out_shape=[jax.ShapeDtypeStruct((dm.heads, nch, dm.vh, ch), MXU_DTYPE),
                   jax.ShapeDtypeStruct((dm.heads, nch, 8, ch), F32)],
        scratch_shapes=[pltpu.VMEM((1, ch), F32), pltpu.VMEM((1, ch), F32), pltpu.VMEM((dm.vh, ch), F32),
                        pltpu.VMEM((ch, ch), F32), pltpu.VMEM((ch, ch), F32),
                        pltpu.VMEM((ch, ch), MXU_DTYPE), pltpu.VMEM((ch, ch), MXU_DTYPE)],
        compiler_params=_params("parallel", "parallel"),
    )(q, k, vt)


def _attn_bwd(dm, q, k, vt, dot, lse, delta):
    seq, ch, nch = q.shape[0], dm.ch, dm.nch

    def body(k_ref, vt_ref, q_ref, dot_ref, lse_ref, delta_ref, dqt_ref, dk_ref, dvt_ref):
        j = pl.program_id(1)
        kv = k_ref[...]
        kt = kv.astype(F32).T.astype(MXU_DTYPE)
        vtv = vt_ref[0, 0]
        dk_ref[...] = jnp.zeros_like(dk_ref)
        dvt_ref[...] = jnp.zeros_like(dvt_ref)

        def stage(i, carry):
            qi = q_ref[pl.ds(pl.multiple_of(i * ch, ch), ch), :]
            doti = dot_ref[0, i]
            pt = jnp.exp(_mm_nt(kv, qi) - lse_ref[0, i][:1])
            dst = (pt * (_mm_tn(vtv, doti) - delta_ref[0, i][:1])).astype(MXU_DTYPE)
            dvt_ref[0, 0] += _mm_nt(doti, pt)
            dk_ref[...] += _mm(dst, qi)
            dq_part = _mm(kt, dst)

            @pl.when(j == 0)
            def _():
                dqt_ref[0, i] = dq_part

            @pl.when(j > 0)
            def _():
                dqt_ref[0, i] += dq_part

            return carry

        lax.fori_loop(0, nch, stage, 0)

    whole = lambda rows: pl.BlockSpec((1, nch, rows, ch), lambda h, j: (h, 0, 0, 0))
    chunk = lambda rows: pl.BlockSpec((1, 1, rows, ch), lambda h, j: (h, j, 0, 0))
    return pl.pallas_call(
        body, name="attn_bwd", grid=(dm.heads, nch),
        in_specs=[pl.BlockSpec((ch, LANES), lambda h, j: (j, h)), chunk(dm.vh),
                  pl.BlockSpec((seq, LANES), lambda h, j: (0, h)), whole(dm.vh), whole(8), whole(8)],
        out_specs=[whole(LANES), pl.BlockSpec((ch, LANES), lambda h, j: (j, h)), chunk(dm.vh)],
        out_shape=[jax.ShapeDtypeStruct((dm.heads, nch, LANES, ch), F32), jax.ShapeDtypeStruct(q.shape, F32),
                   jax.ShapeDtypeStruct((dm.heads, nch, dm.vh, ch), F32)],
        compiler_params=_params("parallel", "arbitrary"),
    )(k, vt, q, dot, lse, delta)


def _loss_head(y, target):
    seq, d = y.shape
    tm = _tile(seq, LOSS_TM)

    def body(y_ref, t_ref, part_ref, dy_ref):
        @pl.when(pl.program_id(0) == 0)
        def _():
            part_ref[...] = jnp.zeros_like(part_ref)

        err = y_ref[...] - t_ref[...]
        part_ref[...] += _colsum(err * err)
        dy_ref[...] = err / d

    row = lambda i: (i, 0)
    return pl.pallas_call(
        body, name="loss_head", grid=(seq // tm,),
        in_specs=[pl.BlockSpec((tm, d), row), pl.BlockSpec((tm, d), row)],
        out_specs=[pl.BlockSpec((1, d), lambda i: (0, 0)), pl.BlockSpec((tm, d), row)],
        out_shape=[jax.ShapeDtypeStruct((1, d), F32), jax.ShapeDtypeStruct((seq, d), F32)],
        compiler_params=_params("arbitrary"),
    )(y, target)


def _my_place():
    return lax.axis_index("x"), lax.axis_index("y"), lax.axis_index("c")


def _all_gather(shards):
    n = len(shards)

    def body(*refs):
        x_refs, out_refs = refs[:n], refs[n:2 * n]
        send_sems, recv_sems, local_sems = refs[2 * n:]
        x, y, c = _my_place()
        me, sibling = (x, y, c), (x, y, 1 - c)
        chips = [(1 - x, y), (x, 1 - y), (1 - x, 1 - y)]

        def slot(t, px, py, pc):
            return out_refs[t].at[4 * px + 2 * py + pc]

        def copy(t, k, block, to, src=None):
            return pltpu.make_async_remote_copy(
                src_ref=slot(t, *block) if src is None else src, dst_ref=slot(t, *block),
                send_sem=send_sems.at[t, k], recv_sem=recv_sems.at[t, k], device_id=to, device_id_type=MESH)

        mine = [pltpu.make_async_copy(x_refs[t], slot(t, *me), local_sems.at[t]) for t in range(n)]
        started = []
        for t in range(n):
            mine[t].start()
            first = [copy(t, 0, me, sibling, src=x_refs[t])]
            first += [copy(t, 1 + j, me, (*chip, c), src=x_refs[t]) for j, chip in enumerate(chips)]
            for cp in first:
                cp.start()
            started += first
        for j, chip in enumerate(chips):
            for t in range(n):
                copy(t, 1 + j, (*chip, c), me).wait_recv()
                passed = copy(t, 4 + j, (*chip, c), sibling)
                passed.start()
                started.append(passed)
        for t in range(n):
            copy(t, 0, sibling, me).wait_recv()
            for j, chip in enumerate(chips):
                copy(t, 4 + j, (*chip, 1 - c), me).wait_recv()
        for cp in started:
            cp.wait_send()
        for cp in mine:
            cp.wait()

    return pl.pallas_call(
        body, name="weights_all_gather",
        out_shape=[jax.ShapeDtypeStruct((N_DEV,) + s.shape, s.dtype) for s in shards],
        in_specs=[pl.BlockSpec(memory_space=pl.ANY)] * n,
        out_specs=[pl.BlockSpec(memory_space=pl.ANY)] * n,
        scratch_shapes=[pltpu.SemaphoreType.DMA((n, 7)), pltpu.SemaphoreType.DMA((n, 7)),
                        pltpu.SemaphoreType.DMA((n,))],
    )(*shards)


def _grad_exchange(blocks, common):
    n, nc = len(blocks), len(common)

    def body(*refs):
        g_refs, c_refs = refs[:n], refs[n:n + nc]
        out_refs = refs[n + nc:2 * (n + nc)]
        send_sems, recv_sems, local_sems = refs[2 * (n + nc):]
        x, y, c = _my_place()
        me = 4 * x + 2 * y + c
        srcs = [lambda dev, r=r: r.at[dev] for r in g_refs] + [lambda dev, r=r: r for r in c_refs]
        local = [pltpu.make_async_copy(srcs[t](me), out_refs[t].at[me], local_sems.at[t]) for t in range(n + nc)]
        for cp in local:
            cp.start()
        copies = []
        for k in range(1, N_DEV):
            px = 1 - x if k & 4 else x
            py = 1 - y if k & 2 else y
            pc = 1 - c if k & 1 else c
            for t in range(n + nc):
                copies.append(pltpu.make_async_remote_copy(
                    src_ref=srcs[t](4 * px + 2 * py + pc), dst_ref=out_refs[t].at[me],
                    send_sem=send_sems.at[t, k - 1], recv_sem=recv_sems.at[t, k - 1],
                    device_id=(px, py, pc), device_id_type=MESH))
        for cp in copies:
            cp.start()
        for cp in copies:
            cp.wait_recv()
        for cp in copies:
            cp.wait_send()
        for cp in local:
            cp.wait()

    return pl.pallas_call(
        body, name="grad_exchange",
        out_shape=[jax.ShapeDtypeStruct(b.shape, b.dtype) for b in blocks]
        + [jax.ShapeDtypeStruct((N_DEV,) + a.shape, a.dtype) for a in common],
        in_specs=[pl.BlockSpec(memory_space=pl.ANY)] * (n + nc),
        out_specs=[pl.BlockSpec(memory_space=pl.ANY)] * (n + nc),
        scratch_shapes=[pltpu.SemaphoreType.DMA((n + nc, 7)), pltpu.SemaphoreType.DMA((n + nc, 7)),
                        pltpu.SemaphoreType.DMA((n + nc,))],
    )(*blocks, *common)


def _adamw(parts, w, m, v):
    rows, width = w.shape
    tr = _tile(rows, max(8, ADAM_BLOCK // width // 8 * 8))

    def body(p_ref, w_ref, m_ref, v_ref, g_ref, d_ref, nm_ref, nv_ref):
        g = p_ref[0].astype(F32)
        for s in range(1, N_DEV):
            g = g + p_ref[s].astype(F32)
        nm = ADAM_B1 * m_ref[...] + (1.0 - ADAM_B1) * g
        nv = ADAM_B2 * v_ref[...] + (1.0 - ADAM_B2) * (g * g)
        m_hat = nm / (1.0 - ADAM_B1 ** ADAM_STEP)
        v_hat = nv / (1.0 - ADAM_B2 ** ADAM_STEP)
        g_ref[...] = g
        d_ref[...] = -ADAM_LR * (m_hat / (jnp.sqrt(v_hat) + ADAM_EPS) + ADAM_WD * w_ref[...])
        nm_ref[...] = nm
        nv_ref[...] = nv

    row = pl.BlockSpec((tr, width), lambda i: (i, 0))
    return pl.pallas_call(
        body, name="adamw", grid=(rows // tr,),
        in_specs=[pl.BlockSpec((N_DEV, tr, width), lambda i: (0, i, 0)), row, row, row],
        out_specs=[row] * 4,
        out_shape=[jax.ShapeDtypeStruct(w.shape, F32)] * 4,
        compiler_params=_params("parallel"),
    )(parts, w, m, v)


def _pack_rows(flat_parts, multiple):
    flat = jnp.concatenate([p.reshape(-1) for p in flat_parts])
    chunk = multiple * PACK_W
    pad = (-flat.shape[0]) % chunk
    if pad:
        flat = jnp.concatenate([flat, jnp.zeros((pad,), flat.dtype)])
    return flat.reshape(-1, PACK_W)


def _unpack(packed, shapes):
    flat = packed.reshape(-1)
    out, off = [], 0
    for shape in shapes:
        size = 1
        for s in shape:
            size *= s
        out.append(flat[off:off + size].reshape(shape))
        off += size
    return out


def _to_full(name, g):
    n, l, a, b = g.shape
    if name in ROW_SHARDED:
        return jnp.transpose(g, (1, 0, 2, 3)).reshape(l, n * a, b)
    return jnp.transpose(g, (1, 2, 0, 3)).reshape(l, a, n * b)


def _to_shards(name, full):
    l, a, b = full.shape
    if name in ROW_SHARDED:
        return jnp.transpose(full.reshape(l, N_DEV, a // N_DEV, b), (1, 0, 2, 3))
    return jnp.transpose(full.reshape(l, a, N_DEV, b // N_DEV), (2, 0, 1, 3))


def _pad_heads(w, heads, real):
    lead = w.shape[:-1]
    w = w.reshape(lead + (heads, real))
    w = jnp.concatenate([w, jnp.zeros(lead + (heads, LANES - real), w.dtype)], axis=-1)
    return w.reshape(lead + (heads * LANES,))


def _unpad_heads(w, heads, real):
    lead = w.shape[:-1]
    return w.reshape(lead + (heads, LANES))[..., :real].reshape(lead + (heads * real,))


def _pad_lanes(v, before):
    l, n = v.shape
    return jnp.concatenate([jnp.zeros((l, before), v.dtype), v, jnp.zeros((l, LANES - before - n), v.dtype)], axis=1)


def kernel(x, ffn1_norm, ffn1_w_gu, ffn1_w_down, mix_norm, w_in, q_lat_norm, kv_lat_norm, w_uq, w_uk, w_uv, q_norm, k_norm, w_pool, pool_scale, w_out, ffn2_norm, ffn2_w_gu, ffn2_w_down, loss_target, m_ffn1_norm, m_ffn1_w_gu, m_ffn1_w_down, m_mix_norm, m_w_in, m_q_lat_norm, m_kv_lat_norm, m_w_uq, m_w_uk, m_w_uv, m_q_norm, m_k_norm, m_w_pool, m_pool_scale, m_w_out, m_ffn2_norm, m_ffn2_w_gu, m_ffn2_w_down, v_ffn1_norm, v_ffn1_w_gu, v_ffn1_w_down, v_mix_norm, v_w_in, v_q_lat_norm, v_kv_lat_norm, v_w_uq, v_w_uk, v_w_uv, v_q_norm, v_k_norm, v_w_pool, v_pool_scale, v_w_out, v_ffn2_norm, v_ffn2_w_gu, v_ffn2_w_down):
    given = dict(locals())
    wts = {n: given[n] for n in WEIGHTS}
    mom1 = {n: given["m_" + n] for n in WEIGHTS}
    mom2 = {n: given["v_" + n] for n in WEIGHTS}

    depth, d = ffn1_norm.shape
    seq = x.shape[1]
    dff = ffn1_w_down.shape[1] * N_DEV
    ql, kvl, head_dim = q_lat_norm.shape[1], kv_lat_norm.shape[1], q_norm.shape[1]
    heads = w_uq.shape[2] * N_DEV // head_dim
    nope = w_uk.shape[2] * N_DEV // heads
    vh = w_uv.shape[2] * N_DEV // heads
    groups, gdim = w_pool.shape[1], w_pool.shape[2]
    pool_w = groups * gdim
    assert gdim == LANES and groups == len(POOL_WINDOWS) and head_dim <= LANES and vh <= LANES
    assert d % LANES == 0 and ql % LANES == 0 and kvl % LANES == 0 and seq % HALO == 0
    dm = _Dims(d, ql, kvl, heads, head_dim, nope, vh, pool_w, seq)

    gathered = _all_gather([wts[n].astype(MXU_DTYPE) for n in SHARDED])
    full = {n: _to_full(n, g) for n, g in zip(SHARDED, gathered)}

    zpad = jnp.zeros((depth, d, LANES), MXU_DTYPE)
    win_p = jnp.concatenate(
        [full["w_in"][..., :dm.o_pe], zpad[..., :nope], full["w_in"][..., dm.o_pe:dm.o_pe + dm.rope],
         zpad[..., :LANES - nope - dm.rope], full["w_in"][..., dm.o_pe + dm.rope:]], axis=-1)
    wuq_p = _pad_heads(full["w_uq"], heads, head_dim)
    wuk_p = _pad_heads(full["w_uk"], heads, nope)
    wuv, wo = full["w_uv"], full["w_out"]
    gq_p, gk_p = _pad_lanes(q_norm, 0), _pad_lanes(k_norm, 0)
    wpool_c = w_pool.astype(MXU_DTYPE)
    rc, rs1, rs2 = _rope_tables(seq, nope, dm.rope)
    row = lambda a, l: a[l][None, :]

    h = x[0]
    saved = []
    for l in range(depth):
        x0 = h
        x1 = _ffn_fwd(x0, row(ffn1_norm, l), full["ffn1_w_gu"][l], full["ffn1_w_down"][l])
        q, k, v, zp = _mixin_fwd(dm, x1, row(mix_norm, l), win_p[l], row(q_lat_norm, l), row(kv_lat_norm, l),
                                 wuq_p[l], wuk_p[l], wuv[l], row(gq_p, l), row(gk_p, l), rc, rs1, rs2)
        o, lse = _attn_fwd(dm, q, k, v)
        x2 = _mixout_fwd(dm, x1, zp, o, wo[l], wpool_c[l], row(pool_scale, l))
        h = _ffn_fwd(x2, row(ffn2_norm, l), full["ffn2_w_gu"][l], full["ffn2_w_down"][l])
        saved.append((x0, x1, x2, q, k, v, zp, o, lse))

    part, g = _loss_head(h, loss_target[0])
    loss = lax.psum(0.5 / d * jnp.sum(part), ("x", "y", "c"))

    grads = {n: [None] * depth for n in WEIGHTS}

    def ffn_grads(prefix, l, xin, gout):
        wgu, wd = full[prefix + "_w_gu"][l], full[prefix + "_w_down"][l]
        gain = row(given[prefix + "_norm"], l)
        hh, act, dg, du = _ffn_bwd_act(xin, gain, wgu, wd.T, gout)
        gin, dgain = _ffn_bwd_in(xin, gain, wgu.T, dg, du, gout)
        grads[prefix + "_norm"][l] = dgain[0]
        grads[prefix + "_w_gu"][l] = jnp.concatenate(
            [_wgrad(hh, dg, 1.0, 1024, 1408), _wgrad(hh, du, 1.0, 1024, 1408)], axis=1)
        grads[prefix + "_w_down"][l] = _wgrad(act, gout, 0.5, 1408, 1024)
        return gin

    for l in reversed(range(depth)):
        x0, x1, x2, q, k, v, zp, o, lse = saved[l]
        g = ffn_grads("ffn2", l, x2, g)
        do, delta, dmixed, dwo, dwpool, dps = _mixout_bwd(dm, g, zp, o, wo[l], wpool_c[l], row(pool_scale, l))
        dq, dk, dv = _attn_bwd(dm, q, k, v, do, lse, delta)
        dz, dwuq, dwuk, dwuv, dgql, dgkvl, dgq, dgk = _mla_in_bwd(
            dm, x1, row(mix_norm, l), win_p[l], row(q_lat_norm, l), row(kv_lat_norm, l), wuq_p[l], wuk_p[l],
            wuv[l], row(gq_p, l), row(gk_p, l), rc, rs1, rs2, dq, dk, dv, dmixed)
        g, dwin, dgmix = _rms_proj_bwd(x1, row(mix_norm, l), win_p[l], dz, g)
        grads["w_out"][l] = dwo
        grads["w_pool"][l] = dwpool
        grads["pool_scale"][l] = dps[0]
        grads["w_uq"][l] = _unpad_heads(dwuq, heads, head_dim)
        grads["w_uk"][l] = _unpad_heads(dwuk, heads, nope)
        grads["w_uv"][l] = dwuv
        grads["q_lat_norm"][l] = dgql[0]
        grads["kv_lat_norm"][l] = dgkvl[0]
        grads["q_norm"][l] = dgq[0, :head_dim]
        grads["k_norm"][l] = dgk[0, :head_dim]
        grads["w_in"][l] = jnp.concatenate(
            [dwin[:, :dm.o_pe], dwin[:, dm.o_pe + nope:dm.o_pe + nope + dm.rope], dwin[:, dm.o_pool:]], axis=1)
        grads["mix_norm"][l] = dgmix[0]
        g = ffn_grads("ffn1", l, x0, g)

    grads = {n: jnp.stack(grads[n]) for n in WEIGHTS}

    repl_shapes = [wts[n].shape for n in REPLICATED]
    received = _grad_exchange([_to_shards(n, grads[n]).astype(WIRE_DTYPE) for n in SHARDED],
                              [_pack_rows([grads[n] for n in REPLICATED], 8)])
    outs = [{}, {}, {}, {}]
    for n, parts in zip(SHARDED, received):
        l, a, b = wts[n].shape
        results = _adamw(parts.reshape(N_DEV, l * a, b),
                         *[src[n].reshape(l * a, b) for src in (wts, mom1, mom2)])
        for o, r in zip(outs, results):
            o[n] = r.reshape(l, a, b)
    results = _adamw(received[-1], *[_pack_rows([src[n] for n in REPLICATED], 8) for src in (wts, mom1, mom2)])
    for o, r in zip(outs, results):
        o.update(zip(REPLICATED, _unpack(r, repl_shapes)))

    return (loss, g[None], *[o[n] for o in outs for n in WEIGHTS])
```

```python
import functools

import jax
import jax.numpy as jnp
from jax import lax
from jax.experimental import pallas as pl
from jax.experimental.pallas import tpu as pltpu

F32 = jnp.float32
MXU_DTYPE = jnp.bfloat16
WIRE_DTYPE = jnp.bfloat16
EPS = 1e-6
ROPE_THETA = 10000.0
POOL_WINDOWS = (2, 4, 8, 16)
ADAM_LR, ADAM_B1, ADAM_B2, ADAM_EPS, ADAM_WD, ADAM_STEP = 0.001, 0.9, 0.999, 1e-08, 0.01, 10

LANES = 128
HALO = 64
PACK_W = 1024
N_DEV = 8
VMEM_LIMIT = 56 * 1024 * 1024
MESH = pl.DeviceIdType.MESH

LOSS_TM = 1024
FFN_FWD_TM, FFN_FWD_TN = 512, 1408
FFN_BWD_TM, FFN_BWD_TN = 256, 1408
FFN_IN_TM = 256
MIX_TM = 256
ATTN_CH = 1024
WG_BK = 2048
ADAM_BLOCK = 128 * 1024

SHARDED = ("ffn1_w_gu", "ffn1_w_down", "w_in", "w_uq", "w_uk", "w_uv", "w_out", "ffn2_w_gu", "ffn2_w_down")
ROW_SHARDED = ("ffn1_w_down", "w_out", "ffn2_w_down")
REPLICATED = ("ffn1_norm", "mix_norm", "q_lat_norm", "kv_lat_norm", "q_norm", "k_norm", "w_pool", "pool_scale",
              "ffn2_norm")
WEIGHTS = ("ffn1_norm", "ffn1_w_gu", "ffn1_w_down", "mix_norm", "w_in", "q_lat_norm", "kv_lat_norm", "w_uq", "w_uk",
           "w_uv", "q_norm", "k_norm", "w_pool", "pool_scale", "w_out", "ffn2_norm", "ffn2_w_gu", "ffn2_w_down")


def _tile(n, pref):
    if n <= pref:
        return n
    t = pref - pref % 8
    while n % t:
        t -= 8
    return t


def _params(*sem):
    return pltpu.CompilerParams(dimension_semantics=sem, vmem_limit_bytes=VMEM_LIMIT)


def _mm(a, b):
    return jnp.dot(a.astype(MXU_DTYPE), b.astype(MXU_DTYPE), preferred_element_type=F32)


def _mm_nt(a, b):
    return lax.dot_general(a.astype(MXU_DTYPE), b.astype(MXU_DTYPE), (((1,), (1,)), ((), ())),
                           preferred_element_type=F32)


def _mm_tn(a, b):
    return lax.dot_general(a.astype(MXU_DTYPE), b.astype(MXU_DTYPE), (((0,), (0,)), ((), ())),
                           preferred_element_type=F32)


def _rstd(x, n):
    return lax.rsqrt(jnp.sum(x * x, axis=-1, keepdims=True) / n + EPS)


def _rms_bwd(dy, xhat, r, gain, n):
    dxh = dy * gain
    return r * (dxh - xhat * (jnp.sum(dxh * xhat, axis=-1, keepdims=True) / n))


def _colsum(x):
    return jnp.sum(x, axis=0, keepdims=True)


def _split3(x):
    hi = x.astype(MXU_DTYPE)
    r1 = x - hi.astype(F32)
    mid = r1.astype(MXU_DTYPE)
    lo = (r1 - mid.astype(F32)).astype(MXU_DTYPE)
    return jnp.concatenate([hi, mid, lo], axis=1)


def _sum3(x):
    n = x.shape[1] // 3
    return (x[:, :n] + x[:, n:2 * n]) + x[:, 2 * n:]


def _pool_mixed(prev, main, nxt, row0, seq):
    tm = main.shape[0]
    k = tm + 2 * HALO
    ext = jnp.concatenate([prev, main, nxt], axis=0)
    s_i = lax.broadcasted_iota(jnp.int32, (tm, k), 0) + row0
    t_j = lax.broadcasted_iota(jnp.int32, (tm, k), 1) + (row0 - HALO)
    s_v = lax.broadcasted_iota(jnp.int32, (tm, 1), 0) + row0
    inside = (t_j >= 0) & (t_j < seq)
    outs = []
    for g, w in enumerate(POOL_WINDOWS):
        left, right = w // 2, w - 1 - w // 2
        band = jnp.where((t_j >= s_i - left) & (t_j <= s_i + right) & inside, 1.0, 0.0).astype(MXU_DTYPE)
        sl = slice(g * LANES, (g + 1) * LANES)
        wsum = _sum3(jnp.dot(band, _split3(ext[:, sl]), preferred_element_type=F32))
        cnt = (jnp.minimum(s_v + right + 1, seq) - jnp.maximum(s_v - left, 0)).astype(F32)
        outs.append(wsum / cnt - main[:, sl])
    return jnp.concatenate(outs, axis=1)


def _pool_mixed_t(prev, main, nxt, row0, seq):
    tm = main.shape[0]
    k = tm + 2 * HALO
    ext = jnp.concatenate([prev, main, nxt], axis=0)
    t_i = lax.broadcasted_iota(jnp.int32, (tm, k), 0) + row0
    s_j = lax.broadcasted_iota(jnp.int32, (tm, k), 1) + (row0 - HALO)
    s_v = lax.broadcasted_iota(jnp.int32, (k, 1), 0) + (row0 - HALO)
    inside = (s_j >= 0) & (s_j < seq)
    inside_v = (s_v >= 0) & (s_v < seq)
    outs = []
    for g, w in enumerate(POOL_WINDOWS):
        left, right = w // 2, w - 1 - w // 2
        band = jnp.where((s_j >= t_i - right) & (s_j <= t_i + left) & inside, 1.0, 0.0).astype(MXU_DTYPE)
        sl = slice(g * LANES, (g + 1) * LANES)
        cnt = (jnp.minimum(s_v + right + 1, seq) - jnp.maximum(s_v - left, 0)).astype(F32)
        scaled = jnp.where(inside_v, ext[:, sl] / jnp.maximum(cnt, 1.0), 0.0)
        outs.append(_sum3(jnp.dot(band, _split3(scaled), preferred_element_type=F32)) - main[:, sl])
    return jnp.concatenate(outs, axis=1)


def _halo_specs(tm, width, seq):
    per = tm // HALO
    last = seq // HALO - 1
    prev = pl.BlockSpec((HALO, width), lambda i: (jnp.maximum(i * per - 1, 0), 0))
    nxt = pl.BlockSpec((HALO, width), lambda i: (jnp.minimum((i + 1) * per, last), 0))
    return prev, nxt


def _rope(x, c, s1, s2, half):
    return x * c + pltpu.roll(x, half, 1) * s1 + pltpu.roll(x, LANES - half, 1) * s2


def _rope_t(d, c, s1, s2, half):
    return d * c + pltpu.roll(d * s1, LANES - half, 1) + pltpu.roll(d * s2, half, 1)


def _rope_tables(seq, nope, rope):
    half = rope // 2
    pos = jnp.arange(seq, dtype=F32)
    inv = ROPE_THETA ** (-jnp.arange(0, rope, 2, dtype=F32) / rope)
    ang = pos[:, None] * inv[None, :]
    cos, sin = jnp.cos(ang), jnp.sin(ang)
    zeros = lambda n: jnp.zeros((seq, n), F32)
    ones = lambda n: jnp.ones((seq, n), F32)
    tail = LANES - nope - rope
    c = jnp.concatenate([ones(nope), cos, cos, ones(tail)], axis=1)
    s1 = jnp.concatenate([zeros(nope + half), sin, zeros(tail)], axis=1)
    s2 = jnp.concatenate([zeros(nope), -sin, zeros(half + tail)], axis=1)
    return c, s1, s2


def _ffn_fwd(x, gain, wgu, wd):
    seq, d = x.shape
    f = wd.shape[0]
    tm, tn = _tile(seq, FFN_FWD_TM), _tile(f, FFN_FWD_TN)
    nk = f // tn

    def body(x_ref, g_ref, wg_ref, wu_ref, wd_ref, o_ref, h_sc, acc_sc):
        k = pl.program_id(1)

        @pl.when(k == 0)
        def _():
            xv = x_ref[...]
            h_sc[...] = (xv * _rstd(xv, d) * g_ref[...]).astype(MXU_DTYPE)
            acc_sc[...] = jnp.zeros_like(acc_sc)

        h = h_sc[...]
        g = jnp.dot(h, wg_ref[...], preferred_element_type=F32)
        u = jnp.dot(h, wu_ref[...], preferred_element_type=F32)
        a = g * (1.0 / (1.0 + jnp.exp(-g))) * u
        acc_sc[...] += _mm(a, wd_ref[...])

        @pl.when(k == nk - 1)
        def _():
            o_ref[...] = x_ref[...] + 0.5 * acc_sc[...]

    return pl.pallas_call(
        body, name="ffn_fwd", grid=(seq // tm, nk),
        in_specs=[pl.BlockSpec((tm, d), lambda i, k: (i, 0)),
                  pl.BlockSpec((1, d), lambda i, k: (0, 0)),
                  pl.BlockSpec((d, tn), lambda i, k: (0, k)),
                  pl.BlockSpec((d, tn), lambda i, k: (0, k + nk)),
                  pl.BlockSpec((tn, d), lambda i, k: (k, 0))],
        out_specs=pl.BlockSpec((tm, d), lambda i, k: (i, 0)),
        out_shape=jax.ShapeDtypeStruct((seq, d), F32),
        scratch_shapes=[pltpu.VMEM((tm, d), MXU_DTYPE), pltpu.VMEM((tm, d), F32)],
        compiler_params=_params("parallel", "arbitrary"),
    )(x, gain, wgu, wgu, wd)


def _ffn_bwd_act(x, gain, wgu, wdt, dout):
    seq, d = x.shape
    f = wdt.shape[1]
    tm, tn = _tile(seq, FFN_BWD_TM), _tile(f, FFN_BWD_TN)
    nk = f // tn

    def body(x_ref, g_ref, wg_ref, wu_ref, wdt_ref, do_ref, h_ref, act_ref, dg_ref, du_ref, dy_sc):
        @pl.when(pl.program_id(1) == 0)
        def _():
            xv = x_ref[...]
            h_ref[...] = (xv * _rstd(xv, d) * g_ref[...]).astype(MXU_DTYPE)
            dy_sc[...] = (0.5 * do_ref[...]).astype(MXU_DTYPE)

        h = h_ref[...]
        g = jnp.dot(h, wg_ref[...], preferred_element_type=F32)
        u = jnp.dot(h, wu_ref[...], preferred_element_type=F32)
        sig = 1.0 / (1.0 + jnp.exp(-g))
        silu = g * sig
        act_ref[...] = (silu * u).astype(MXU_DTYPE)
        da = jnp.dot(dy_sc[...], wdt_ref[...], preferred_element_type=F32)
        du_ref[...] = (da * silu).astype(MXU_DTYPE)
        dg_ref[...] = (da * u * (sig * (1.0 + g * (1.0 - sig)))).astype(MXU_DTYPE)

    row = lambda i, k: (i, 0)
    col = lambda i, k: (i, k)
    return pl.pallas_call(
        body, name="ffn_bwd_act", grid=(seq // tm, nk),
        in_specs=[pl.BlockSpec((tm, d), row),
                  pl.BlockSpec((1, d), lambda i, k: (0, 0)),
                  pl.BlockSpec((d, tn), lambda i, k: (0, k)),
                  pl.BlockSpec((d, tn), lambda i, k: (0, k + nk)),
                  pl.BlockSpec((d, tn), lambda i, k: (0, k)),
                  pl.BlockSpec((tm, d), row)],
        out_specs=[pl.BlockSpec((tm, d), row), pl.BlockSpec((tm, tn), col), pl.BlockSpec((tm, tn), col),
                   pl.BlockSpec((tm, tn), col)],
        out_shape=[jax.ShapeDtypeStruct((seq, d), MXU_DTYPE), jax.ShapeDtypeStruct((seq, f), MXU_DTYPE),
                   jax.ShapeDtypeStruct((seq, f), MXU_DTYPE), jax.ShapeDtypeStruct((seq, f), MXU_DTYPE)],
        scratch_shapes=[pltpu.VMEM((tm, d), MXU_DTYPE)],
        compiler_params=_params("parallel", "arbitrary"),
    )(x, gain, wgu, wgu, wdt, dout)


def _ffn_bwd_in(x, gain, wgut, dg, du, dout):
    seq, d = x.shape
    f = dg.shape[1]
    tm = _tile(seq, FFN_IN_TM)

    def body(x_ref, g_ref, wgut_ref, dg_ref, du_ref, do_ref, dx_ref, dgain_ref):
        @pl.when(pl.program_id(0) == 0)
        def _():
            dgain_ref[...] = jnp.zeros_like(dgain_ref)

        dh = (jnp.dot(dg_ref[...], wgut_ref[:f, :], preferred_element_type=F32)
              + jnp.dot(du_ref[...], wgut_ref[f:, :], preferred_element_type=F32))
        xv = x_ref[...]
        r = _rstd(xv, d)
        xhat = xv * r
        dgain_ref[...] += _colsum(dh * xhat)
        dx_ref[...] = do_ref[...] + _rms_bwd(dh, xhat, r, g_ref[...], d)

    row = lambda i: (i, 0)
    return pl.pallas_call(
        body, name="ffn_bwd_in", grid=(seq // tm,),
        in_specs=[pl.BlockSpec((tm, d), row), pl.BlockSpec((1, d), lambda i: (0, 0)),
                  pl.BlockSpec((2 * f, d), lambda i: (0, 0)), pl.BlockSpec((tm, f), row), pl.BlockSpec((tm, f), row),
                  pl.BlockSpec((tm, d), row)],
        out_specs=[pl.BlockSpec((tm, d), row), pl.BlockSpec((1, d), lambda i: (0, 0))],
        out_shape=[jax.ShapeDtypeStruct((seq, d), F32), jax.ShapeDtypeStruct((1, d), F32)],
        compiler_params=_params("arbitrary"),
    )(x, gain, wgut, dg, du, dout)


def _wgrad(a, b, scale, bm, bn):
    seq, m = a.shape
    n = b.shape[1]
    bm, bn, bk = _tile(m, bm), _tile(n, bn), _tile(seq, WG_BK)
    ns = seq // bk

    def body(a_ref, b_ref, o_ref):
        @pl.when(pl.program_id(2) == 0)
        def _():
            o_ref[...] = jnp.zeros_like(o_ref)

        o_ref[...] += scale * _mm_tn(a_ref[...], b_ref[...])

    return pl.pallas_call(
        body, name="wgrad", grid=(m // bm, n // bn, ns),
        in_specs=[pl.BlockSpec((bk, bm), lambda i, j, s: (s, i)),
                  pl.BlockSpec((bk, bn), lambda i, j, s: (s, j))],
        out_specs=pl.BlockSpec((bm, bn), lambda i, j, s: (i, j)),
        out_shape=jax.ShapeDtypeStruct((m, n), F32),
        compiler_params=_params("parallel", "parallel", "arbitrary"),
    )(a, b)


class _Dims:
    def __init__(self, d, ql, kvl, heads, head_dim, nope, vh, pool_w, seq):
        self.d, self.ql, self.kvl, self.heads, self.head_dim, self.nope, self.vh, self.pool_w = (
            d, ql, kvl, heads, head_dim, nope, vh, pool_w)
        self.rope = head_dim - nope
        self.half = self.rope // 2
        self.hw = heads * LANES
        self.vw = heads * vh
        self.ch = _tile(seq, ATTN_CH)
        self.nch = seq // self.ch
        self.o_kv = ql
        self.o_pe = ql + kvl
        self.o_pool = ql + kvl + LANES
        self.zw = self.o_pool + pool_w
        self.scale = head_dim ** -0.5


def _chunk_spec(dm, rows, tm):
    per = dm.ch // tm
    return pl.BlockSpec((dm.heads, 1, rows, tm), lambda i: (0, i // per, 0, i % per))


def _mixin_fwd(dm, x, gmix, win, gql, gkvl, wuq, wuk, wuv, gq, gk, rc, rs1, rs2):
    seq, d = x.shape
    tm = _tile(seq, MIX_TM)

    def body(x_ref, gmix_ref, win_ref, gql_ref, gkvl_ref, wuq_ref, wuk_ref, wuv_ref, gq_ref, gk_ref,
             rc_ref, rs1_ref, rs2_ref, qt_ref, k_ref, v_ref, zp_ref):
        xv = x_ref[...]
        z = _mm(xv * _rstd(xv, d) * gmix_ref[...], win_ref[...])
        cq, ckv = z[:, :dm.o_kv], z[:, dm.o_kv:dm.o_pe]
        kpe = z[:, dm.o_pe:dm.o_pool]
        zp_ref[...] = z[:, dm.o_pool:]
        cqn = cq * _rstd(cq, dm.ql) * gql_ref[...]
        ckvn = ckv * _rstd(ckv, dm.kvl) * gkvl_ref[...]
        q = _mm(cqn, wuq_ref[...])
        kn = _mm(ckvn, wuk_ref[...])
        v_ref[...] = _mm(ckvn, wuv_ref[...]).T.reshape(dm.heads, 1, dm.vh, tm).astype(MXU_DTYPE)
        c, s1, s2 = rc_ref[...], rs1_ref[...], rs2_ref[...]
        for h in range(dm.heads):
            sl = slice(h * LANES, (h + 1) * LANES)
            qh = q[:, sl]
            qn = qh * _rstd(qh, dm.head_dim) * gq_ref[...]
            qt_ref[h, 0] = (_rope(qn, c, s1, s2, dm.half) * dm.scale).T.astype(MXU_DTYPE)
            kh = kn[:, sl] + kpe
            kk = kh * _rstd(kh, dm.head_dim) * gk_ref[...]
            k_ref[:, sl] = _rope(kk, c, s1, s2, dm.half).astype(MXU_DTYPE)

    row = lambda i: (i, 0)
    full = lambda a: pl.BlockSpec(a.shape, lambda i: (0,) * a.ndim)
    return pl.pallas_call(
        body, name="mixin_fwd", grid=(seq // tm,),
        in_specs=[pl.BlockSpec((tm, d), row), full(gmix), full(win), full(gql), full(gkvl), full(wuq), full(wuk),
                  full(wuv), full(gq), full(gk),
                  pl.BlockSpec((tm, LANES), row), pl.BlockSpec((tm, LANES), row), pl.BlockSpec((tm, LANES), row)],
        out_specs=[_chunk_spec(dm, LANES, tm), pl.BlockSpec((tm, dm.hw), row), _chunk_spec(dm, dm.vh, tm),
                   pl.BlockSpec((tm, dm.pool_w), row)],
        out_shape=[jax.ShapeDtypeStruct((dm.heads, dm.nch, LANES, dm.ch), MXU_DTYPE),
                   jax.ShapeDtypeStruct((seq, dm.hw), MXU_DTYPE),
                   jax.ShapeDtypeStruct((dm.heads, dm.nch, dm.vh, dm.ch), MXU_DTYPE),
                   jax.ShapeDtypeStruct((seq, dm.pool_w), F32)],
        compiler_params=_params("parallel"),
    )(x, gmix, win, gql, gkvl, wuq, wuk, wuv, gq, gk, rc, rs1, rs2)


def _mla_in_bwd(dm, x, gmix, win, gql, gkvl, wuq, wuk, wuv, gq, gk, rc, rs1, rs2, dq, dk, dv, dmixed):
    seq, d = x.shape
    tm = _tile(seq, MIX_TM)

    def body(x_ref, gmix_ref, win_ref, gql_ref, gkvl_ref, wuq_ref, wuk_ref, wuv_ref, gq_ref, gk_ref,
             rc_ref, rs1_ref, rs2_ref, dq_ref, dk_ref, dv_ref, dmp_ref, dm_ref, dmn_ref,
             dz_ref, dwuq_ref, dwuk_ref, dwuv_ref, dgql_ref, dgkvl_ref, dgq_ref, dgk_ref):
        i = pl.program_id(0)

        @pl.when(i == 0)
        def _():
            for ref in (dwuq_ref, dwuk_ref, dwuv_ref, dgql_ref, dgkvl_ref, dgq_ref, dgk_ref):
                ref[...] = jnp.zeros_like(ref)

        xv = x_ref[...]
        z = _mm(xv * _rstd(xv, d) * gmix_ref[...], win_ref[...])
        cq, ckv = z[:, :dm.o_kv], z[:, dm.o_kv:dm.o_pe]
        kpe = z[:, dm.o_pe:dm.o_pool]
        r_q, r_kv = _rstd(cq, dm.ql), _rstd(ckv, dm.kvl)
        cqh, ckvh = cq * r_q, ckv * r_kv
        cqn = (cqh * gql_ref[...]).astype(MXU_DTYPE)
        ckvn = (ckvh * gkvl_ref[...]).astype(MXU_DTYPE)
        q = _mm(cqn, wuq_ref[...])
        kn = _mm(ckvn, wuk_ref[...])
        c, s1, s2 = rc_ref[...], rs1_ref[...], rs2_ref[...]
        dqv = dq_ref[...].reshape(dm.hw, tm).T
        dkv = dk_ref[...].reshape(dm.hw, tm).T
        dq_pre, dk_pre = [], []
        dkpe = jnp.zeros((tm, LANES), F32)
        dgq = jnp.zeros((1, LANES), F32)
        dgk = jnp.zeros((1, LANES), F32)
        for h in range(dm.heads):
            sl = slice(h * LANES, (h + 1) * LANES)
            qh = q[:, sl]
            rq = _rstd(qh, dm.head_dim)
            xq = qh * rq
            dqn = _rope_t(dqv[:, sl] * dm.scale, c, s1, s2, dm.half)
            dgq += _colsum(dqn * xq)
            dq_pre.append(_rms_bwd(dqn, xq, rq, gq_ref[...], dm.head_dim))
            kh = kn[:, sl] + kpe
            rk = _rstd(kh, dm.head_dim)
            xk = kh * rk
            dkn = _rope_t(dkv[:, sl], c, s1, s2, dm.half)
            dgk += _colsum(dkn * xk)
            dkh = _rms_bwd(dkn, xk, rk, gk_ref[...], dm.head_dim)
            dk_pre.append(dkh)
            dkpe += dkh
        dgq_ref[...] += dgq
        dgk_ref[...] += dgk
        dq_pre = jnp.concatenate(dq_pre, axis=1).astype(MXU_DTYPE)
        dk_pre = jnp.concatenate(dk_pre, axis=1).astype(MXU_DTYPE)
        dvv = dv_ref[...].reshape(dm.vw, tm).T.astype(MXU_DTYPE)
        dwuq_ref[...] += _mm_tn(cqn, dq_pre)
        dwuk_ref[...] += _mm_tn(ckvn, dk_pre)
        dwuv_ref[...] += _mm_tn(ckvn, dvv)
        dcqn = _mm_nt(dq_pre, wuq_ref[...])
        dckvn = _mm_nt(dk_pre, wuk_ref[...]) + _mm_nt(dvv, wuv_ref[...])
        dgql_ref[...] += _colsum(dcqn * cqh)
        dgkvl_ref[...] += _colsum(dckvn * ckvh)
        dcq = _rms_bwd(dcqn, cqh, r_q, gql_ref[...], dm.ql)
        dckv = _rms_bwd(dckvn, ckvh, r_kv, gkvl_ref[...], dm.kvl)
        dzp = _pool_mixed_t(dmp_ref[...], dm_ref[...], dmn_ref[...], i * tm, seq)
        dz_ref[...] = jnp.concatenate([dcq, dckv, dkpe, dzp], axis=1).astype(MXU_DTYPE)

    row = lambda i: (i, 0)
    full = lambda a: pl.BlockSpec(a.shape, lambda i: (0,) * a.ndim)
    acc = lambda shape: pl.BlockSpec(shape, lambda i: (0, 0))
    prev, nxt = _halo_specs(tm, dm.pool_w, seq)
    shapes = [(seq, dm.zw), wuq.shape, wuk.shape, wuv.shape, (1, dm.ql), (1, dm.kvl), (1, LANES), (1, LANES)]
    return pl.pallas_call(
        body, name="mla_in_bwd", grid=(seq // tm,),
        in_specs=[pl.BlockSpec((tm, d), row), full(gmix), full(win), full(gql), full(gkvl), full(wuq), full(wuk),
                  full(wuv), full(gq), full(gk),
                  pl.BlockSpec((tm, LANES), row), pl.BlockSpec((tm, LANES), row), pl.BlockSpec((tm, LANES), row),
                  _chunk_spec(dm, LANES, tm), _chunk_spec(dm, LANES, tm), _chunk_spec(dm, dm.vh, tm),
                  prev, pl.BlockSpec((tm, dm.pool_w), row), nxt],
        out_specs=[pl.BlockSpec((tm, dm.zw), row)] + [acc(s) for s in shapes[1:]],
        out_shape=[jax.ShapeDtypeStruct(shapes[0], MXU_DTYPE)] + [jax.ShapeDtypeStruct(s, F32) for s in shapes[1:]],
        compiler_params=_params("arbitrary"),
    )(x, gmix, win, gql, gkvl, wuq, wuk, wuv, gq, gk, rc, rs1, rs2, dq, dk, dv, dmixed, dmixed, dmixed)


def _rms_proj_bwd(x, gain, w, dz, gin):
    seq, d = x.shape
    n = w.shape[1]
    tm = _tile(seq, MIX_TM)

    def body(x_ref, g_ref, w_ref, dz_ref, gin_ref, gout_ref, dw_ref, dgain_ref):
        @pl.when(pl.program_id(0) == 0)
        def _():
            dw_ref[...] = jnp.zeros_like(dw_ref)
            dgain_ref[...] = jnp.zeros_like(dgain_ref)

        xv = x_ref[...]
        r = _rstd(xv, d)
        xhat = xv * r
        dzv = dz_ref[...]
        dh = _mm_nt(dzv, w_ref[...])
        dw_ref[...] += _mm_tn(xhat * g_ref[...], dzv)
        dgain_ref[...] += _colsum(dh * xhat)
        gout_ref[...] = gin_ref[...] + _rms_bwd(dh, xhat, r, g_ref[...], d)

    row = lambda i: (i, 0)
    return pl.pallas_call(
        body, name="rms_proj_bwd", grid=(seq // tm,),
        in_specs=[pl.BlockSpec((tm, d), row), pl.BlockSpec((1, d), lambda i: (0, 0)),
                  pl.BlockSpec((d, n), lambda i: (0, 0)), pl.BlockSpec((tm, n), row), pl.BlockSpec((tm, d), row)],
        out_specs=[pl.BlockSpec((tm, d), row), pl.BlockSpec((d, n), lambda i: (0, 0)),
                   pl.BlockSpec((1, d), lambda i: (0, 0))],
        out_shape=[jax.ShapeDtypeStruct((seq, d), F32), jax.ShapeDtypeStruct((d, n), F32),
                   jax.ShapeDtypeStruct((1, d), F32)],
        compiler_params=_params("arbitrary"),
    )(x, gain, w, dz, gin)


def _pool_branch(mixed, wpool_ref):
    groups = mixed.shape[1] // LANES
    return jnp.concatenate(
        [_mm(mixed[:, g * LANES:(g + 1) * LANES], wpool_ref[g]) for g in range(groups)], axis=1)


def _mixout_fwd(dm, x, zp, ot, wo, wpool, pscale):
    seq, d = x.shape
    tm = _tile(seq, MIX_TM)

    def body(x_ref, zpp_ref, zp_ref, zpn_ref, ot_ref, wo_ref, wpool_ref, ps_ref, out_ref):
        mixed = _pool_mixed(zpp_ref[...], zp_ref[...], zpn_ref[...], pl.program_id(0) * tm, seq)
        b = _pool_branch(mixed, wpool_ref) * ps_ref[...]
        a = _mm_tn(ot_ref[...].reshape(dm.vw, tm), wo_ref[:dm.vw, :])
        out_ref[...] = x_ref[...] + a + _mm(b, wo_ref[dm.vw:, :])

    row = lambda i: (i, 0)
    full = lambda a: pl.BlockSpec(a.shape, lambda i: (0,) * a.ndim)
    prev, nxt = _halo_specs(tm, dm.pool_w, seq)
    return pl.pallas_call(
        body, name="mixout_fwd", grid=(seq // tm,),
        in_specs=[pl.BlockSpec((tm, d), row), prev, pl.BlockSpec((tm, dm.pool_w), row), nxt,
                  _chunk_spec(dm, dm.vh, tm), full(wo), full(wpool), full(pscale)],
        out_specs=pl.BlockSpec((tm, d), row),
        out_shape=jax.ShapeDtypeStruct((seq, d), F32),
        compiler_params=_params("parallel"),
    )(x, zp, zp, zp, ot, wo, wpool, pscale)


def _mixout_bwd(dm, g, zp, ot, wo, wpool, pscale):
    seq, d = g.shape
    tm = _tile(seq, MIX_TM)
    groups = dm.pool_w // LANES

    def body(g_ref, zpp_ref, zp_ref, zpn_ref, ot_ref, wo_ref, wpool_ref, ps_ref,
             dot_ref, delta_ref, dmixed_ref, dwo_ref, dwpool_ref, dps_ref):
        i = pl.program_id(0)

        @pl.when(i == 0)
        def _():
            for ref in (dwo_ref, dwpool_ref, dps_ref):
                ref[...] = jnp.zeros_like(ref)

        gv = g_ref[...].astype(MXU_DTYPE)
        otv = ot_ref[...].reshape(dm.vw, tm)
        dat = _mm_nt(wo_ref[:dm.vw, :], gv)
        db = _mm_nt(gv, wo_ref[dm.vw:, :])
        mixed = _pool_mixed(zpp_ref[...], zp_ref[...], zpn_ref[...], i * tm, seq).astype(MXU_DTYPE)
        y = _pool_branch(mixed, wpool_ref)
        b = (y * ps_ref[...]).astype(MXU_DTYPE)
        dwo_ref[:dm.vw, :] += _mm(otv, gv)
        dwo_ref[dm.vw:, :] += _mm_tn(b, gv)
        dps_ref[...] += _colsum(db * y)
        dy = (db * ps_ref[...]).astype(MXU_DTYPE)
        dmx = []
        for gi in range(groups):
            sl = slice(gi * LANES, (gi + 1) * LANES)
            dmx.append(_mm_nt(dy[:, sl], wpool_ref[gi]))
            dwpool_ref[gi] += _mm_tn(mixed[:, sl], dy[:, sl])
        dmixed_ref[...] = jnp.concatenate(dmx, axis=1)
        dot_ref[...] = dat.reshape(dm.heads, 1, dm.vh, tm).astype(MXU_DTYPE)
        prod = dat * otv.astype(F32)
        for h in range(dm.heads):
            delta_ref[h, 0] = jnp.broadcast_to(_colsum(prod[h * dm.vh:(h + 1) * dm.vh]), (8, tm))

    row = lambda i: (i, 0)
    full = lambda a: pl.BlockSpec(a.shape, lambda i: (0,) * a.ndim)
    prev, nxt = _halo_specs(tm, dm.pool_w, seq)
    return pl.pallas_call(
        body, name="mixout_bwd", grid=(seq // tm,),
        in_specs=[pl.BlockSpec((tm, d), row), prev, pl.BlockSpec((tm, dm.pool_w), row), nxt,
                  _chunk_spec(dm, dm.vh, tm), full(wo), full(wpool), full(pscale)],
        out_specs=[_chunk_spec(dm, dm.vh, tm), _chunk_spec(dm, 8, tm), pl.BlockSpec((tm, dm.pool_w), row),
                   full(wo), full(wpool), full(pscale)],
        out_shape=[jax.ShapeDtypeStruct((dm.heads, dm.nch, dm.vh, dm.ch), MXU_DTYPE),
                   jax.ShapeDtypeStruct((dm.heads, dm.nch, 8, dm.ch), F32),
                   jax.ShapeDtypeStruct((seq, dm.pool_w), F32), jax.ShapeDtypeStruct(wo.shape, F32),
                   jax.ShapeDtypeStruct(wpool.shape, F32), jax.ShapeDtypeStruct(pscale.shape, F32)],
        compiler_params=_params("arbitrary"),
    )(g, zp, zp, zp, ot, wo, wpool, pscale)


def _attn_fwd(dm, qt, k, vt):
    seq, ch, nch = k.shape[0], dm.ch, dm.nch

    def body(qt_ref, k_ref, vt_ref, ot_ref, lse_ref, m_sc, l_sc, acc_sc, st0, st1, pt0, pt1):
        m_sc[...] = jnp.full_like(m_sc, -jnp.inf)
        l_sc[...] = jnp.zeros_like(l_sc)
        acc_sc[...] = jnp.zeros_like(acc_sc)
        qtv = qt_ref[0, 0]

        def scores(j):
            return _mm(k_ref[pl.ds(pl.multiple_of(j * ch, ch), ch), :], qtv)

        def stage(j, st_cur, st_nxt, pt_cur, pt_prev):
            st_nxt[...] = scores(jnp.minimum(j + 1, nch - 1))
            acc = acc_sc[...] + _mm(vt_ref[0, jnp.maximum(j - 1, 0)], pt_prev[...])
            st = st_cur[...]
            m_prev = m_sc[...]
            m_new = jnp.maximum(m_prev, jnp.max(st, axis=0, keepdims=True))
            alpha = jnp.exp(m_prev - m_new)
            pt = jnp.exp(st - m_new)
            pt_cur[...] = pt.astype(MXU_DTYPE)
            l_sc[...] = alpha * l_sc[...] + _colsum(pt)
            acc_sc[...] = alpha * acc
            m_sc[...] = m_new

        st0[...] = scores(0)
        pt1[...] = jnp.zeros_like(pt1)

        def pair(jj, carry):
            stage(2 * jj, st0, st1, pt0, pt1)
            stage(2 * jj + 1, st1, st0, pt1, pt0)
            return carry

        lax.fori_loop(0, nch // 2, pair, 0)
        acc = acc_sc[...] + _mm(vt_ref[0, nch - 1], pt1[...])
        ot_ref[0, 0] = (acc / l_sc[...]).astype(MXU_DTYPE)
        lse_ref[0, 0] = jnp.broadcast_to(m_sc[...] + jnp.log(l_sc[...]), (8, ch))

    assert nch % 2 == 0
    chunk = lambda rows: pl.BlockSpec((1, 1, rows, ch), lambda h, i: (h, i, 0, 0))
    return pl.pallas_call(
        body, name="attn_fwd", grid=(dm.heads, nch),
        in_specs=[chunk(LANES),
                  pl.BlockSpec((seq, LANES), lambda h, i: (0, h)),
                  pl.BlockSpec((1, nch, dm.vh, ch), lambda h, i: (h, 0, 0, 0))],
        out_specs=[chunk(dm.vh), chunk(8)],
        out_shape=[jax.ShapeDtypeStruct((dm.heads, nch, dm.vh, ch), MXU_DTYPE),
                   jax.ShapeDtypeStruct((dm.heads, nch, 8, ch), F32)],
        scratch_shapes=[pltpu.VMEM((1, ch), F32), pltpu.VMEM((1, ch), F32), pltpu.VMEM((dm.vh, ch), F32),
                        pltpu.VMEM((ch, ch), F32), pltpu.VMEM((ch, ch), F32),
                        pltpu.VMEM((ch, ch), MXU_DTYPE), pltpu.VMEM((ch, ch), MXU_DTYPE)],
        compiler_params=_params("parallel", "parallel"),
    )(qt, k, vt)


def _attn_bwd(dm, qt, k, vt, dot, lse, delta):
    seq, ch, nch = k.shape[0], dm.ch, dm.nch

    def body(k_ref, vt_ref, qt_ref, dot_ref, lse_ref, delta_ref, dqt_ref, dkt_ref, dvt_ref):
        j = pl.program_id(1)
        kv = k_ref[...]
        kt = kv.astype(F32).T.astype(MXU_DTYPE)
        vtv = vt_ref[0, 0]
        dkt_ref[...] = jnp.zeros_like(dkt_ref)
        dvt_ref[...] = jnp.zeros_like(dvt_ref)

        @pl.when(j == 0)
        def _():
            dqt_ref[...] = jnp.zeros_like(dqt_ref)

        def stage(i, carry):
            qti = qt_ref[0, i]
            doti = dot_ref[0, i]
            pt = jnp.exp(_mm(kv, qti) - lse_ref[0, i][:1]).astype(MXU_DTYPE)
            dst = pt * (_mm_tn(vtv, doti) - delta_ref[0, i][:1]).astype(MXU_DTYPE)
            dvt_ref[0, 0] += _mm_nt(doti, pt)
            dkt_ref[0, 0] += _mm_nt(qti, dst)
            dqt_ref[0, i] += _mm(kt, dst)
            return carry

        lax.fori_loop(0, nch, stage, 0)

    whole = lambda rows: pl.BlockSpec((1, nch, rows, ch), lambda h, j: (h, 0, 0, 0))
    chunk = lambda rows: pl.BlockSpec((1, 1, rows, ch), lambda h, j: (h, j, 0, 0))
    return pl.pallas_call(
        body, name="attn_bwd", grid=(dm.heads, nch),
        in_specs=[pl.BlockSpec((ch, LANES), lambda h, j: (j, h)), chunk(dm.vh),
                  whole(LANES), whole(dm.vh), whole(8), whole(8)],
        out_specs=[whole(LANES), chunk(LANES), chunk(dm.vh)],
        out_shape=[jax.ShapeDtypeStruct((dm.heads, nch, LANES, ch), F32),
                   jax.ShapeDtypeStruct((dm.heads, nch, LANES, ch), F32),
                   jax.ShapeDtypeStruct((dm.heads, nch, dm.vh, ch), F32)],
        compiler_params=_params("parallel", "arbitrary"),
    )(k, vt, qt, dot, lse, delta)


def _loss_head(y, target):
    seq, d = y.shape
    tm = _tile(seq, LOSS_TM)

    def body(y_ref, t_ref, part_ref, dy_ref):
        @pl.when(pl.program_id(0) == 0)
        def _():
            part_ref[...] = jnp.zeros_like(part_ref)

        err = y_ref[...] - t_ref[...]
        part_ref[...] += _colsum(err * err)
        dy_ref[...] = err / d

    row = lambda i: (i, 0)
    return pl.pallas_call(
        body, name="loss_head", grid=(seq // tm,),
        in_specs=[pl.BlockSpec((tm, d), row), pl.BlockSpec((tm, d), row)],
        out_specs=[pl.BlockSpec((1, d), lambda i: (0, 0)), pl.BlockSpec((tm, d), row)],
        out_shape=[jax.ShapeDtypeStruct((1, d), F32), jax.ShapeDtypeStruct((seq, d), F32)],
        compiler_params=_params("arbitrary"),
    )(y, target)


def _my_place():
    return lax.axis_index("x"), lax.axis_index("y"), lax.axis_index("c")


def _all_gather(shards):
    n = len(shards)

    def body(*refs):
        x_refs, out_refs = refs[:n], refs[n:2 * n]
        send_sems, recv_sems, local_sems = refs[2 * n:]
        x, y, c = _my_place()
        me, sibling = (x, y, c), (x, y, 1 - c)
        chips = [(1 - x, y), (x, 1 - y), (1 - x, 1 - y)]

        def slot(t, px, py, pc):
            return out_refs[t].at[4 * px + 2 * py + pc]

        def copy(t, k, block, to, src=None):
            return pltpu.make_async_remote_copy(
                src_ref=slot(t, *block) if src is None else src, dst_ref=slot(t, *block),
                send_sem=send_sems.at[t, k], recv_sem=recv_sems.at[t, k], device_id=to, device_id_type=MESH)

        mine = [pltpu.make_async_copy(x_refs[t], slot(t, *me), local_sems.at[t]) for t in range(n)]
        started = []
        for t in range(n):
            mine[t].start()
            first = [copy(t, 0, me, sibling, src=x_refs[t])]
            first += [copy(t, 1 + j, me, (*chip, c), src=x_refs[t]) for j, chip in enumerate(chips)]
            for cp in first:
                cp.start()
            started += first
        for j, chip in enumerate(chips):
            for t in range(n):
                copy(t, 1 + j, (*chip, c), me).wait_recv()
                passed = copy(t, 4 + j, (*chip, c), sibling)
                passed.start()
                started.append(passed)
        for t in range(n):
            copy(t, 0, sibling, me).wait_recv()
            for j, chip in enumerate(chips):
                copy(t, 4 + j, (*chip, 1 - c), me).wait_recv()
        for cp in started:
            cp.wait_send()
        for cp in mine:
            cp.wait()

    return pl.pallas_call(
        body, name="weights_all_gather",
        out_shape=[jax.ShapeDtypeStruct((N_DEV,) + s.shape, s.dtype) for s in shards],
        in_specs=[pl.BlockSpec(memory_space=pl.ANY)] * n,
        out_specs=[pl.BlockSpec(memory_space=pl.ANY)] * n,
        scratch_shapes=[pltpu.SemaphoreType.DMA((n, 7)), pltpu.SemaphoreType.DMA((n, 7)),
                        pltpu.SemaphoreType.DMA((n,))],
    )(*shards)


def _grad_exchange(blocks, common):
    n, nc = len(blocks), len(common)

    def body(*refs):
        g_refs, c_refs = refs[:n], refs[n:n + nc]
        out_refs = refs[n + nc:2 * (n + nc)]
        send_sems, recv_sems, local_sems = refs[2 * (n + nc):]
        x, y, c = _my_place()
        me = 4 * x + 2 * y + c
        srcs = [lambda dev, r=r: r.at[dev] for r in g_refs] + [lambda dev, r=r: r for r in c_refs]
        local = [pltpu.make_async_copy(srcs[t](me), out_refs[t].at[me], local_sems.at[t]) for t in range(n + nc)]
        for cp in local:
            cp.start()
        copies = []
        for k in range(1, N_DEV):
            px = 1 - x if k & 4 else x
            py = 1 - y if k & 2 else y
            pc = 1 - c if k & 1 else c
            for t in range(n + nc):
                copies.append(pltpu.make_async_remote_copy(
                    src_ref=srcs[t](4 * px + 2 * py + pc), dst_ref=out_refs[t].at[me],
                    send_sem=send_sems.at[t, k - 1], recv_sem=recv_sems.at[t, k - 1],
                    device_id=(px, py, pc), device_id_type=MESH))
        for cp in copies:
            cp.start()
        for cp in copies:
            cp.wait_recv()
        for cp in copies:
            cp.wait_send()
        for cp in local:
            cp.wait()

    return pl.pallas_call(
        body, name="grad_exchange",
        out_shape=[jax.ShapeDtypeStruct(b.shape, b.dtype) for b in blocks]
        + [jax.ShapeDtypeStruct((N_DEV,) + a.shape, a.dtype) for a in common],
        in_specs=[pl.BlockSpec(memory_space=pl.ANY)] * (n + nc),
        out_specs=[pl.BlockSpec(memory_space=pl.ANY)] * (n + nc),
        scratch_shapes=[pltpu.SemaphoreType.DMA((n + nc, 7)), pltpu.SemaphoreType.DMA((n + nc, 7)),
                        pltpu.SemaphoreType.DMA((n + nc,))],
    )(*blocks, *common)


def _adamw(parts, w, m, v):
    rows, width = w.shape
    tr = _tile(rows, max(8, ADAM_BLOCK // width // 8 * 8))

    def body(p_ref, w_ref, m_ref, v_ref, g_ref, d_ref, nm_ref, nv_ref):
        g = p_ref[0].astype(F32)
        for s in range(1, N_DEV):
            g = g + p_ref[s].astype(F32)
        nm = ADAM_B1 * m_ref[...] + (1.0 - ADAM_B1) * g
        nv = ADAM_B2 * v_ref[...] + (1.0 - ADAM_B2) * (g * g)
        m_hat = nm / (1.0 - ADAM_B1 ** ADAM_STEP)
        v_hat = nv / (1.0 - ADAM_B2 ** ADAM_STEP)
        g_ref[...] = g
        d_ref[...] = -ADAM_LR * (m_hat / (jnp.sqrt(v_hat) + ADAM_EPS) + ADAM_WD * w_ref[...])
        nm_ref[...] = nm
        nv_ref[...] = nv

    row = pl.BlockSpec((tr, width), lambda i: (i, 0))
    return pl.pallas_call(
        body, name="adamw", grid=(rows // tr,),
        in_specs=[pl.BlockSpec((N_DEV, tr, width), lambda i: (0, i, 0)), row, row, row],
        out_specs=[row] * 4,
        out_shape=[jax.ShapeDtypeStruct(w.shape, F32)] * 4,
        compiler_params=_params("parallel"),
    )(parts, w, m, v)


def _pack_rows(flat_parts, multiple):
    flat = jnp.concatenate([p.reshape(-1) for p in flat_parts])
    chunk = multiple * PACK_W
    pad = (-flat.shape[0]) % chunk
    if pad:
        flat = jnp.concatenate([flat, jnp.zeros((pad,), flat.dtype)])
    return flat.reshape(-1, PACK_W)


def _unpack(packed, shapes):
    flat = packed.reshape(-1)
    out, off = [], 0
    for shape in shapes:
        size = 1
        for s in shape:
            size *= s
        out.append(flat[off:off + size].reshape(shape))
        off += size
    return out


def _to_full(name, g):
    n, l, a, b = g.shape
    if name in ROW_SHARDED:
        return jnp.transpose(g, (1, 0, 2, 3)).reshape(l, n * a, b)
    return jnp.transpose(g, (1, 2, 0, 3)).reshape(l, a, n * b)


def _to_shards(name, full):
    l, a, b = full.shape
    if name in ROW_SHARDED:
        return jnp.transpose(full.reshape(l, N_DEV, a // N_DEV, b), (1, 0, 2, 3))
    return jnp.transpose(full.reshape(l, a, N_DEV, b // N_DEV), (2, 0, 1, 3))


def _pad_heads(w, heads, real):
    lead = w.shape[:-1]
    w = w.reshape(lead + (heads, real))
    w = jnp.concatenate([w, jnp.zeros(lead + (heads, LANES - real), w.dtype)], axis=-1)
    return w.reshape(lead + (heads * LANES,))


def _unpad_heads(w, heads, real):
    lead = w.shape[:-1]
    return w.reshape(lead + (heads, LANES))[..., :real].reshape(lead + (heads * real,))


def _pad_lanes(v, before):
    l, n = v.shape
    return jnp.concatenate([jnp.zeros((l, before), v.dtype), v, jnp.zeros((l, LANES - before - n), v.dtype)], axis=1)


def kernel(x, ffn1_norm, ffn1_w_gu, ffn1_w_down, mix_norm, w_in, q_lat_norm, kv_lat_norm, w_uq, w_uk, w_uv, q_norm, k_norm, w_pool, pool_scale, w_out, ffn2_norm, ffn2_w_gu, ffn2_w_down, loss_target, m_ffn1_norm, m_ffn1_w_gu, m_ffn1_w_down, m_mix_norm, m_w_in, m_q_lat_norm, m_kv_lat_norm, m_w_uq, m_w_uk, m_w_uv, m_q_norm, m_k_norm, m_w_pool, m_pool_scale, m_w_out, m_ffn2_norm, m_ffn2_w_gu, m_ffn2_w_down, v_ffn1_norm, v_ffn1_w_gu, v_ffn1_w_down, v_mix_norm, v_w_in, v_q_lat_norm, v_kv_lat_norm, v_w_uq, v_w_uk, v_w_uv, v_q_norm, v_k_norm, v_w_pool, v_pool_scale, v_w_out, v_ffn2_norm, v_ffn2_w_gu, v_ffn2_w_down):
    given = dict(locals())
    wts = {n: given[n] for n in WEIGHTS}
    mom1 = {n: given["m_" + n] for n in WEIGHTS}
    mom2 = {n: given["v_" + n] for n in WEIGHTS}

    depth, d = ffn1_norm.shape
    seq = x.shape[1]
    dff = ffn1_w_down.shape[1] * N_DEV
    ql, kvl, head_dim = q_lat_norm.shape[1], kv_lat_norm.shape[1], q_norm.shape[1]
    heads = w_uq.shape[2] * N_DEV // head_dim
    nope = w_uk.shape[2] * N_DEV // heads
    vh = w_uv.shape[2] * N_DEV // heads
    groups, gdim = w_pool.shape[1], w_pool.shape[2]
    pool_w = groups * gdim
    assert gdim == LANES and groups == len(POOL_WINDOWS) and head_dim <= LANES and vh <= LANES
    assert d % LANES == 0 and ql % LANES == 0 and kvl % LANES == 0 and seq % HALO == 0
    dm = _Dims(d, ql, kvl, heads, head_dim, nope, vh, pool_w, seq)

    gathered = _all_gather([wts[n].astype(MXU_DTYPE) for n in SHARDED])
    full = {n: _to_full(n, g) for n, g in zip(SHARDED, gathered)}

    zpad = jnp.zeros((depth, d, LANES), MXU_DTYPE)
    win_p = jnp.concatenate(
        [full["w_in"][..., :dm.o_pe], zpad[..., :nope], full["w_in"][..., dm.o_pe:dm.o_pe + dm.rope],
         zpad[..., :LANES - nope - dm.rope], full["w_in"][..., dm.o_pe + dm.rope:]], axis=-1)
    wuq_p = _pad_heads(full["w_uq"], heads, head_dim)
    wuk_p = _pad_heads(full["w_uk"], heads, nope)
    wuv, wo = full["w_uv"], full["w_out"]
    gq_p, gk_p = _pad_lanes(q_norm, 0), _pad_lanes(k_norm, 0)
    wpool_c = w_pool.astype(MXU_DTYPE)
    rc, rs1, rs2 = _rope_tables(seq, nope, dm.rope)
    row = lambda a, l: a[l][None, :]

    h = x[0]
    saved = []
    for l in range(depth):
        x0 = h
        x1 = _ffn_fwd(x0, row(ffn1_norm, l), full["ffn1_w_gu"][l], full["ffn1_w_down"][l])
        q, k, v, zp = _mixin_fwd(dm, x1, row(mix_norm, l), win_p[l], row(q_lat_norm, l), row(kv_lat_norm, l),
                                 wuq_p[l], wuk_p[l], wuv[l], row(gq_p, l), row(gk_p, l), rc, rs1, rs2)
        o, lse = _attn_fwd(dm, q, k, v)
        x2 = _mixout_fwd(dm, x1, zp, o, wo[l], wpool_c[l], row(pool_scale, l))
        h = _ffn_fwd(x2, row(ffn2_norm, l), full["ffn2_w_gu"][l], full["ffn2_w_down"][l])
        saved.append((x0, x1, x2, q, k, v, zp, o, lse))

    part, g = _loss_head(h, loss_target[0])
    loss = lax.psum(0.5 / d * jnp.sum(part), ("x", "y", "c"))

    grads = {n: [None] * depth for n in WEIGHTS}

    def ffn_grads(prefix, l, xin, gout):
        wgu, wd = full[prefix + "_w_gu"][l], full[prefix + "_w_down"][l]
        gain = row(given[prefix + "_norm"], l)
        hh, act, dg, du = _ffn_bwd_act(xin, gain, wgu, wd.T, gout)
        gin, dgain = _ffn_bwd_in(xin, gain, wgu.T, dg, du, gout)
        grads[prefix + "_norm"][l] = dgain[0]
        grads[prefix + "_w_gu"][l] = jnp.concatenate(
            [_wgrad(hh, dg, 1.0, 1024, 1408), _wgrad(hh, du, 1.0, 1024, 1408)], axis=1)
        grads[prefix + "_w_down"][l] = _wgrad(act, gout, 0.5, 1408, 1024)
        return gin

    for l in reversed(range(depth)):
        x0, x1, x2, q, k, v, zp, o, lse = saved[l]
        g = ffn_grads("ffn2", l, x2, g)
        do, delta, dmixed, dwo, dwpool, dps = _mixout_bwd(dm, g, zp, o, wo[l], wpool_c[l], row(pool_scale, l))
        dq, dk, dv = _attn_bwd(dm, q, k, v, do, lse, delta)
        dz, dwuq, dwuk, dwuv, dgql, dgkvl, dgq, dgk = _mla_in_bwd(
            dm, x1, row(mix_norm, l), win_p[l], row(q_lat_norm, l), row(kv_lat_norm, l), wuq_p[l], wuk_p[l],
            wuv[l], row(gq_p, l), row(gk_p, l), rc, rs1, rs2, dq, dk, dv, dmixed)
        g, dwin, dgmix = _rms_proj_bwd(x1, row(mix_norm, l), win_p[l], dz, g)
        grads["w_out"][l] = dwo
        grads["w_pool"][l] = dwpool
        grads["pool_scale"][l] = dps[0]
        grads["w_uq"][l] = _unpad_heads(dwuq, heads, head_dim)
        grads["w_uk"][l] = _unpad_heads(dwuk, heads, nope)
        grads["w_uv"][l] = dwuv
        grads["q_lat_norm"][l] = dgql[0]
        grads["kv_lat_norm"][l] = dgkvl[0]
        grads["q_norm"][l] = dgq[0, :head_dim]
        grads["k_norm"][l] = dgk[0, :head_dim]
        grads["w_in"][l] = jnp.concatenate(
            [dwin[:, :dm.o_pe], dwin[:, dm.o_pe + nope:dm.o_pe + nope + dm.rope], dwin[:, dm.o_pool:]], axis=1)
        grads["mix_norm"][l] = dgmix[0]
        g = ffn_grads("ffn1", l, x0, g)

    grads = {n: jnp.stack(grads[n]) for n in WEIGHTS}

    repl_shapes = [wts[n].shape for n in REPLICATED]
    received = _grad_exchange([_to_shards(n, grads[n]).astype(WIRE_DTYPE) for n in SHARDED],
                              [_pack_rows([grads[n] for n in REPLICATED], 8)])
    outs = [{}, {}, {}, {}]
    for n, parts in zip(SHARDED, received):
        l, a, b = wts[n].shape
        results = _adamw(parts.reshape(N_DEV, l * a, b),
                         *[src[n].reshape(l * a, b) for src in (wts, mom1, mom2)])
        for o, r in zip(outs, results):
            o[n] = r.reshape(l, a, b)
    results = _adamw(received[-1], *[_pack_rows([src[n] for n in REPLICATED], 8) for src in (wts, mom1, mom2)])
    for o, r in zip(outs, results):
        o.update(zip(REPLICATED, _unpack(r, repl_shapes)))

    return (loss, g[None], *[o[n] for o in outs for n in WEIGHTS])
```

```python
import functools

import jax
import jax.numpy as jnp
from jax import lax
from jax.experimental import pallas as pl
from jax.experimental.pallas import tpu as pltpu

F32 = jnp.float32
MXU_DTYPE = jnp.bfloat16
WIRE_DTYPE = jnp.bfloat16
EPS = 1e-6
ROPE_THETA = 10000.0
POOL_WINDOWS = (2, 4, 8, 16)
ADAM_LR, ADAM_B1, ADAM_B2, ADAM_EPS, ADAM_WD, ADAM_STEP = 0.001, 0.9, 0.999, 1e-08, 0.01, 10

LANES = 128
HALO = 64
PACK_W = 1024
N_DEV = 8
VMEM_LIMIT = 56 * 1024 * 1024
MESH = pl.DeviceIdType.MESH

LOSS_TM = 1024
FFN_FWD_TM, FFN_FWD_TN = 512, 1408
FFN_BWD_TM, FFN_BWD_TN = 256, 1408
FFN_IN_TM = 256
MIX_TM = 256
ATTN_CH = 1024
WG_BK = 2048
ADAM_BLOCK = 128 * 1024

SHARDED = ("ffn1_w_gu", "ffn1_w_down", "w_in", "w_uq", "w_uk", "w_uv", "w_out", "ffn2_w_gu", "ffn2_w_down")
ROW_SHARDED = ("ffn1_w_down", "w_out", "ffn2_w_down")
REPLICATED = ("ffn1_norm", "mix_norm", "q_lat_norm", "kv_lat_norm", "q_norm", "k_norm", "w_pool", "pool_scale",
              "ffn2_norm")
WEIGHTS = ("ffn1_norm", "ffn1_w_gu", "ffn1_w_down", "mix_norm", "w_in", "q_lat_norm", "kv_lat_norm", "w_uq", "w_uk",
           "w_uv", "q_norm", "k_norm", "w_pool", "pool_scale", "w_out", "ffn2_norm", "ffn2_w_gu", "ffn2_w_down")


def _tile(n, pref):
    if n <= pref:
        return n
    t = pref - pref % 8
    while n % t:
        t -= 8
    return t


def _params(*sem):
    return pltpu.CompilerParams(dimension_semantics=sem, vmem_limit_bytes=VMEM_LIMIT)


def _mm(a, b):
    return jnp.dot(a.astype(MXU_DTYPE), b.astype(MXU_DTYPE), preferred_element_type=F32)


def _mm_nt(a, b):
    return lax.dot_general(a.astype(MXU_DTYPE), b.astype(MXU_DTYPE), (((1,), (1,)), ((), ())),
                           preferred_element_type=F32)


def _mm_tn(a, b):
    return lax.dot_general(a.astype(MXU_DTYPE), b.astype(MXU_DTYPE), (((0,), (0,)), ((), ())),
                           preferred_element_type=F32)


def _rstd(x, n):
    return lax.rsqrt(jnp.sum(x * x, axis=-1, keepdims=True) / n + EPS)


def _rms_bwd(dy, xhat, r, gain, n):
    dxh = dy * gain
    return r * (dxh - xhat * (jnp.sum(dxh * xhat, axis=-1, keepdims=True) / n))


def _colsum(x):
    return jnp.sum(x, axis=0, keepdims=True)


def _split3(x):
    hi = x.astype(MXU_DTYPE)
    r1 = x - hi.astype(F32)
    mid = r1.astype(MXU_DTYPE)
    lo = (r1 - mid.astype(F32)).astype(MXU_DTYPE)
    return jnp.concatenate([hi, mid, lo], axis=1)


def _sum3(x):
    n = x.shape[1] // 3
    return (x[:, :n] + x[:, n:2 * n]) + x[:, 2 * n:]


def _pool_mixed(prev, main, nxt, row0, seq):
    tm = main.shape[0]
    k = tm + 2 * HALO
    ext = jnp.concatenate([prev, main, nxt], axis=0)
    s_i = lax.broadcasted_iota(jnp.int32, (tm, k), 0) + row0
    t_j = lax.broadcasted_iota(jnp.int32, (tm, k), 1) + (row0 - HALO)
    s_v = lax.broadcasted_iota(jnp.int32, (tm, 1), 0) + row0
    inside = (t_j >= 0) & (t_j < seq)
    outs = []
    for g, w in enumerate(POOL_WINDOWS):
        left, right = w // 2, w - 1 - w // 2
        band = jnp.where((t_j >= s_i - left) & (t_j <= s_i + right) & inside, 1.0, 0.0).astype(MXU_DTYPE)
        sl = slice(g * LANES, (g + 1) * LANES)
        wsum = _sum3(jnp.dot(band, _split3(ext[:, sl]), preferred_element_type=F32))
        cnt = (jnp.minimum(s_v + right + 1, seq) - jnp.maximum(s_v - left, 0)).astype(F32)
        outs.append(wsum / cnt - main[:, sl])
    return jnp.concatenate(outs, axis=1)


def _pool_mixed_t(prev, main, nxt, row0, seq):
    tm = main.shape[0]
    k = tm + 2 * HALO
    ext = jnp.concatenate([prev, main, nxt], axis=0)
    t_i = lax.broadcasted_iota(jnp.int32, (tm, k), 0) + row0
    s_j = lax.broadcasted_iota(jnp.int32, (tm, k), 1) + (row0 - HALO)
    s_v = lax.broadcasted_iota(jnp.int32, (k, 1), 0) + (row0 - HALO)
    inside = (s_j >= 0) & (s_j < seq)
    inside_v = (s_v >= 0) & (s_v < seq)
    outs = []
    for g, w in enumerate(POOL_WINDOWS):
        left, right = w // 2, w - 1 - w // 2
        band = jnp.where((s_j >= t_i - right) & (s_j <= t_i + left) & inside, 1.0, 0.0).astype(MXU_DTYPE)
        sl = slice(g * LANES, (g + 1) * LANES)
        cnt = (jnp.minimum(s_v + right + 1, seq) - jnp.maximum(s_v - left, 0)).astype(F32)
        scaled = jnp.where(inside_v, ext[:, sl] / jnp.maximum(cnt, 1.0), 0.0)
        outs.append(_sum3(jnp.dot(band, _split3(scaled), preferred_element_type=F32)) - main[:, sl])
    return jnp.concatenate(outs, axis=1)


def _halo_specs(tm, width, seq):
    per = tm // HALO
    last = seq // HALO - 1
    prev = pl.BlockSpec((HALO, width), lambda i: (jnp.maximum(i * per - 1, 0), 0))
    nxt = pl.BlockSpec((HALO, width), lambda i: (jnp.minimum((i + 1) * per, last), 0))
    return prev, nxt


def _rope(x, c, s1, s2, half):
    return x * c + pltpu.roll(x, half, 1) * s1 + pltpu.roll(x, LANES - half, 1) * s2


def _rope_t(d, c, s1, s2, half):
    return d * c + pltpu.roll(d * s1, LANES - half, 1) + pltpu.roll(d * s2, half, 1)


def _rope_tables(seq, nope, rope):
    half = rope // 2
    pos = jnp.arange(seq, dtype=F32)
    inv = ROPE_THETA ** (-jnp.arange(0, rope, 2, dtype=F32) / rope)
    ang = pos[:, None] * inv[None, :]
    cos, sin = jnp.cos(ang), jnp.sin(ang)
    zeros = lambda n: jnp.zeros((seq, n), F32)
    ones = lambda n: jnp.ones((seq, n), F32)
    tail = LANES - nope - rope
    c = jnp.concatenate([ones(nope), cos, cos, ones(tail)], axis=1)
    s1 = jnp.concatenate([zeros(nope + half), sin, zeros(tail)], axis=1)
    s2 = jnp.concatenate([zeros(nope), -sin, zeros(half + tail)], axis=1)
    return c, s1, s2


def _ffn_fwd(x, gain, wgu, wd):
    seq, d = x.shape
    f = wd.shape[0]
    tm, tn = _tile(seq, FFN_FWD_TM), _tile(f, FFN_FWD_TN)
    nk = f // tn

    def body(x_ref, g_ref, wg_ref, wu_ref, wd_ref, o_ref, h_sc, acc_sc):
        k = pl.program_id(1)

        @pl.when(k == 0)
        def _():
            xv = x_ref[...]
            h_sc[...] = (xv * _rstd(xv, d) * g_ref[...]).astype(MXU_DTYPE)
            acc_sc[...] = jnp.zeros_like(acc_sc)

        h = h_sc[...]
        g = jnp.dot(h, wg_ref[...], preferred_element_type=F32)
        u = jnp.dot(h, wu_ref[...], preferred_element_type=F32)
        a = g * (1.0 / (1.0 + jnp.exp(-g))) * u
        acc_sc[...] += _mm(a, wd_ref[...])

        @pl.when(k == nk - 1)
        def _():
            o_ref[...] = x_ref[...] + 0.5 * acc_sc[...]

    return pl.pallas_call(
        body, name="ffn_fwd", grid=(seq // tm, nk),
        in_specs=[pl.BlockSpec((tm, d), lambda i, k: (i, 0)),
                  pl.BlockSpec((1, d), lambda i, k: (0, 0)),
                  pl.BlockSpec((d, tn), lambda i, k: (0, k)),
                  pl.BlockSpec((d, tn), lambda i, k: (0, k + nk)),
                  pl.BlockSpec((tn, d), lambda i, k: (k, 0))],
        out_specs=pl.BlockSpec((tm, d), lambda i, k: (i, 0)),
        out_shape=jax.ShapeDtypeStruct((seq, d), F32),
        scratch_shapes=[pltpu.VMEM((tm, d), MXU_DTYPE), pltpu.VMEM((tm, d), F32)],
        compiler_params=_params("parallel", "arbitrary"),
    )(x, gain, wgu, wgu, wd)


def _ffn_bwd_act(x, gain, wgu, wdt, dout):
    seq, d = x.shape
    f = wdt.shape[1]
    tm, tn = _tile(seq, FFN_BWD_TM), _tile(f, FFN_BWD_TN)
    nk = f // tn

    def body(x_ref, g_ref, wg_ref, wu_ref, wdt_ref, do_ref, h_ref, act_ref, dg_ref, du_ref, dy_sc):
        @pl.when(pl.program_id(1) == 0)
        def _():
            xv = x_ref[...]
            h_ref[...] = (xv * _rstd(xv, d) * g_ref[...]).astype(MXU_DTYPE)
            dy_sc[...] = (0.5 * do_ref[...]).astype(MXU_DTYPE)

        h = h_ref[...]
        g = jnp.dot(h, wg_ref[...], preferred_element_type=F32)
        u = jnp.dot(h, wu_ref[...], preferred_element_type=F32)
        sig = 1.0 / (1.0 + jnp.exp(-g))
        silu = g * sig
        act_ref[...] = (silu * u).astype(MXU_DTYPE)
        da = jnp.dot(dy_sc[...], wdt_ref[...], preferred_element_type=F32)
        du_ref[...] = (da * silu).astype(MXU_DTYPE)
        dg_ref[...] = (da * u * (sig * (1.0 + g * (1.0 - sig)))).astype(MXU_DTYPE)

    row = lambda i, k: (i, 0)
    col = lambda i, k: (i, k)
    return pl.pallas_call(
        body, name="ffn_bwd_act", grid=(seq // tm, nk),
        in_specs=[pl.BlockSpec((tm, d), row),
                  pl.BlockSpec((1, d), lambda i, k: (0, 0)),
                  pl.BlockSpec((d, tn), lambda i, k: (0, k)),
                  pl.BlockSpec((d, tn), lambda i, k: (0, k + nk)),
                  pl.BlockSpec((d, tn), lambda i, k: (0, k)),
                  pl.BlockSpec((tm, d), row)],
        out_specs=[pl.BlockSpec((tm, d), row), pl.BlockSpec((tm, tn), col), pl.BlockSpec((tm, tn), col),
                   pl.BlockSpec((tm, tn), col)],
        out_shape=[jax.ShapeDtypeStruct((seq, d), MXU_DTYPE), jax.ShapeDtypeStruct((seq, f), MXU_DTYPE),
                   jax.ShapeDtypeStruct((seq, f), MXU_DTYPE), jax.ShapeDtypeStruct((seq, f), MXU_DTYPE)],
        scratch_shapes=[pltpu.VMEM((tm, d), MXU_DTYPE)],
        compiler_params=_params("parallel", "arbitrary"),
    )(x, gain, wgu, wgu, wdt, dout)


def _ffn_bwd_in(x, gain, wgut, dg, du, dout):
    seq, d = x.shape
    f = dg.shape[1]
    tm = _tile(seq, FFN_IN_TM)

    def body(x_ref, g_ref, wgut_ref, dg_ref, du_ref, do_ref, dx_ref, dgain_ref):
        @pl.when(pl.program_id(0) == 0)
        def _():
            dgain_ref[...] = jnp.zeros_like(dgain_ref)

        dh = (jnp.dot(dg_ref[...], wgut_ref[:f, :], preferred_element_type=F32)
              + jnp.dot(du_ref[...], wgut_ref[f:, :], preferred_element_type=F32))
        xv = x_ref[...]
        r = _rstd(xv, d)
        xhat = xv * r
        dgain_ref[...] += _colsum(dh * xhat)
        dx_ref[...] = do_ref[...] + _rms_bwd(dh, xhat, r, g_ref[...], d)

    row = lambda i: (i, 0)
    return pl.pallas_call(
        body, name="ffn_bwd_in", grid=(seq // tm,),
        in_specs=[pl.BlockSpec((tm, d), row), pl.BlockSpec((1, d), lambda i: (0, 0)),
                  pl.BlockSpec((2 * f, d), lambda i: (0, 0)), pl.BlockSpec((tm, f), row), pl.BlockSpec((tm, f), row),
                  pl.BlockSpec((tm, d), row)],
        out_specs=[pl.BlockSpec((tm, d), row), pl.BlockSpec((1, d), lambda i: (0, 0))],
        out_shape=[jax.ShapeDtypeStruct((seq, d), F32), jax.ShapeDtypeStruct((1, d), F32)],
        compiler_params=_params("arbitrary"),
    )(x, gain, wgut, dg, du, dout)


def _wgrad(a, b, scale, bm, bn):
    seq, m = a.shape
    n = b.shape[1]
    bm, bn, bk = _tile(m, bm), _tile(n, bn), _tile(seq, WG_BK)
    ns = seq // bk

    def body(a_ref, b_ref, o_ref):
        @pl.when(pl.program_id(2) == 0)
        def _():
            o_ref[...] = jnp.zeros_like(o_ref)

        o_ref[...] += scale * _mm_tn(a_ref[...], b_ref[...])

    return pl.pallas_call(
        body, name="wgrad", grid=(m // bm, n // bn, ns),
        in_specs=[pl.BlockSpec((bk, bm), lambda i, j, s: (s, i)),
                  pl.BlockSpec((bk, bn), lambda i, j, s: (s, j))],
        out_specs=pl.BlockSpec((bm, bn), lambda i, j, s: (i, j)),
        out_shape=jax.ShapeDtypeStruct((m, n), F32),
        compiler_params=_params("parallel", "parallel", "arbitrary"),
    )(a, b)


class _Dims:
    def __init__(self, d, ql, kvl, heads, head_dim, nope, vh, pool_w, seq):
        self.d, self.ql, self.kvl, self.heads, self.head_dim, self.nope, self.vh, self.pool_w = (
            d, ql, kvl, heads, head_dim, nope, vh, pool_w)
        self.rope = head_dim - nope
        self.half = self.rope // 2
        self.hw = heads * LANES
        self.vw = heads * vh
        self.ch = _tile(seq, ATTN_CH)
        self.nch = seq // self.ch
        self.o_kv = ql
        self.o_pe = ql + kvl
        self.o_pool = ql + kvl + LANES
        self.zw = self.o_pool + pool_w
        self.scale = head_dim ** -0.5


def _chunk_spec(dm, rows, tm):
    per = dm.ch // tm
    return pl.BlockSpec((dm.heads, 1, rows, tm), lambda i: (0, i // per, 0, i % per))


def _mixin_fwd(dm, x, gmix, win, gql, gkvl, wuq, wuk, wuv, gq, gk, rc, rs1, rs2):
    seq, d = x.shape
    tm = _tile(seq, MIX_TM)

    def body(x_ref, gmix_ref, win_ref, gql_ref, gkvl_ref, wuq_ref, wuk_ref, wuv_ref, gq_ref, gk_ref,
             rc_ref, rs1_ref, rs2_ref, q_ref, k_ref, v_ref, zp_ref):
        xv = x_ref[...]
        z = _mm(xv * _rstd(xv, d) * gmix_ref[...], win_ref[...])
        cq, ckv = z[:, :dm.o_kv], z[:, dm.o_kv:dm.o_pe]
        kpe = z[:, dm.o_pe:dm.o_pool]
        zp_ref[...] = z[:, dm.o_pool:]
        cqn = cq * _rstd(cq, dm.ql) * gql_ref[...]
        ckvn = ckv * _rstd(ckv, dm.kvl) * gkvl_ref[...]
        q = _mm(cqn, wuq_ref[...])
        kn = _mm(ckvn, wuk_ref[...])
        v_ref[...] = _mm(ckvn, wuv_ref[...]).T.reshape(dm.heads, 1, dm.vh, tm).astype(MXU_DTYPE)
        c, s1, s2 = rc_ref[...], rs1_ref[...], rs2_ref[...]
        for h in range(dm.heads):
            sl = slice(h * LANES, (h + 1) * LANES)
            qh = q[:, sl]
            qn = qh * _rstd(qh, dm.head_dim) * gq_ref[...]
            q_ref[:, sl] = (_rope(qn, c, s1, s2, dm.half) * dm.scale).astype(MXU_DTYPE)
            kh = kn[:, sl] + kpe
            kk = kh * _rstd(kh, dm.head_dim) * gk_ref[...]
            k_ref[:, sl] = _rope(kk, c, s1, s2, dm.half).astype(MXU_DTYPE)

    row = lambda i: (i, 0)
    full = lambda a: pl.BlockSpec(a.shape, lambda i: (0,) * a.ndim)
    return pl.pallas_call(
        body, name="mixin_fwd", grid=(seq // tm,),
        in_specs=[pl.BlockSpec((tm, d), row), full(gmix), full(win), full(gql), full(gkvl), full(wuq), full(wuk),
                  full(wuv), full(gq), full(gk),
                  pl.BlockSpec((tm, LANES), row), pl.BlockSpec((tm, LANES), row), pl.BlockSpec((tm, LANES), row)],
        out_specs=[pl.BlockSpec((tm, dm.hw), row), pl.BlockSpec((tm, dm.hw), row), _chunk_spec(dm, dm.vh, tm),
                   pl.BlockSpec((tm, dm.pool_w), row)],
        out_shape=[jax.ShapeDtypeStruct((seq, dm.hw), MXU_DTYPE),
                   jax.ShapeDtypeStruct((seq, dm.hw), MXU_DTYPE),
                   jax.ShapeDtypeStruct((dm.heads, dm.nch, dm.vh, dm.ch), MXU_DTYPE),
                   jax.ShapeDtypeStruct((seq, dm.pool_w), F32)],
        compiler_params=_params("parallel"),
    )(x, gmix, win, gql, gkvl, wuq, wuk, wuv, gq, gk, rc, rs1, rs2)


def _mla_in_bwd(dm, x, gmix, win, gql, gkvl, wuq, wuk, wuv, gq, gk, rc, rs1, rs2, dq, dk, dv, dmixed):
    seq, d = x.shape
    tm = _tile(seq, MIX_TM)

    def body(x_ref, gmix_ref, win_ref, gql_ref, gkvl_ref, wuq_ref, wuk_ref, wuv_ref, gq_ref, gk_ref,
             rc_ref, rs1_ref, rs2_ref, dq_ref, dk_ref, dv_ref, dmp_ref, dm_ref, dmn_ref,
             dz_ref, dwuq_ref, dwuk_ref, dwuv_ref, dgql_ref, dgkvl_ref, dgq_ref, dgk_ref):
        i = pl.program_id(0)

        @pl.when(i == 0)
        def _():
            for ref in (dwuq_ref, dwuk_ref, dwuv_ref, dgql_ref, dgkvl_ref, dgq_ref, dgk_ref):
                ref[...] = jnp.zeros_like(ref)

        xv = x_ref[...]
        z = _mm(xv * _rstd(xv, d) * gmix_ref[...], win_ref[...])
        cq, ckv = z[:, :dm.o_kv], z[:, dm.o_kv:dm.o_pe]
        kpe = z[:, dm.o_pe:dm.o_pool]
        r_q, r_kv = _rstd(cq, dm.ql), _rstd(ckv, dm.kvl)
        cqh, ckvh = cq * r_q, ckv * r_kv
        cqn = (cqh * gql_ref[...]).astype(MXU_DTYPE)
        ckvn = (ckvh * gkvl_ref[...]).astype(MXU_DTYPE)
        q = _mm(cqn, wuq_ref[...])
        kn = _mm(ckvn, wuk_ref[...])
        c, s1, s2 = rc_ref[...], rs1_ref[...], rs2_ref[...]
        dqv = dq_ref[...].reshape(dm.hw, tm).T
        dq_pre, dk_pre = [], []
        dkpe = jnp.zeros((tm, LANES), F32)
        dgq = jnp.zeros((1, LANES), F32)
        dgk = jnp.zeros((1, LANES), F32)
        for h in range(dm.heads):
            sl = slice(h * LANES, (h + 1) * LANES)
            qh = q[:, sl]
            rq = _rstd(qh, dm.head_dim)
            xq = qh * rq
            dqn = _rope_t(dqv[:, sl] * dm.scale, c, s1, s2, dm.half)
            dgq += _colsum(dqn * xq)
            dq_pre.append(_rms_bwd(dqn, xq, rq, gq_ref[...], dm.head_dim))
            kh = kn[:, sl] + kpe
            rk = _rstd(kh, dm.head_dim)
            xk = kh * rk
            dkn = _rope_t(dk_ref[:, sl], c, s1, s2, dm.half)
            dgk += _colsum(dkn * xk)
            dkh = _rms_bwd(dkn, xk, rk, gk_ref[...], dm.head_dim)
            dk_pre.append(dkh)
            dkpe += dkh
        dgq_ref[...] += dgq
        dgk_ref[...] += dgk
        dq_pre = jnp.concatenate(dq_pre, axis=1).astype(MXU_DTYPE)
        dk_pre = jnp.concatenate(dk_pre, axis=1).astype(MXU_DTYPE)
        dvv = dv_ref[...].reshape(dm.vw, tm).T.astype(MXU_DTYPE)
        dwuq_ref[...] += _mm_tn(cqn, dq_pre)
        dwuk_ref[...] += _mm_tn(ckvn, dk_pre)
        dwuv_ref[...] += _mm_tn(ckvn, dvv)
        dcqn = _mm_nt(dq_pre, wuq_ref[...])
        dckvn = _mm_nt(dk_pre, wuk_ref[...]) + _mm_nt(dvv, wuv_ref[...])
        dgql_ref[...] += _colsum(dcqn * cqh)
        dgkvl_ref[...] += _colsum(dckvn * ckvh)
        dcq = _rms_bwd(dcqn, cqh, r_q, gql_ref[...], dm.ql)
        dckv = _rms_bwd(dckvn, ckvh, r_kv, gkvl_ref[...], dm.kvl)
        dzp = _pool_mixed_t(dmp_ref[...], dm_ref[...], dmn_ref[...], i * tm, seq)
        dz_ref[...] = jnp.concatenate([dcq, dckv, dkpe, dzp], axis=1).astype(MXU_DTYPE)

    row = lambda i: (i, 0)
    full = lambda a: pl.BlockSpec(a.shape, lambda i: (0,) * a.ndim)
    acc = lambda shape: pl.BlockSpec(shape, lambda i: (0, 0))
    prev, nxt = _halo_specs(tm, dm.pool_w, seq)
    shapes = [(seq, dm.zw), wuq.shape, wuk.shape, wuv.shape, (1, dm.ql), (1, dm.kvl), (1, LANES), (1, LANES)]
    return pl.pallas_call(
        body, name="mla_in_bwd", grid=(seq // tm,),
        in_specs=[pl.BlockSpec((tm, d), row), full(gmix), full(win), full(gql), full(gkvl), full(wuq), full(wuk),
                  full(wuv), full(gq), full(gk),
                  pl.BlockSpec((tm, LANES), row), pl.BlockSpec((tm, LANES), row), pl.BlockSpec((tm, LANES), row),
                  _chunk_spec(dm, LANES, tm), pl.BlockSpec((tm, dm.hw), row), _chunk_spec(dm, dm.vh, tm),
                  prev, pl.BlockSpec((tm, dm.pool_w), row), nxt],
        out_specs=[pl.BlockSpec((tm, dm.zw), row)] + [acc(s) for s in shapes[1:]],
        out_shape=[jax.ShapeDtypeStruct(shapes[0], MXU_DTYPE)] + [jax.ShapeDtypeStruct(s, F32) for s in shapes[1:]],
        compiler_params=_params("arbitrary"),
    )(x, gmix, win, gql, gkvl, wuq, wuk, wuv, gq, gk, rc, rs1, rs2, dq, dk, dv, dmixed, dmixed, dmixed)


def _rms_proj_bwd(x, gain, w, dz, gin):
    seq, d = x.shape
    n = w.shape[1]
    tm = _tile(seq, MIX_TM)

    def body(x_ref, g_ref, w_ref, dz_ref, gin_ref, gout_ref, dw_ref, dgain_ref):
        @pl.when(pl.program_id(0) == 0)
        def _():
            dw_ref[...] = jnp.zeros_like(dw_ref)
            dgain_ref[...] = jnp.zeros_like(dgain_ref)

        xv = x_ref[...]
        r = _rstd(xv, d)
        xhat = xv * r
        dzv = dz_ref[...]
        dh = _mm_nt(dzv, w_ref[...])
        dw_ref[...] += _mm_tn(xhat * g_ref[...], dzv)
        dgain_ref[...] += _colsum(dh * xhat)
        gout_ref[...] = gin_ref[...] + _rms_bwd(dh, xhat, r, g_ref[...], d)

    row = lambda i: (i, 0)
    return pl.pallas_call(
        body, name="rms_proj_bwd", grid=(seq // tm,),
        in_specs=[pl.BlockSpec((tm, d), row), pl.BlockSpec((1, d), lambda i: (0, 0)),
                  pl.BlockSpec((d, n), lambda i: (0, 0)), pl.BlockSpec((tm, n), row), pl.BlockSpec((tm, d), row)],
        out_specs=[pl.BlockSpec((tm, d), row), pl.BlockSpec((d, n), lambda i: (0, 0)),
                   pl.BlockSpec((1, d), lambda i: (0, 0))],
        out_shape=[jax.ShapeDtypeStruct((seq, d), F32), jax.ShapeDtypeStruct((d, n), F32),
                   jax.ShapeDtypeStruct((1, d), F32)],
        compiler_params=_params("arbitrary"),
    )(x, gain, w, dz, gin)


def _pool_branch(mixed, wpool_ref):
    groups = mixed.shape[1] // LANES
    return jnp.concatenate(
        [_mm(mixed[:, g * LANES:(g + 1) * LANES], wpool_ref[g]) for g in range(groups)], axis=1)


def _mixout_fwd(dm, x, zp, ot, wo, wpool, pscale):
    seq, d = x.shape
    tm = _tile(seq, MIX_TM)

    def body(x_ref, zpp_ref, zp_ref, zpn_ref, ot_ref, wo_ref, wpool_ref, ps_ref, out_ref):
        mixed = _pool_mixed(zpp_ref[...], zp_ref[...], zpn_ref[...], pl.program_id(0) * tm, seq)
        b = _pool_branch(mixed, wpool_ref) * ps_ref[...]
        a = _mm_tn(ot_ref[...].reshape(dm.vw, tm), wo_ref[:dm.vw, :])
        out_ref[...] = x_ref[...] + a + _mm(b, wo_ref[dm.vw:, :])

    row = lambda i: (i, 0)
    full = lambda a: pl.BlockSpec(a.shape, lambda i: (0,) * a.ndim)
    prev, nxt = _halo_specs(tm, dm.pool_w, seq)
    return pl.pallas_call(
        body, name="mixout_fwd", grid=(seq // tm,),
        in_specs=[pl.BlockSpec((tm, d), row), prev, pl.BlockSpec((tm, dm.pool_w), row), nxt,
                  _chunk_spec(dm, dm.vh, tm), full(wo), full(wpool), full(pscale)],
        out_specs=pl.BlockSpec((tm, d), row),
        out_shape=jax.ShapeDtypeStruct((seq, d), F32),
        compiler_params=_params("parallel"),
    )(x, zp, zp, zp, ot, wo, wpool, pscale)


def _mixout_bwd(dm, g, zp, ot, wo, wpool, pscale):
    seq, d = g.shape
    tm = _tile(seq, MIX_TM)
    groups = dm.pool_w // LANES

    def body(g_ref, zpp_ref, zp_ref, zpn_ref, ot_ref, wo_ref, wpool_ref, ps_ref,
             dot_ref, delta_ref, dmixed_ref, dwo_ref, dwpool_ref, dps_ref):
        i = pl.program_id(0)

        @pl.when(i == 0)
        def _():
            for ref in (dwo_ref, dwpool_ref, dps_ref):
                ref[...] = jnp.zeros_like(ref)

        gv = g_ref[...].astype(MXU_DTYPE)
        otv = ot_ref[...].reshape(dm.vw, tm)
        dat = _mm_nt(wo_ref[:dm.vw, :], gv)
        db = _mm_nt(gv, wo_ref[dm.vw:, :])
        mixed = _pool_mixed(zpp_ref[...], zp_ref[...], zpn_ref[...], i * tm, seq).astype(MXU_DTYPE)
        y = _pool_branch(mixed, wpool_ref)
        b = (y * ps_ref[...]).astype(MXU_DTYPE)
        dwo_ref[:dm.vw, :] += _mm(otv, gv)
        dwo_ref[dm.vw:, :] += _mm_tn(b, gv)
        dps_ref[...] += _colsum(db * y)
        dy = (db * ps_ref[...]).astype(MXU_DTYPE)
        dmx = []
        for gi in range(groups):
            sl = slice(gi * LANES, (gi + 1) * LANES)
            dmx.append(_mm_nt(dy[:, sl], wpool_ref[gi]))
            dwpool_ref[gi] += _mm_tn(mixed[:, sl], dy[:, sl])
        dmixed_ref[...] = jnp.concatenate(dmx, axis=1)
        dot_ref[...] = dat.reshape(dm.heads, 1, dm.vh, tm).astype(MXU_DTYPE)
        prod = dat * otv.astype(F32)
        for h in range(dm.heads):
            delta_ref[h, 0] = jnp.broadcast_to(_colsum(prod[h * dm.vh:(h + 1) * dm.vh]), (8, tm))

    row = lambda i: (i, 0)
    full = lambda a: pl.BlockSpec(a.shape, lambda i: (0,) * a.ndim)
    prev, nxt = _halo_specs(tm, dm.pool_w, seq)
    return pl.pallas_call(
        body, name="mixout_bwd", grid=(seq // tm,),
        in_specs=[pl.BlockSpec((tm, d), row), prev, pl.BlockSpec((tm, dm.pool_w), row), nxt,
                  _chunk_spec(dm, dm.vh, tm), full(wo), full(wpool), full(pscale)],
        out_specs=[_chunk_spec(dm, dm.vh, tm), _chunk_spec(dm, 8, tm), pl.BlockSpec((tm, dm.pool_w), row),
                   full(wo), full(wpool), full(pscale)],
        out_shape=[jax.ShapeDtypeStruct((dm.heads, dm.nch, dm.vh, dm.ch), MXU_DTYPE),
                   jax.ShapeDtypeStruct((dm.heads, dm.nch, 8, dm.ch), F32),
                   jax.ShapeDtypeStruct((seq, dm.pool_w), F32), jax.ShapeDtypeStruct(wo.shape, F32),
                   jax.ShapeDtypeStruct(wpool.shape, F32), jax.ShapeDtypeStruct(pscale.shape, F32)],
        compiler_params=_params("arbitrary"),
    )(g, zp, zp, zp, ot, wo, wpool, pscale)


def _attn_fwd(dm, q, k, vt):
    seq, ch, nch = k.shape[0], dm.ch, dm.nch

    def body(q_ref, k_ref, vt_ref, ot_ref, lse_ref, m_sc, l_sc, acc_sc, st0, st1, pt0, pt1):
        m_sc[...] = jnp.full_like(m_sc, -jnp.inf)
        l_sc[...] = jnp.zeros_like(l_sc)
        acc_sc[...] = jnp.zeros_like(acc_sc)
        qv = q_ref[...]

        def scores(j):
            return _mm_nt(k_ref[pl.ds(pl.multiple_of(j * ch, ch), ch), :], qv)

        def stage(j, st_cur, st_nxt, pt_cur, pt_prev):
            st_nxt[...] = scores(jnp.minimum(j + 1, nch - 1))
            acc = acc_sc[...] + _mm(vt_ref[0, jnp.maximum(j - 1, 0)], pt_prev[...])
            st = st_cur[...]
            m_prev = m_sc[...]
            m_new = jnp.maximum(m_prev, jnp.max(st, axis=0, keepdims=True))
            alpha = jnp.exp(m_prev - m_new)
            pt = jnp.exp(st - m_new)
            pt_cur[...] = pt.astype(MXU_DTYPE)
            l_sc[...] = alpha * l_sc[...] + _colsum(pt)
            acc_sc[...] = alpha * acc
            m_sc[...] = m_new

        st0[...] = scores(0)
        pt1[...] = jnp.zeros_like(pt1)

        def pair(jj, carry):
            stage(2 * jj, st0, st1, pt0, pt1)
            stage(2 * jj + 1, st1, st0, pt1, pt0)
            return carry

        lax.fori_loop(0, nch // 2, pair, 0)
        acc = acc_sc[...] + _mm(vt_ref[0, nch - 1], pt1[...])
        ot_ref[0, 0] = (acc / l_sc[...]).astype(MXU_DTYPE)
        lse_ref[0, 0] = jnp.broadcast_to(m_sc[...] + jnp.log(l_sc[...]), (8, ch))

    assert nch % 2 == 0
    chunk = lambda rows: pl.BlockSpec((1, 1, rows, ch), lambda h, i: (h, i, 0, 0))
    return pl.pallas_call(
        body, name="attn_fwd", grid=(dm.heads, nch),
        in_specs=[pl.BlockSpec((ch, LANES), lambda h, i: (i, h)),
                  pl.BlockSpec((seq, LANES), lambda h, i: (0, h)),
                  pl.BlockSpec((1, nch, dm.vh, ch), lambda h, i: (h, 0, 0, 0))],
        out_specs=[chunk(dm.vh), chunk(8)],
        out_shape=[jax.ShapeDtypeStruct((dm.heads, nch, dm.vh, ch), MXU_DTYPE),
                   jax.ShapeDtypeStruct((dm.heads, nch, 8, ch), F32)],
        scratch_shapes=[pltpu.VMEM((1, ch), F32), pltpu.VMEM((1, ch), F32), pltpu.VMEM((dm.vh, ch), F32),
                        pltpu.VMEM((ch, ch), F32), pltpu.VMEM((ch, ch), F32),
                        pltpu.VMEM((ch, ch), MXU_DTYPE), pltpu.VMEM((ch, ch), MXU_DTYPE)],
        compiler_params=_params("parallel", "parallel"),
    )(q, k, vt)


def _attn_bwd(dm, q, k, vt, dot, lse, delta, blocks=(), common=()):
    seq, ch, nch = q.shape[0], dm.ch, dm.nch
    n_ride = len(blocks) + len(common)

    def body(*refs):
        k_ref, vt_ref, q_ref, dot_ref, lse_ref, delta_ref = refs[:6]
        ride_in = refs[6:6 + n_ride]
        dqt_ref, dk_ref, dvt_ref = refs[6 + n_ride:9 + n_ride]
        ride_out = refs[9 + n_ride:9 + 2 * n_ride]
        h, j = pl.program_id(0), pl.program_id(1)

        if n_ride:
            sems = refs[9 + 2 * n_ride:]

            @pl.when((h == 0) & (j == 0))
            def _():
                for cp in _exchange_copies(ride_in[:len(blocks)], ride_in[len(blocks):], ride_out, *sems):
                    cp.start()

        kv = k_ref[...]
        kt = kv.astype(F32).T.astype(MXU_DTYPE)
        vtv = vt_ref[0, 0]
        dk_ref[...] = jnp.zeros_like(dk_ref)
        dvt_ref[...] = jnp.zeros_like(dvt_ref)

        @pl.when(j == 0)
        def _():
            dqt_ref[...] = jnp.zeros_like(dqt_ref)

        def stage(i, carry):
            qi = q_ref[pl.ds(pl.multiple_of(i * ch, ch), ch), :]
            doti = dot_ref[0, i]
            pt = jnp.exp(_mm_nt(kv, qi) - lse_ref[0, i][:1]).astype(MXU_DTYPE)
            dst = pt * (_mm_tn(vtv, doti) - delta_ref[0, i][:1]).astype(MXU_DTYPE)
            dvt_ref[0, 0] += _mm_nt(doti, pt)
            dk_ref[...] += _mm(dst, qi)
            dqt_ref[0, i] += _mm(kt, dst)
            return carry

        lax.fori_loop(0, nch, stage, 0)

        if n_ride:
            @pl.when((h == dm.heads - 1) & (j == nch - 1))
            def _():
                copies = _exchange_copies(ride_in[:len(blocks)], ride_in[len(blocks):], ride_out, *sems)
                _exchange_wait(copies)

    whole = lambda rows: pl.BlockSpec((1, nch, rows, ch), lambda h, j: (h, 0, 0, 0))
    chunk = lambda rows: pl.BlockSpec((1, 1, rows, ch), lambda h, j: (h, j, 0, 0))
    anywhere = [pl.BlockSpec(memory_space=pl.ANY)] * n_ride
    out = pl.pallas_call(
        body, name="attn_bwd_exchange" if n_ride else "attn_bwd", grid=(dm.heads, nch),
        in_specs=[pl.BlockSpec((ch, LANES), lambda h, j: (j, h)), chunk(dm.vh),
                  pl.BlockSpec((seq, LANES), lambda h, j: (0, h)), whole(dm.vh), whole(8), whole(8)] + anywhere,
        out_specs=[whole(LANES), pl.BlockSpec((ch, LANES), lambda h, j: (j, h)), chunk(dm.vh)] + anywhere,
        out_shape=[jax.ShapeDtypeStruct((dm.heads, nch, LANES, ch), F32), jax.ShapeDtypeStruct(q.shape, F32),
                   jax.ShapeDtypeStruct((dm.heads, nch, dm.vh, ch), F32)] + _exchange_out_shapes(blocks, common),
        scratch_shapes=_exchange_sems(n_ride) if n_ride else [],
        compiler_params=_params("arbitrary", "arbitrary"),
    )(k, vt, q, dot, lse, delta, *blocks, *common)
    return out[:3], out[3:]


def _loss_head(y, target):
    seq, d = y.shape
    tm = _tile(seq, LOSS_TM)

    def body(y_ref, t_ref, part_ref, dy_ref):
        @pl.when(pl.program_id(0) == 0)
        def _():
            part_ref[...] = jnp.zeros_like(part_ref)

        err = y_ref[...] - t_ref[...]
        part_ref[...] += _colsum(err * err)
        dy_ref[...] = err / d

    row = lambda i: (i, 0)
    return pl.pallas_call(
        body, name="loss_head", grid=(seq // tm,),
        in_specs=[pl.BlockSpec((tm, d), row), pl.BlockSpec((tm, d), row)],
        out_specs=[pl.BlockSpec((1, d), lambda i: (0, 0)), pl.BlockSpec((tm, d), row)],
        out_shape=[jax.ShapeDtypeStruct((1, d), F32), jax.ShapeDtypeStruct((seq, d), F32)],
        compiler_params=_params("arbitrary"),
    )(y, target)


def _my_place():
    return lax.axis_index("x"), lax.axis_index("y"), lax.axis_index("c")


def _all_gather(shards):
    n = len(shards)

    def body(*refs):
        x_refs, out_refs = refs[:n], refs[n:2 * n]
        send_sems, recv_sems, local_sems = refs[2 * n:]
        x, y, c = _my_place()
        me, sibling = (x, y, c), (x, y, 1 - c)
        chips = [(1 - x, y), (x, 1 - y), (1 - x, 1 - y)]

        def slot(t, px, py, pc):
            return out_refs[t].at[4 * px + 2 * py + pc]

        def copy(t, k, block, to, src=None):
            return pltpu.make_async_remote_copy(
                src_ref=slot(t, *block) if src is None else src, dst_ref=slot(t, *block),
                send_sem=send_sems.at[t, k], recv_sem=recv_sems.at[t, k], device_id=to, device_id_type=MESH)

        mine = [pltpu.make_async_copy(x_refs[t], slot(t, *me), local_sems.at[t]) for t in range(n)]
        started = []
        for t in range(n):
            mine[t].start()
            first = [copy(t, 0, me, sibling, src=x_refs[t])]
            first += [copy(t, 1 + j, me, (*chip, c), src=x_refs[t]) for j, chip in enumerate(chips)]
            for cp in first:
                cp.start()
            started += first
        for j, chip in enumerate(chips):
            for t in range(n):
                copy(t, 1 + j, (*chip, c), me).wait_recv()
                passed = copy(t, 4 + j, (*chip, c), sibling)
                passed.start()
                started.append(passed)
        for t in range(n):
            copy(t, 0, sibling, me).wait_recv()
            for j, chip in enumerate(chips):
                copy(t, 4 + j, (*chip, 1 - c), me).wait_recv()
        for cp in started:
            cp.wait_send()
        for cp in mine:
            cp.wait()

    return pl.pallas_call(
        body, name="weights_all_gather",
        out_shape=[jax.ShapeDtypeStruct((N_DEV,) + s.shape, s.dtype) for s in shards],
        in_specs=[pl.BlockSpec(memory_space=pl.ANY)] * n,
        out_specs=[pl.BlockSpec(memory_space=pl.ANY)] * n,
        scratch_shapes=[pltpu.SemaphoreType.DMA((n, 7)), pltpu.SemaphoreType.DMA((n, 7)),
                        pltpu.SemaphoreType.DMA((n,))],
    )(*shards)


def _exchange_copies(g_refs, c_refs, out_refs, send_sems, recv_sems, local_sems):
    x, y, c = _my_place()
    me = 4 * x + 2 * y + c
    srcs = [lambda dev, r=r: r.at[dev] for r in g_refs] + [lambda dev, r=r: r for r in c_refs]
    copies = [pltpu.make_async_copy(srcs[t](me), out_refs[t].at[me], local_sems.at[t]) for t in range(len(srcs))]
    for k in range(1, N_DEV):
        px = 1 - x if k & 4 else x
        py = 1 - y if k & 2 else y
        pc = 1 - c if k & 1 else c
        for t in range(len(srcs)):
            copies.append(pltpu.make_async_remote_copy(
                src_ref=srcs[t](4 * px + 2 * py + pc), dst_ref=out_refs[t].at[me],
                send_sem=send_sems.at[t, k - 1], recv_sem=recv_sems.at[t, k - 1],
                device_id=(px, py, pc), device_id_type=MESH))
    return copies


def _exchange_wait(copies):
    n_local = len(copies) // N_DEV
    for cp in copies[n_local:]:
        cp.wait_recv()
    for cp in copies[n_local:]:
        cp.wait_send()
    for cp in copies[:n_local]:
        cp.wait()


def _exchange_out_shapes(blocks, common):
    return ([jax.ShapeDtypeStruct(b.shape, b.dtype) for b in blocks]
            + [jax.ShapeDtypeStruct((N_DEV,) + a.shape, a.dtype) for a in common])


def _exchange_sems(n):
    return [pltpu.SemaphoreType.DMA((n, 7)), pltpu.SemaphoreType.DMA((n, 7)), pltpu.SemaphoreType.DMA((n,))]


def _grad_exchange(blocks, common):
    n, nc = len(blocks), len(common)

    def body(*refs):
        copies = _exchange_copies(refs[:n], refs[n:n + nc], refs[n + nc:2 * (n + nc)], *refs[2 * (n + nc):])
        for cp in copies:
            cp.start()
        _exchange_wait(copies)

    return pl.pallas_call(
        body, name="grad_exchange",
        out_shape=_exchange_out_shapes(blocks, common),
        in_specs=[pl.BlockSpec(memory_space=pl.ANY)] * (n + nc),
        out_specs=[pl.BlockSpec(memory_space=pl.ANY)] * (n + nc),
        scratch_shapes=_exchange_sems(n + nc),
    )(*blocks, *common)


def _adamw(parts, w, m, v):
    rows, width = w.shape
    tr = _tile(rows, max(8, ADAM_BLOCK // width // 8 * 8))

    def body(p_ref, w_ref, m_ref, v_ref, g_ref, d_ref, nm_ref, nv_ref):
        g = p_ref[0].astype(F32)
        for s in range(1, N_DEV):
            g = g + p_ref[s].astype(F32)
        nm = ADAM_B1 * m_ref[...] + (1.0 - ADAM_B1) * g
        nv = ADAM_B2 * v_ref[...] + (1.0 - ADAM_B2) * (g * g)
        m_hat = nm / (1.0 - ADAM_B1 ** ADAM_STEP)
        v_hat = nv / (1.0 - ADAM_B2 ** ADAM_STEP)
        g_ref[...] = g
        d_ref[...] = -ADAM_LR * (m_hat / (jnp.sqrt(v_hat) + ADAM_EPS) + ADAM_WD * w_ref[...])
        nm_ref[...] = nm
        nv_ref[...] = nv

    row = pl.BlockSpec((tr, width), lambda i: (i, 0))
    return pl.pallas_call(
        body, name="adamw", grid=(rows // tr,),
        in_specs=[pl.BlockSpec((N_DEV, tr, width), lambda i: (0, i, 0)), row, row, row],
        out_specs=[row] * 4,
        out_shape=[jax.ShapeDtypeStruct(w.shape, F32)] * 4,
        compiler_params=_params("parallel"),
    )(parts, w, m, v)


def _pack_rows(flat_parts, multiple):
    flat = jnp.concatenate([p.reshape(-1) for p in flat_parts])
    chunk = multiple * PACK_W
    pad = (-flat.shape[0]) % chunk
    if pad:
        flat = jnp.concatenate([flat, jnp.zeros((pad,), flat.dtype)])
    return flat.reshape(-1, PACK_W)


def _unpack(packed, shapes):
    flat = packed.reshape(-1)
    out, off = [], 0
    for shape in shapes:
        size = 1
        for s in shape:
            size *= s
        out.append(flat[off:off + size].reshape(shape))
        off += size
    return out


def _to_full(name, g):
    n, l, a, b = g.shape
    if name in ROW_SHARDED:
        return jnp.transpose(g, (1, 0, 2, 3)).reshape(l, n * a, b)
    return jnp.transpose(g, (1, 2, 0, 3)).reshape(l, a, n * b)


def _to_shards(name, full):
    l, a, b = full.shape
    if name in ROW_SHARDED:
        return jnp.transpose(full.reshape(l, N_DEV, a // N_DEV, b), (1, 0, 2, 3))
    return jnp.transpose(full.reshape(l, a, N_DEV, b // N_DEV), (2, 0, 1, 3))


def _pad_heads(w, heads, real):
    lead = w.shape[:-1]
    w = w.reshape(lead + (heads, real))
    w = jnp.concatenate([w, jnp.zeros(lead + (heads, LANES - real), w.dtype)], axis=-1)
    return w.reshape(lead + (heads * LANES,))


def _unpad_heads(w, heads, real):
    lead = w.shape[:-1]
    return w.reshape(lead + (heads, LANES))[..., :real].reshape(lead + (heads * real,))


def _pad_lanes(v, before):
    l, n = v.shape
    return jnp.concatenate([jnp.zeros((l, before), v.dtype), v, jnp.zeros((l, LANES - before - n), v.dtype)], axis=1)


def kernel(x, ffn1_norm, ffn1_w_gu, ffn1_w_down, mix_norm, w_in, q_lat_norm, kv_lat_norm, w_uq, w_uk, w_uv, q_norm, k_norm, w_pool, pool_scale, w_out, ffn2_norm, ffn2_w_gu, ffn2_w_down, loss_target, m_ffn1_norm, m_ffn1_w_gu, m_ffn1_w_down, m_mix_norm, m_w_in, m_q_lat_norm, m_kv_lat_norm, m_w_uq, m_w_uk, m_w_uv, m_q_norm, m_k_norm, m_w_pool, m_pool_scale, m_w_out, m_ffn2_norm, m_ffn2_w_gu, m_ffn2_w_down, v_ffn1_norm, v_ffn1_w_gu, v_ffn1_w_down, v_mix_norm, v_w_in, v_q_lat_norm, v_kv_lat_norm, v_w_uq, v_w_uk, v_w_uv, v_q_norm, v_k_norm, v_w_pool, v_pool_scale, v_w_out, v_ffn2_norm, v_ffn2_w_gu, v_ffn2_w_down):
    given = dict(locals())
    wts = {n: given[n] for n in WEIGHTS}
    mom1 = {n: given["m_" + n] for n in WEIGHTS}
    mom2 = {n: given["v_" + n] for n in WEIGHTS}

    depth, d = ffn1_norm.shape
    seq = x.shape[1]
    dff = ffn1_w_down.shape[1] * N_DEV
    ql, kvl, head_dim = q_lat_norm.shape[1], kv_lat_norm.shape[1], q_norm.shape[1]
    heads = w_uq.shape[2] * N_DEV // head_dim
    nope = w_uk.shape[2] * N_DEV // heads
    vh = w_uv.shape[2] * N_DEV // heads
    groups, gdim = w_pool.shape[1], w_pool.shape[2]
    pool_w = groups * gdim
    assert gdim == LANES and groups == len(POOL_WINDOWS) and head_dim <= LANES and vh <= LANES
    assert d % LANES == 0 and ql % LANES == 0 and kvl % LANES == 0 and seq % HALO == 0
    dm = _Dims(d, ql, kvl, heads, head_dim, nope, vh, pool_w, seq)

    gathered = _all_gather([wts[n].astype(MXU_DTYPE) for n in SHARDED])
    full = {n: _to_full(n, g) for n, g in zip(SHARDED, gathered)}

    zpad = jnp.zeros((depth, d, LANES), MXU_DTYPE)
    win_p = jnp.concatenate(
        [full["w_in"][..., :dm.o_pe], zpad[..., :nope], full["w_in"][..., dm.o_pe:dm.o_pe + dm.rope],
         zpad[..., :LANES - nope - dm.rope], full["w_in"][..., dm.o_pe + dm.rope:]], axis=-1)
    wuq_p = _pad_heads(full["w_uq"], heads, head_dim)
    wuk_p = _pad_heads(full["w_uk"], heads, nope)
    wuv, wo = full["w_uv"], full["w_out"]
    gq_p, gk_p = _pad_lanes(q_norm, 0), _pad_lanes(k_norm, 0)
    wpool_c = w_pool.astype(MXU_DTYPE)
    rc, rs1, rs2 = _rope_tables(seq, nope, dm.rope)
    row = lambda a, l: a[l][None, :]

    h = x[0]
    saved = []
    for l in range(depth):
        x0 = h
        x1 = _ffn_fwd(x0, row(ffn1_norm, l), full["ffn1_w_gu"][l], full["ffn1_w_down"][l])
        q, k, v, zp = _mixin_fwd(dm, x1, row(mix_norm, l), win_p[l], row(q_lat_norm, l), row(kv_lat_norm, l),
                                 wuq_p[l], wuk_p[l], wuv[l], row(gq_p, l), row(gk_p, l), rc, rs1, rs2)
        o, lse = _attn_fwd(dm, q, k, v)
        x2 = _mixout_fwd(dm, x1, zp, o, wo[l], wpool_c[l], row(pool_scale, l))
        h = _ffn_fwd(x2, row(ffn2_norm, l), full["ffn2_w_gu"][l], full["ffn2_w_down"][l])
        saved.append((x0, x1, x2, q, k, v, zp, o, lse))

    part, g = _loss_head(h, loss_target[0])
    loss = lax.psum(0.5 / d * jnp.sum(part), ("x", "y", "c"))

    def ffn_grads(grads, prefix, l, xin, gout):
        wgu, wd = full[prefix + "_w_gu"][l], full[prefix + "_w_down"][l]
        gain = row(given[prefix + "_norm"], l)
        hh, act, dg, du = _ffn_bwd_act(xin, gain, wgu, wd.T, gout)
        gin, dgain = _ffn_bwd_in(xin, gain, wgu.T, dg, du, gout)
        grads[prefix + "_norm"] = dgain[0]
        grads[prefix + "_w_gu"] = jnp.concatenate(
            [_wgrad(hh, dg, 1.0, 1024, 1408), _wgrad(hh, du, 1.0, 1024, 1408)], axis=1)
        grads[prefix + "_w_down"] = _wgrad(act, gout, 0.5, 1408, 1024)
        return gin

    received = [None] * depth
    pending = None
    for l in reversed(range(depth)):
        x0, x1, x2, q, k, v, zp, o, lse = saved[l]
        grads = {}
        g = ffn_grads(grads, "ffn2", l, x2, g)
        do, delta, dmixed, dwo, dwpool, dps = _mixout_bwd(dm, g, zp, o, wo[l], wpool_c[l], row(pool_scale, l))
        if pending is None:
            (dq, dk, dv), _ = _attn_bwd(dm, q, k, v, do, lse, delta)
        else:
            (dq, dk, dv), received[l + 1] = _attn_bwd(dm, q, k, v, do, lse, delta, *pending)
        dz, dwuq, dwuk, dwuv, dgql, dgkvl, dgq, dgk = _mla_in_bwd(
            dm, x1, row(mix_norm, l), win_p[l], row(q_lat_norm, l), row(kv_lat_norm, l), wuq_p[l], wuk_p[l],
            wuv[l], row(gq_p, l), row(gk_p, l), rc, rs1, rs2, dq, dk, dv, dmixed)
        g, dwin, dgmix = _rms_proj_bwd(x1, row(mix_norm, l), win_p[l], dz, g)
        grads["w_out"] = dwo
        grads["w_pool"] = dwpool
        grads["pool_scale"] = dps[0]
        grads["w_uq"] = _unpad_heads(dwuq, heads, head_dim)
        grads["w_uk"] = _unpad_heads(dwuk, heads, nope)
        grads["w_uv"] = dwuv
        grads["q_lat_norm"] = dgql[0]
        grads["kv_lat_norm"] = dgkvl[0]
        grads["q_norm"] = dgq[0, :head_dim]
        grads["k_norm"] = dgk[0, :head_dim]
        grads["w_in"] = jnp.concatenate(
            [dwin[:, :dm.o_pe], dwin[:, dm.o_pe + nope:dm.o_pe + nope + dm.rope], dwin[:, dm.o_pool:]], axis=1)
        grads["mix_norm"] = dgmix[0]
        g = ffn_grads(grads, "ffn1", l, x0, g)
        pending = ([_to_shards(n, grads[n][None])[:, 0].astype(WIRE_DTYPE) for n in SHARDED],
                   [_pack_rows([grads[n] for n in REPLICATED], 8)])
    received[0] = _grad_exchange(*pending)

    outs = [{}, {}, {}, {}]
    for t, n in enumerate(SHARDED):
        nl, a, b = wts[n].shape
        parts = jnp.stack([received[l][t] for l in range(depth)], axis=1)
        results = _adamw(parts.reshape(N_DEV, nl * a, b),
                         *[src[n].reshape(nl * a, b) for src in (wts, mom1, mom2)])
        for out, r in zip(outs, results):
            out[n] = r.reshape(nl, a, b)
    repl_shapes = [wts[n].shape[1:] for n in REPLICATED]
    per_layer = []
    for l in range(depth):
        results = _adamw(received[l][-1],
                         *[_pack_rows([src[n][l] for n in REPLICATED], 8) for src in (wts, mom1, mom2)])
        per_layer.append([_unpack(r, repl_shapes) for r in results])
    for i, out in enumerate(outs):
        for t, n in enumerate(REPLICATED):
            out[n] = jnp.stack([per_layer[l][i][t] for l in range(depth)])

    return (loss, g[None], *[out[n] for out in outs for n in WEIGHTS])
```

```python
import functools

import jax
import jax.numpy as jnp
from jax import lax
from jax.experimental import pallas as pl
from jax.experimental.pallas import tpu as pltpu

F32 = jnp.float32
MXU_DTYPE = jnp.bfloat16
WIRE_DTYPE = jnp.bfloat16
EPS = 1e-6
LOG2E, LN2 = 1.4426950408889634, 0.6931471805599453
ROPE_THETA = 10000.0
POOL_WINDOWS = (2, 4, 8, 16)
ADAM_LR, ADAM_B1, ADAM_B2, ADAM_EPS, ADAM_WD, ADAM_STEP = 0.001, 0.9, 0.999, 1e-08, 0.01, 10

LANES = 128
HALO = 64
PACK_W = 1024
N_DEV = 8
VMEM_LIMIT = 56 * 1024 * 1024
MESH = pl.DeviceIdType.MESH

LOSS_TM = 1024
FFN_FWD_TM, FFN_FWD_TN = 512, 1408
FFN_BWD_TM, FFN_BWD_TN = 512, 1408
FFN_IN_TM = 256
MIX_TM = 256
ATTN_CH = 1024
WG_BK = 2048
ADAM_BLOCK = 128 * 1024

SHARDED = ("ffn1_w_gu", "ffn1_w_down", "w_in", "w_uq", "w_uk", "w_uv", "w_out", "ffn2_w_gu", "ffn2_w_down")
ROW_SHARDED = ("ffn1_w_down", "w_out", "ffn2_w_down")
REPLICATED = ("ffn1_norm", "mix_norm", "q_lat_norm", "kv_lat_norm", "q_norm", "k_norm", "w_pool", "pool_scale",
              "ffn2_norm")
WEIGHTS = ("ffn1_norm", "ffn1_w_gu", "ffn1_w_down", "mix_norm", "w_in", "q_lat_norm", "kv_lat_norm", "w_uq", "w_uk",
           "w_uv", "q_norm", "k_norm", "w_pool", "pool_scale", "w_out", "ffn2_norm", "ffn2_w_gu", "ffn2_w_down")


def _tile(n, pref):
    if n <= pref:
        return n
    t = pref - pref % 8
    while n % t:
        t -= 8
    return t


def _params(*sem):
    return pltpu.CompilerParams(dimension_semantics=sem, vmem_limit_bytes=VMEM_LIMIT)


def _mm(a, b):
    return jnp.dot(a.astype(MXU_DTYPE), b.astype(MXU_DTYPE), preferred_element_type=F32)


def _mm_nt(a, b):
    return lax.dot_general(a.astype(MXU_DTYPE), b.astype(MXU_DTYPE), (((1,), (1,)), ((), ())),
                           preferred_element_type=F32)


def _mm_tn(a, b):
    return lax.dot_general(a.astype(MXU_DTYPE), b.astype(MXU_DTYPE), (((0,), (0,)), ((), ())),
                           preferred_element_type=F32)


def _rstd(x, n):
    return lax.rsqrt(jnp.sum(x * x, axis=-1, keepdims=True) / n + EPS)


def _rms_bwd(dy, xhat, r, gain, n):
    dxh = dy * gain
    return r * (dxh - xhat * (jnp.sum(dxh * xhat, axis=-1, keepdims=True) / n))


def _colsum(x):
    return jnp.sum(x, axis=0, keepdims=True)


def _split3(x):
    hi = x.astype(MXU_DTYPE)
    r1 = x - hi.astype(F32)
    mid = r1.astype(MXU_DTYPE)
    lo = (r1 - mid.astype(F32)).astype(MXU_DTYPE)
    return jnp.concatenate([hi, mid, lo], axis=1)


def _sum3(x):
    n = x.shape[1] // 3
    return (x[:, :n] + x[:, n:2 * n]) + x[:, 2 * n:]


def _pool_mixed(prev, main, nxt, row0, seq):
    tm = main.shape[0]
    k = tm + 2 * HALO
    ext = jnp.concatenate([prev, main, nxt], axis=0)
    s_i = lax.broadcasted_iota(jnp.int32, (tm, k), 0) + row0
    t_j = lax.broadcasted_iota(jnp.int32, (tm, k), 1) + (row0 - HALO)
    s_v = lax.broadcasted_iota(jnp.int32, (tm, 1), 0) + row0
    inside = (t_j >= 0) & (t_j < seq)
    outs = []
    for g, w in enumerate(POOL_WINDOWS):
        left, right = w // 2, w - 1 - w // 2
        band = jnp.where((t_j >= s_i - left) & (t_j <= s_i + right) & inside, 1.0, 0.0).astype(MXU_DTYPE)
        sl = slice(g * LANES, (g + 1) * LANES)
        wsum = _sum3(jnp.dot(band, _split3(ext[:, sl]), preferred_element_type=F32))
        cnt = (jnp.minimum(s_v + right + 1, seq) - jnp.maximum(s_v - left, 0)).astype(F32)
        outs.append(wsum / cnt - main[:, sl])
    return jnp.concatenate(outs, axis=1)


def _pool_mixed_t(prev, main, nxt, row0, seq):
    tm = main.shape[0]
    k = tm + 2 * HALO
    ext = jnp.concatenate([prev, main, nxt], axis=0)
    t_i = lax.broadcasted_iota(jnp.int32, (tm, k), 0) + row0
    s_j = lax.broadcasted_iota(jnp.int32, (tm, k), 1) + (row0 - HALO)
    s_v = lax.broadcasted_iota(jnp.int32, (k, 1), 0) + (row0 - HALO)
    inside = (s_j >= 0) & (s_j < seq)
    inside_v = (s_v >= 0) & (s_v < seq)
    outs = []
    for g, w in enumerate(POOL_WINDOWS):
        left, right = w // 2, w - 1 - w // 2
        band = jnp.where((s_j >= t_i - right) & (s_j <= t_i + left) & inside, 1.0, 0.0).astype(MXU_DTYPE)
        sl = slice(g * LANES, (g + 1) * LANES)
        cnt = (jnp.minimum(s_v + right + 1, seq) - jnp.maximum(s_v - left, 0)).astype(F32)
        scaled = jnp.where(inside_v, ext[:, sl] / jnp.maximum(cnt, 1.0), 0.0)
        outs.append(_sum3(jnp.dot(band, _split3(scaled), preferred_element_type=F32)) - main[:, sl])
    return jnp.concatenate(outs, axis=1)


def _halo_specs(tm, width, seq):
    per = tm // HALO
    last = seq // HALO - 1
    prev = pl.BlockSpec((HALO, width), lambda i: (jnp.maximum(i * per - 1, 0), 0))
    nxt = pl.BlockSpec((HALO, width), lambda i: (jnp.minimum((i + 1) * per, last), 0))
    return prev, nxt


def _rope(x, c, s1, s2, half):
    return x * c + pltpu.roll(x, half, 1) * s1 + pltpu.roll(x, LANES - half, 1) * s2


def _rope_t(d, c, s1, s2, half):
    return d * c + pltpu.roll(d * s1, LANES - half, 1) + pltpu.roll(d * s2, half, 1)


def _rope_tables(seq, nope, rope):
    half = rope // 2
    pos = jnp.arange(seq, dtype=F32)
    inv = ROPE_THETA ** (-jnp.arange(0, rope, 2, dtype=F32) / rope)
    ang = pos[:, None] * inv[None, :]
    cos, sin = jnp.cos(ang), jnp.sin(ang)
    zeros = lambda n: jnp.zeros((seq, n), F32)
    ones = lambda n: jnp.ones((seq, n), F32)
    tail = LANES - nope - rope
    c = jnp.concatenate([ones(nope), cos, cos, ones(tail)], axis=1)
    s1 = jnp.concatenate([zeros(nope + half), sin, zeros(tail)], axis=1)
    s2 = jnp.concatenate([zeros(nope), -sin, zeros(half + tail)], axis=1)
    return c, s1, s2


def _ffn_fwd(x, gain, wgu, wd):
    seq, d = x.shape
    f = wd.shape[0]
    tm, tn = _tile(seq, FFN_FWD_TM), _tile(f, FFN_FWD_TN)
    nk = f // tn

    def body(x_ref, g_ref, wg_ref, wu_ref, wd_ref, o_ref, h_sc, acc_sc):
        k = pl.program_id(1)

        @pl.when(k == 0)
        def _():
            xv = x_ref[...]
            h_sc[...] = (xv * _rstd(xv, d) * g_ref[...]).astype(MXU_DTYPE)
            acc_sc[...] = jnp.zeros_like(acc_sc)

        h = h_sc[...]
        g = jnp.dot(h, wg_ref[...], preferred_element_type=F32)
        u = jnp.dot(h, wu_ref[...], preferred_element_type=F32)
        a = g * (1.0 / (1.0 + jnp.exp(-g))) * u
        acc_sc[...] += _mm(a, wd_ref[...])

        @pl.when(k == nk - 1)
        def _():
            o_ref[...] = x_ref[...] + 0.5 * acc_sc[...]

    return pl.pallas_call(
        body, name="ffn_fwd", grid=(seq // tm, nk),
        in_specs=[pl.BlockSpec((tm, d), lambda i, k: (i, 0)),
                  pl.BlockSpec((1, d), lambda i, k: (0, 0)),
                  pl.BlockSpec((d, tn), lambda i, k: (0, k)),
                  pl.BlockSpec((d, tn), lambda i, k: (0, k + nk)),
                  pl.BlockSpec((tn, d), lambda i, k: (k, 0))],
        out_specs=pl.BlockSpec((tm, d), lambda i, k: (i, 0)),
        out_shape=jax.ShapeDtypeStruct((seq, d), F32),
        scratch_shapes=[pltpu.VMEM((tm, d), MXU_DTYPE), pltpu.VMEM((tm, d), F32)],
        compiler_params=_params("parallel", "arbitrary"),
    )(x, gain, wgu, wgu, wd)


def _ffn_bwd_act(x, gain, wgu, wdt, dout):
    seq, d = x.shape
    f = wdt.shape[1]
    tm, tn = _tile(seq, FFN_BWD_TM), _tile(f, FFN_BWD_TN)
    nk = f // tn

    def body(x_ref, g_ref, wg_ref, wu_ref, wdt_ref, do_ref, h_ref, act_ref, dg_ref, du_ref, dy_sc):
        @pl.when(pl.program_id(1) == 0)
        def _():
            xv = x_ref[...]
            h_ref[...] = (xv * _rstd(xv, d) * g_ref[...]).astype(MXU_DTYPE)
            dy_sc[...] = (0.5 * do_ref[...]).astype(MXU_DTYPE)

        h = h_ref[...]
        g = jnp.dot(h, wg_ref[...], preferred_element_type=F32)
        u = jnp.dot(h, wu_ref[...], preferred_element_type=F32)
        sig = 1.0 / (1.0 + jnp.exp(-g))
        silu = g * sig
        act_ref[...] = (silu * u).astype(MXU_DTYPE)
        da = jnp.dot(dy_sc[...], wdt_ref[...], preferred_element_type=F32)
        du_ref[...] = (da * silu).astype(MXU_DTYPE)
        dg_ref[...] = (da * u * (sig * (1.0 + g * (1.0 - sig)))).astype(MXU_DTYPE)

    row = lambda i, k: (i, 0)
    col = lambda i, k: (i, k)
    return pl.pallas_call(
        body, name="ffn_bwd_act", grid=(seq // tm, nk),
        in_specs=[pl.BlockSpec((tm, d), row),
                  pl.BlockSpec((1, d), lambda i, k: (0, 0)),
                  pl.BlockSpec((d, tn), lambda i, k: (0, k)),
                  pl.BlockSpec((d, tn), lambda i, k: (0, k + nk)),
                  pl.BlockSpec((d, tn), lambda i, k: (0, k)),
                  pl.BlockSpec((tm, d), row)],
        out_specs=[pl.BlockSpec((tm, d), row), pl.BlockSpec((tm, tn), col), pl.BlockSpec((tm, tn), col),
                   pl.BlockSpec((tm, tn), col)],
        out_shape=[jax.ShapeDtypeStruct((seq, d), MXU_DTYPE), jax.ShapeDtypeStruct((seq, f), MXU_DTYPE),
                   jax.ShapeDtypeStruct((seq, f), MXU_DTYPE), jax.ShapeDtypeStruct((seq, f), MXU_DTYPE)],
        scratch_shapes=[pltpu.VMEM((tm, d), MXU_DTYPE)],
        compiler_params=_params("parallel", "arbitrary"),
    )(x, gain, wgu, wgu, wdt, dout)


def _ffn_bwd_in(x, gain, wgut, dg, du, dout):
    seq, d = x.shape
    f = dg.shape[1]
    tm = _tile(seq, FFN_IN_TM)

    def body(x_ref, g_ref, wgut_ref, dg_ref, du_ref, do_ref, dx_ref, dgain_ref):
        @pl.when(pl.program_id(0) == 0)
        def _():
            dgain_ref[...] = jnp.zeros_like(dgain_ref)

        dh = (jnp.dot(dg_ref[...], wgut_ref[:f, :], preferred_element_type=F32)
              + jnp.dot(du_ref[...], wgut_ref[f:, :], preferred_element_type=F32))
        xv = x_ref[...]
        r = _rstd(xv, d)
        xhat = xv * r
        dgain_ref[...] += _colsum(dh * xhat)
        dx_ref[...] = do_ref[...] + _rms_bwd(dh, xhat, r, g_ref[...], d)

    row = lambda i: (i, 0)
    return pl.pallas_call(
        body, name="ffn_bwd_in", grid=(seq // tm,),
        in_specs=[pl.BlockSpec((tm, d), row), pl.BlockSpec((1, d), lambda i: (0, 0)),
                  pl.BlockSpec((2 * f, d), lambda i: (0, 0)), pl.BlockSpec((tm, f), row), pl.BlockSpec((tm, f), row),
                  pl.BlockSpec((tm, d), row)],
        out_specs=[pl.BlockSpec((tm, d), row), pl.BlockSpec((1, d), lambda i: (0, 0))],
        out_shape=[jax.ShapeDtypeStruct((seq, d), F32), jax.ShapeDtypeStruct((1, d), F32)],
        compiler_params=_params("arbitrary"),
    )(x, gain, wgut, dg, du, dout)


def _wgrad(a, b, scale, bm, bn):
    seq, m = a.shape
    n = b.shape[1]
    bm, bn, bk = _tile(m, bm), _tile(n, bn), _tile(seq, WG_BK)
    ns = seq // bk

    def body(a_ref, b_ref, o_ref):
        @pl.when(pl.program_id(2) == 0)
        def _():
            o_ref[...] = jnp.zeros_like(o_ref)

        o_ref[...] += scale * _mm_tn(a_ref[...], b_ref[...])

    return pl.pallas_call(
        body, name="wgrad", grid=(m // bm, n // bn, ns),
        in_specs=[pl.BlockSpec((bk, bm), lambda i, j, s: (s, i)),
                  pl.BlockSpec((bk, bn), lambda i, j, s: (s, j))],
        out_specs=pl.BlockSpec((bm, bn), lambda i, j, s: (i, j)),
        out_shape=jax.ShapeDtypeStruct((m, n), F32),
        compiler_params=_params("parallel", "parallel", "arbitrary"),
    )(a, b)


class _Dims:
    def __init__(self, d, ql, kvl, heads, head_dim, nope, vh, pool_w, seq):
        self.d, self.ql, self.kvl, self.heads, self.head_dim, self.nope, self.vh, self.pool_w = (
            d, ql, kvl, heads, head_dim, nope, vh, pool_w)
        self.rope = head_dim - nope
        self.half = self.rope // 2
        self.hw = heads * LANES
        self.vw = heads * vh
        self.ch = _tile(seq, ATTN_CH)
        self.nch = seq // self.ch
        self.o_kv = ql
        self.o_pe = ql + kvl
        self.o_pool = ql + kvl + LANES
        self.zw = self.o_pool + pool_w
        self.scale = head_dim ** -0.5


def _chunk_spec(dm, rows, tm):
    per = dm.ch // tm
    return pl.BlockSpec((dm.heads, 1, rows, tm), lambda i: (0, i // per, 0, i % per))


def _mixin_fwd(dm, x, gmix, win, gql, gkvl, wuq, wuk, wuv, gq, gk, rc, rs1, rs2):
    seq, d = x.shape
    tm = _tile(seq, MIX_TM)

    def body(x_ref, gmix_ref, win_ref, gql_ref, gkvl_ref, wuq_ref, wuk_ref, wuv_ref, gq_ref, gk_ref,
             rc_ref, rs1_ref, rs2_ref, q_ref, k_ref, v_ref, zp_ref):
        xv = x_ref[...]
        z = _mm(xv * _rstd(xv, d) * gmix_ref[...], win_ref[...])
        cq, ckv = z[:, :dm.o_kv], z[:, dm.o_kv:dm.o_pe]
        kpe = z[:, dm.o_pe:dm.o_pool]
        zp_ref[...] = z[:, dm.o_pool:]
        cqn = cq * _rstd(cq, dm.ql) * gql_ref[...]
        ckvn = ckv * _rstd(ckv, dm.kvl) * gkvl_ref[...]
        q = _mm(cqn, wuq_ref[...])
        kn = _mm(ckvn, wuk_ref[...])
        v_ref[...] = _mm(ckvn, wuv_ref[...]).T.reshape(dm.heads, 1, dm.vh, tm).astype(MXU_DTYPE)
        c, s1, s2 = rc_ref[...], rs1_ref[...], rs2_ref[...]
        for h in range(dm.heads):
            sl = slice(h * LANES, (h + 1) * LANES)
            qh = q[:, sl]
            qn = qh * _rstd(qh, dm.head_dim) * gq_ref[...]
            q_ref[:, sl] = (_rope(qn, c, s1, s2, dm.half) * dm.scale * LOG2E).astype(MXU_DTYPE)
            kh = kn[:, sl] + kpe
            kk = kh * _rstd(kh, dm.head_dim) * gk_ref[...]
            k_ref[:, sl] = _rope(kk, c, s1, s2, dm.half).astype(MXU_DTYPE)

    row = lambda i: (i, 0)
    full = lambda a: pl.BlockSpec(a.shape, lambda i: (0,) * a.ndim)
    return pl.pallas_call(
        body, name="mixin_fwd", grid=(seq // tm,),
        in_specs=[pl.BlockSpec((tm, d), row), full(gmix), full(win), full(gql), full(gkvl), full(wuq), full(wuk),
                  full(wuv), full(gq), full(gk),
                  pl.BlockSpec((tm, LANES), row), pl.BlockSpec((tm, LANES), row), pl.BlockSpec((tm, LANES), row)],
        out_specs=[pl.BlockSpec((tm, dm.hw), row), pl.BlockSpec((tm, dm.hw), row), _chunk_spec(dm, dm.vh, tm),
                   pl.BlockSpec((tm, dm.pool_w), row)],
        out_shape=[jax.ShapeDtypeStruct((seq, dm.hw), MXU_DTYPE),
                   jax.ShapeDtypeStruct((seq, dm.hw), MXU_DTYPE),
                   jax.ShapeDtypeStruct((dm.heads, dm.nch, dm.vh, dm.ch), MXU_DTYPE),
                   jax.ShapeDtypeStruct((seq, dm.pool_w), F32)],
        compiler_params=_params("parallel"),
    )(x, gmix, win, gql, gkvl, wuq, wuk, wuv, gq, gk, rc, rs1, rs2)


def _mla_in_bwd(dm, x, gmix, win, gql, gkvl, wuq, wuk, wuv, gq, gk, rc, rs1, rs2, dq, dk, dv, dmixed):
    seq, d = x.shape
    tm = _tile(seq, MIX_TM)

    def body(x_ref, gmix_ref, win_ref, gql_ref, gkvl_ref, wuq_ref, wuk_ref, wuv_ref, gq_ref, gk_ref,
             rc_ref, rs1_ref, rs2_ref, dq_ref, dk_ref, dv_ref, dmp_ref, dm_ref, dmn_ref,
             dz_ref, dwuq_ref, dwuk_ref, dwuv_ref, dgql_ref, dgkvl_ref, dgq_ref, dgk_ref):
        i = pl.program_id(0)

        @pl.when(i == 0)
        def _():
            for ref in (dwuq_ref, dwuk_ref, dwuv_ref, dgql_ref, dgkvl_ref, dgq_ref, dgk_ref):
                ref[...] = jnp.zeros_like(ref)

        xv = x_ref[...]
        z = _mm(xv * _rstd(xv, d) * gmix_ref[...], win_ref[...])
        cq, ckv = z[:, :dm.o_kv], z[:, dm.o_kv:dm.o_pe]
        kpe = z[:, dm.o_pe:dm.o_pool]
        r_q, r_kv = _rstd(cq, dm.ql), _rstd(ckv, dm.kvl)
        cqh, ckvh = cq * r_q, ckv * r_kv
        cqn = (cqh * gql_ref[...]).astype(MXU_DTYPE)
        ckvn = (ckvh * gkvl_ref[...]).astype(MXU_DTYPE)
        q = _mm(cqn, wuq_ref[...])
        kn = _mm(ckvn, wuk_ref[...])
        c, s1, s2 = rc_ref[...], rs1_ref[...], rs2_ref[...]
        dqv = dq_ref[...].reshape(dm.hw, tm).T
        dq_pre, dk_pre = [], []
        dkpe = jnp.zeros((tm, LANES), F32)
        dgq = jnp.zeros((1, LANES), F32)
        dgk = jnp.zeros((1, LANES), F32)
        for h in range(dm.heads):
            sl = slice(h * LANES, (h + 1) * LANES)
            qh = q[:, sl]
            rq = _rstd(qh, dm.head_dim)
            xq = qh * rq
            dqn = _rope_t(dqv[:, sl] * dm.scale, c, s1, s2, dm.half)
            dgq += _colsum(dqn * xq)
            dq_pre.append(_rms_bwd(dqn, xq, rq, gq_ref[...], dm.head_dim))
            kh = kn[:, sl] + kpe
            rk = _rstd(kh, dm.head_dim)
            xk = kh * rk
            dkn = _rope_t(dk_ref[:, sl], c, s1, s2, dm.half)
            dgk += _colsum(dkn * xk)
            dkh = _rms_bwd(dkn, xk, rk, gk_ref[...], dm.head_dim)
            dk_pre.append(dkh)
            dkpe += dkh
        dgq_ref[...] += dgq
        dgk_ref[...] += dgk
        dq_pre = jnp.concatenate(dq_pre, axis=1).astype(MXU_DTYPE)
        dk_pre = jnp.concatenate(dk_pre, axis=1).astype(MXU_DTYPE)
        dvv = dv_ref[...].reshape(dm.vw, tm).T.astype(MXU_DTYPE)
        dwuq_ref[...] += _mm_tn(cqn, dq_pre)
        dwuk_ref[...] += _mm_tn(ckvn, dk_pre)
        dwuv_ref[...] += _mm_tn(ckvn, dvv)
        dcqn = _mm_nt(dq_pre, wuq_ref[...])
        dckvn = _mm_nt(dk_pre, wuk_ref[...]) + _mm_nt(dvv, wuv_ref[...])
        dgql_ref[...] += _colsum(dcqn * cqh)
        dgkvl_ref[...] += _colsum(dckvn * ckvh)
        dcq = _rms_bwd(dcqn, cqh, r_q, gql_ref[...], dm.ql)
        dckv = _rms_bwd(dckvn, ckvh, r_kv, gkvl_ref[...], dm.kvl)
        dzp = _pool_mixed_t(dmp_ref[...], dm_ref[...], dmn_ref[...], i * tm, seq)
        dz_ref[...] = jnp.concatenate([dcq, dckv, dkpe, dzp], axis=1).astype(MXU_DTYPE)

    row = lambda i: (i, 0)
    full = lambda a: pl.BlockSpec(a.shape, lambda i: (0,) * a.ndim)
    acc = lambda shape: pl.BlockSpec(shape, lambda i: (0, 0))
    prev, nxt = _halo_specs(tm, dm.pool_w, seq)
    shapes = [(seq, dm.zw), wuq.shape, wuk.shape, wuv.shape, (1, dm.ql), (1, dm.kvl), (1, LANES), (1, LANES)]
    return pl.pallas_call(
        body, name="mla_in_bwd", grid=(seq // tm,),
        in_specs=[pl.BlockSpec((tm, d), row), full(gmix), full(win), full(gql), full(gkvl), full(wuq), full(wuk),
                  full(wuv), full(gq), full(gk),
                  pl.BlockSpec((tm, LANES), row), pl.BlockSpec((tm, LANES), row), pl.BlockSpec((tm, LANES), row),
                  _chunk_spec(dm, LANES, tm), pl.BlockSpec((tm, dm.hw), row), _chunk_spec(dm, dm.vh, tm),
                  prev, pl.BlockSpec((tm, dm.pool_w), row), nxt],
        out_specs=[pl.BlockSpec((tm, dm.zw), row)] + [acc(s) for s in shapes[1:]],
        out_shape=[jax.ShapeDtypeStruct(shapes[0], MXU_DTYPE)] + [jax.ShapeDtypeStruct(s, F32) for s in shapes[1:]],
        compiler_params=_params("arbitrary"),
    )(x, gmix, win, gql, gkvl, wuq, wuk, wuv, gq, gk, rc, rs1, rs2, dq, dk, dv, dmixed, dmixed, dmixed)


def _rms_proj_bwd(x, gain, w, dz, gin):
    seq, d = x.shape
    n = w.shape[1]
    tm = _tile(seq, MIX_TM)

    def body(x_ref, g_ref, w_ref, dz_ref, gin_ref, gout_ref, dw_ref, dgain_ref):
        @pl.when(pl.program_id(0) == 0)
        def _():
            dw_ref[...] = jnp.zeros_like(dw_ref)
            dgain_ref[...] = jnp.zeros_like(dgain_ref)

        xv = x_ref[...]
        r = _rstd(xv, d)
        xhat = xv * r
        dzv = dz_ref[...]
        dh = _mm_nt(dzv, w_ref[...])
        dw_ref[...] += _mm_tn(xhat * g_ref[...], dzv)
        dgain_ref[...] += _colsum(dh * xhat)
        gout_ref[...] = gin_ref[...] + _rms_bwd(dh, xhat, r, g_ref[...], d)

    row = lambda i: (i, 0)
    return pl.pallas_call(
        body, name="rms_proj_bwd", grid=(seq // tm,),
        in_specs=[pl.BlockSpec((tm, d), row), pl.BlockSpec((1, d), lambda i: (0, 0)),
                  pl.BlockSpec((d, n), lambda i: (0, 0)), pl.BlockSpec((tm, n), row), pl.BlockSpec((tm, d), row)],
        out_specs=[pl.BlockSpec((tm, d), row), pl.BlockSpec((d, n), lambda i: (0, 0)),
                   pl.BlockSpec((1, d), lambda i: (0, 0))],
        out_shape=[jax.ShapeDtypeStruct((seq, d), F32), jax.ShapeDtypeStruct((d, n), F32),
                   jax.ShapeDtypeStruct((1, d), F32)],
        compiler_params=_params("arbitrary"),
    )(x, gain, w, dz, gin)


def _pool_branch(mixed, wpool_ref):
    groups = mixed.shape[1] // LANES
    return jnp.concatenate(
        [_mm(mixed[:, g * LANES:(g + 1) * LANES], wpool_ref[g]) for g in range(groups)], axis=1)


def _mixout_fwd(dm, x, zp, ot, wo, wpool, pscale):
    seq, d = x.shape
    tm = _tile(seq, MIX_TM)

    def body(x_ref, zpp_ref, zp_ref, zpn_ref, ot_ref, wo_ref, wpool_ref, ps_ref, out_ref):
        mixed = _pool_mixed(zpp_ref[...], zp_ref[...], zpn_ref[...], pl.program_id(0) * tm, seq)
        b = _pool_branch(mixed, wpool_ref) * ps_ref[...]
        a = _mm_tn(ot_ref[...].reshape(dm.vw, tm), wo_ref[:dm.vw, :])
        out_ref[...] = x_ref[...] + a + _mm(b, wo_ref[dm.vw:, :])

    row = lambda i: (i, 0)
    full = lambda a: pl.BlockSpec(a.shape, lambda i: (0,) * a.ndim)
    prev, nxt = _halo_specs(tm, dm.pool_w, seq)
    return pl.pallas_call(
        body, name="mixout_fwd", grid=(seq // tm,),
        in_specs=[pl.BlockSpec((tm, d), row), prev, pl.BlockSpec((tm, dm.pool_w), row), nxt,
                  _chunk_spec(dm, dm.vh, tm), full(wo), full(wpool), full(pscale)],
        out_specs=pl.BlockSpec((tm, d), row),
        out_shape=jax.ShapeDtypeStruct((seq, d), F32),
        compiler_params=_params("parallel"),
    )(x, zp, zp, zp, ot, wo, wpool, pscale)


def _mixout_bwd(dm, g, zp, ot, wo, wpool, pscale):
    seq, d = g.shape
    tm = _tile(seq, MIX_TM)
    groups = dm.pool_w // LANES

    def body(g_ref, zpp_ref, zp_ref, zpn_ref, ot_ref, wo_ref, wpool_ref, ps_ref,
             dot_ref, delta_ref, dmixed_ref, dwo_ref, dwpool_ref, dps_ref):
        i = pl.program_id(0)

        @pl.when(i == 0)
        def _():
            for ref in (dwo_ref, dwpool_ref, dps_ref):
                ref[...] = jnp.zeros_like(ref)

        gv = g_ref[...].astype(MXU_DTYPE)
        otv = ot_ref[...].reshape(dm.vw, tm)
        dat = _mm_nt(wo_ref[:dm.vw, :], gv)
        db = _mm_nt(gv, wo_ref[dm.vw:, :])
        mixed = _pool_mixed(zpp_ref[...], zp_ref[...], zpn_ref[...], i * tm, seq).astype(MXU_DTYPE)
        y = _pool_branch(mixed, wpool_ref)
        b = (y * ps_ref[...]).astype(MXU_DTYPE)
        dwo_ref[:dm.vw, :] += _mm(otv, gv)
        dwo_ref[dm.vw:, :] += _mm_tn(b, gv)
        dps_ref[...] += _colsum(db * y)
        dy = (db * ps_ref[...]).astype(MXU_DTYPE)
        dmx = []
        for gi in range(groups):
            sl = slice(gi * LANES, (gi + 1) * LANES)
            dmx.append(_mm_nt(dy[:, sl], wpool_ref[gi]))
            dwpool_ref[gi] += _mm_tn(mixed[:, sl], dy[:, sl])
        dmixed_ref[...] = jnp.concatenate(dmx, axis=1)
        dot_ref[...] = dat.reshape(dm.heads, 1, dm.vh, tm).astype(MXU_DTYPE)
        prod = dat * otv.astype(F32)
        for h in range(dm.heads):
            delta_ref[h, 0] = jnp.broadcast_to(_colsum(prod[h * dm.vh:(h + 1) * dm.vh]), (8, tm))

    row = lambda i: (i, 0)
    full = lambda a: pl.BlockSpec(a.shape, lambda i: (0,) * a.ndim)
    prev, nxt = _halo_specs(tm, dm.pool_w, seq)
    return pl.pallas_call(
        body, name="mixout_bwd", grid=(seq // tm,),
        in_specs=[pl.BlockSpec((tm, d), row), prev, pl.BlockSpec((tm, dm.pool_w), row), nxt,
                  _chunk_spec(dm, dm.vh, tm), full(wo), full(wpool), full(pscale)],
        out_specs=[_chunk_spec(dm, dm.vh, tm), _chunk_spec(dm, 8, tm), pl.BlockSpec((tm, dm.pool_w), row),
                   full(wo), full(wpool), full(pscale)],
        out_shape=[jax.ShapeDtypeStruct((dm.heads, dm.nch, dm.vh, dm.ch), MXU_DTYPE),
                   jax.ShapeDtypeStruct((dm.heads, dm.nch, 8, dm.ch), F32),
                   jax.ShapeDtypeStruct((seq, dm.pool_w), F32), jax.ShapeDtypeStruct(wo.shape, F32),
                   jax.ShapeDtypeStruct(wpool.shape, F32), jax.ShapeDtypeStruct(pscale.shape, F32)],
        compiler_params=_params("arbitrary"),
    )(g, zp, zp, zp, ot, wo, wpool, pscale)


def _attn_fwd(dm, q, k, vt):
    seq, ch, nch = k.shape[0], dm.ch, dm.nch

    def body(q_ref, k_ref, vt_ref, ot_ref, lse_ref, m_sc, l_sc, acc_sc, st0, st1, pt0, pt1):
        m_sc[...] = jnp.full_like(m_sc, -jnp.inf)
        l_sc[...] = jnp.zeros_like(l_sc)
        acc_sc[...] = jnp.zeros_like(acc_sc)
        qv = q_ref[...]

        def scores(j):
            return _mm_nt(k_ref[pl.ds(pl.multiple_of(j * ch, ch), ch), :], qv)

        def stage(j, st_cur, st_nxt, pt_cur, pt_prev):
            st_nxt[...] = scores(jnp.minimum(j + 1, nch - 1))
            acc = acc_sc[...] + _mm(vt_ref[0, jnp.maximum(j - 1, 0)], pt_prev[...])
            st = st_cur[...]
            m_prev = m_sc[...]
            m_new = jnp.maximum(m_prev, jnp.max(st, axis=0, keepdims=True))
            alpha = jnp.exp2(m_prev - m_new)
            pt = jnp.exp2(st - m_new)
            pt_cur[...] = pt.astype(MXU_DTYPE)
            l_sc[...] = alpha * l_sc[...] + _colsum(pt)
            acc_sc[...] = alpha * acc
            m_sc[...] = m_new

        st0[...] = scores(0)
        pt1[...] = jnp.zeros_like(pt1)

        def pair(jj, carry):
            stage(2 * jj, st0, st1, pt0, pt1)
            stage(2 * jj + 1, st1, st0, pt1, pt0)
            return carry

        lax.fori_loop(0, nch // 2, pair, 0)
        acc = acc_sc[...] + _mm(vt_ref[0, nch - 1], pt1[...])
        ot_ref[0, 0] = (acc / l_sc[...]).astype(MXU_DTYPE)
        lse_ref[0, 0] = jnp.broadcast_to(m_sc[...] + jnp.log(l_sc[...]) * LOG2E, (8, ch))

    assert nch % 2 == 0
    chunk = lambda rows: pl.BlockSpec((1, 1, rows, ch), lambda h, i: (h, i, 0, 0))
    return pl.pallas_call(
        body, name="attn_fwd", grid=(dm.heads, nch),
        in_specs=[pl.BlockSpec((ch, LANES), lambda h, i: (i, h)),
                  pl.BlockSpec((seq, LANES), lambda h, i: (0, h)),
                  pl.BlockSpec((1, nch, dm.vh, ch), lambda h, i: (h, 0, 0, 0))],
        out_specs=[chunk(dm.vh), chunk(8)],
        out_shape=[jax.ShapeDtypeStruct((dm.heads, nch, dm.vh, ch), MXU_DTYPE),
                   jax.ShapeDtypeStruct((dm.heads, nch, 8, ch), F32)],
        scratch_shapes=[pltpu.VMEM((1, ch), F32), pltpu.VMEM((1, ch), F32), pltpu.VMEM((dm.vh, ch), F32),
                        pltpu.VMEM((ch, ch), F32), pltpu.VMEM((ch, ch), F32),
                        pltpu.VMEM((ch, ch), MXU_DTYPE), pltpu.VMEM((ch, ch), MXU_DTYPE)],
        compiler_params=_params("parallel", "parallel"),
    )(q, k, vt)


def _attn_bwd(dm, q, k, vt, dot, lse, delta, blocks=(), common=()):
    seq, ch, nch = q.shape[0], dm.ch, dm.nch
    n_ride = len(blocks) + len(common)

    def body(*refs):
        k_ref, vt_ref, q_ref, dot_ref, lse_ref, delta_ref = refs[:6]
        ride_in = refs[6:6 + n_ride]
        dqt_ref, dk_ref, dvt_ref = refs[6 + n_ride:9 + n_ride]
        ride_out = refs[9 + n_ride:9 + 2 * n_ride]
        h, j = pl.program_id(0), pl.program_id(1)

        if n_ride:
            sems = refs[9 + 2 * n_ride:]

            @pl.when((h == 0) & (j == 0))
            def _():
                for cp in _exchange_copies(ride_in[:len(blocks)], ride_in[len(blocks):], ride_out, *sems):
                    cp.start()

        kv = k_ref[...]
        kt = kv.astype(F32).T.astype(MXU_DTYPE)
        vtv = vt_ref[0, 0]
        dk_ref[...] = jnp.zeros_like(dk_ref)
        dvt_ref[...] = jnp.zeros_like(dvt_ref)

        @pl.when(j == 0)
        def _():
            dqt_ref[...] = jnp.zeros_like(dqt_ref)

        def stage(i, carry):
            qi = q_ref[pl.ds(pl.multiple_of(i * ch, ch), ch), :]
            doti = dot_ref[0, i]
            pt = jnp.exp2(_mm_nt(kv, qi) - lse_ref[0, i][:1]).astype(MXU_DTYPE)
            dst = pt * (_mm_tn(vtv, doti) - delta_ref[0, i][:1]).astype(MXU_DTYPE)
            dvt_ref[0, 0] += _mm_nt(doti, pt)
            dk_ref[...] += _mm(dst, qi)
            dqt_ref[0, i] += _mm(kt, dst)
            return carry

        lax.fori_loop(0, nch, stage, 0)
        dk_ref[...] = dk_ref[...] * LN2

        if n_ride:
            @pl.when((h == dm.heads - 1) & (j == nch - 1))
            def _():
                copies = _exchange_copies(ride_in[:len(blocks)], ride_in[len(blocks):], ride_out, *sems)
                _exchange_wait(copies)

    whole = lambda rows: pl.BlockSpec((1, nch, rows, ch), lambda h, j: (h, 0, 0, 0))
    chunk = lambda rows: pl.BlockSpec((1, 1, rows, ch), lambda h, j: (h, j, 0, 0))
    anywhere = [pl.BlockSpec(memory_space=pl.ANY)] * n_ride
    out = pl.pallas_call(
        body, name="attn_bwd_exchange" if n_ride else "attn_bwd", grid=(dm.heads, nch),
        in_specs=[pl.BlockSpec((ch, LANES), lambda h, j: (j, h)), chunk(dm.vh),
                  pl.BlockSpec((seq, LANES), lambda h, j: (0, h)), whole(dm.vh), whole(8), whole(8)] + anywhere,
        out_specs=[whole(LANES), pl.BlockSpec((ch, LANES), lambda h, j: (j, h)), chunk(dm.vh)] + anywhere,
        out_shape=[jax.ShapeDtypeStruct((dm.heads, nch, LANES, ch), F32), jax.ShapeDtypeStruct(q.shape, F32),
                   jax.ShapeDtypeStruct((dm.heads, nch, dm.vh, ch), F32)] + _exchange_out_shapes(blocks, common),
        scratch_shapes=_exchange_sems(n_ride) if n_ride else [],
        compiler_params=_params("arbitrary", "arbitrary"),
    )(k, vt, q, dot, lse, delta, *blocks, *common)
    return out[:3], out[3:]


def _loss_head(y, target):
    seq, d = y.shape
    tm = _tile(seq, LOSS_TM)

    def body(y_ref, t_ref, part_ref, dy_ref):
        @pl.when(pl.program_id(0) == 0)
        def _():
            part_ref[...] = jnp.zeros_like(part_ref)

        err = y_ref[...] - t_ref[...]
        part_ref[...] += _colsum(err * err)
        dy_ref[...] = err / d

    row = lambda i: (i, 0)
    return pl.pallas_call(
        body, name="loss_head", grid=(seq // tm,),
        in_specs=[pl.BlockSpec((tm, d), row), pl.BlockSpec((tm, d), row)],
        out_specs=[pl.BlockSpec((1, d), lambda i: (0, 0)), pl.BlockSpec((tm, d), row)],
        out_shape=[jax.ShapeDtypeStruct((1, d), F32), jax.ShapeDtypeStruct((seq, d), F32)],
        compiler_params=_params("arbitrary"),
    )(y, target)


def _my_place():
    return lax.axis_index("x"), lax.axis_index("y"), lax.axis_index("c")


def _all_gather(shards):
    n = len(shards)

    def body(*refs):
        x_refs, out_refs = refs[:n], refs[n:2 * n]
        send_sems, recv_sems, local_sems = refs[2 * n:]
        x, y, c = _my_place()
        me, sibling = (x, y, c), (x, y, 1 - c)
        chips = [(1 - x, y), (x, 1 - y), (1 - x, 1 - y)]

        def slot(t, px, py, pc):
            return out_refs[t].at[4 * px + 2 * py + pc]

        def copy(t, k, block, to, src=None):
            return pltpu.make_async_remote_copy(
                src_ref=slot(t, *block) if src is None else src, dst_ref=slot(t, *block),
                send_sem=send_sems.at[t, k], recv_sem=recv_sems.at[t, k], device_id=to, device_id_type=MESH)

        mine = [pltpu.make_async_copy(x_refs[t], slot(t, *me), local_sems.at[t]) for t in range(n)]
        started = []
        for t in range(n):
            mine[t].start()
            first = [copy(t, 0, me, sibling, src=x_refs[t])]
            first += [copy(t, 1 + j, me, (*chip, c), src=x_refs[t]) for j, chip in enumerate(chips)]
            for cp in first:
                cp.start()
            started += first
        for j, chip in enumerate(chips):
            for t in range(n):
                copy(t, 1 + j, (*chip, c), me).wait_recv()
                passed = copy(t, 4 + j, (*chip, c), sibling)
                passed.start()
                started.append(passed)
        for t in range(n):
            copy(t, 0, sibling, me).wait_recv()
            for j, chip in enumerate(chips):
                copy(t, 4 + j, (*chip, 1 - c), me).wait_recv()
        for cp in started:
            cp.wait_send()
        for cp in mine:
            cp.wait()

    return pl.pallas_call(
        body, name="weights_all_gather",
        out_shape=[jax.ShapeDtypeStruct((N_DEV,) + s.shape, s.dtype) for s in shards],
        in_specs=[pl.BlockSpec(memory_space=pl.ANY)] * n,
        out_specs=[pl.BlockSpec(memory_space=pl.ANY)] * n,
        scratch_shapes=[pltpu.SemaphoreType.DMA((n, 7)), pltpu.SemaphoreType.DMA((n, 7)),
                        pltpu.SemaphoreType.DMA((n,))],
    )(*shards)


def _exchange_copies(g_refs, c_refs, out_refs, send_sems, recv_sems, local_sems):
    x, y, c = _my_place()
    me = 4 * x + 2 * y + c
    srcs = [lambda dev, r=r: r.at[dev] for r in g_refs] + [lambda dev, r=r: r for r in c_refs]
    copies = [pltpu.make_async_copy(srcs[t](me), out_refs[t].at[me], local_sems.at[t]) for t in range(len(srcs))]
    for k in range(1, N_DEV):
        px = 1 - x if k & 4 else x
        py = 1 - y if k & 2 else y
        pc = 1 - c if k & 1 else c
        for t in range(len(srcs)):
            copies.append(pltpu.make_async_remote_copy(
                src_ref=srcs[t](4 * px + 2 * py + pc), dst_ref=out_refs[t].at[me],
                send_sem=send_sems.at[t, k - 1], recv_sem=recv_sems.at[t, k - 1],
                device_id=(px, py, pc), device_id_type=MESH))
    return copies


def _exchange_wait(copies):
    n_local = len(copies) // N_DEV
    for cp in copies[n_local:]:
        cp.wait_recv()
    for cp in copies[n_local:]:
        cp.wait_send()
    for cp in copies[:n_local]:
        cp.wait()


def _exchange_out_shapes(blocks, common):
    return ([jax.ShapeDtypeStruct(b.shape, b.dtype) for b in blocks]
            + [jax.ShapeDtypeStruct((N_DEV,) + a.shape, a.dtype) for a in common])


def _exchange_sems(n):
    return [pltpu.SemaphoreType.DMA((n, 7)), pltpu.SemaphoreType.DMA((n, 7)), pltpu.SemaphoreType.DMA((n,))]


def _grad_exchange(blocks, common):
    n, nc = len(blocks), len(common)

    def body(*refs):
        copies = _exchange_copies(refs[:n], refs[n:n + nc], refs[n + nc:2 * (n + nc)], *refs[2 * (n + nc):])
        for cp in copies:
            cp.start()
        _exchange_wait(copies)

    return pl.pallas_call(
        body, name="grad_exchange",
        out_shape=_exchange_out_shapes(blocks, common),
        in_specs=[pl.BlockSpec(memory_space=pl.ANY)] * (n + nc),
        out_specs=[pl.BlockSpec(memory_space=pl.ANY)] * (n + nc),
        scratch_shapes=_exchange_sems(n + nc),
    )(*blocks, *common)


def _adamw(parts, w, m, v):
    rows, width = w.shape
    tr = _tile(rows, max(8, ADAM_BLOCK // width // 8 * 8))

    def body(p_ref, w_ref, m_ref, v_ref, g_ref, d_ref, nm_ref, nv_ref):
        g = p_ref[0].astype(F32)
        for s in range(1, N_DEV):
            g = g + p_ref[s].astype(F32)
        nm = ADAM_B1 * m_ref[...] + (1.0 - ADAM_B1) * g
        nv = ADAM_B2 * v_ref[...] + (1.0 - ADAM_B2) * (g * g)
        m_hat = nm / (1.0 - ADAM_B1 ** ADAM_STEP)
        v_hat = nv / (1.0 - ADAM_B2 ** ADAM_STEP)
        g_ref[...] = g
        d_ref[...] = -ADAM_LR * (m_hat / (jnp.sqrt(v_hat) + ADAM_EPS) + ADAM_WD * w_ref[...])
        nm_ref[...] = nm
        nv_ref[...] = nv

    row = pl.BlockSpec((tr, width), lambda i: (i, 0))
    return pl.pallas_call(
        body, name="adamw", grid=(rows // tr,),
        in_specs=[pl.BlockSpec((N_DEV, tr, width), lambda i: (0, i, 0)), row, row, row],
        out_specs=[row] * 4,
        out_shape=[jax.ShapeDtypeStruct(w.shape, F32)] * 4,
        compiler_params=_params("parallel"),
    )(parts, w, m, v)


def _pack_rows(flat_parts, multiple):
    flat = jnp.concatenate([p.reshape(-1) for p in flat_parts])
    chunk = multiple * PACK_W
    pad = (-flat.shape[0]) % chunk
    if pad:
        flat = jnp.concatenate([flat, jnp.zeros((pad,), flat.dtype)])
    return flat.reshape(-1, PACK_W)


def _unpack(packed, shapes):
    flat = packed.reshape(-1)
    out, off = [], 0
    for shape in shapes:
        size = 1
        for s in shape:
            size *= s
        out.append(flat[off:off + size].reshape(shape))
        off += size
    return out


def _to_full(name, g):
    n, l, a, b = g.shape
    if name in ROW_SHARDED:
        return jnp.transpose(g, (1, 0, 2, 3)).reshape(l, n * a, b)
    return jnp.transpose(g, (1, 2, 0, 3)).reshape(l, a, n * b)


def _to_shards(name, full):
    l, a, b = full.shape
    if name in ROW_SHARDED:
        return jnp.transpose(full.reshape(l, N_DEV, a // N_DEV, b), (1, 0, 2, 3))
    return jnp.transpose(full.reshape(l, a, N_DEV, b // N_DEV), (2, 0, 1, 3))


def _pad_heads(w, heads, real):
    lead = w.shape[:-1]
    w = w.reshape(lead + (heads, real))
    w = jnp.concatenate([w, jnp.zeros(lead + (heads, LANES - real), w.dtype)], axis=-1)
    return w.reshape(lead + (heads * LANES,))


def _unpad_heads(w, heads, real):
    lead = w.shape[:-1]
    return w.reshape(lead + (heads, LANES))[..., :real].reshape(lead + (heads * real,))


def _pad_lanes(v, before):
    l, n = v.shape
    return jnp.concatenate([jnp.zeros((l, before), v.dtype), v, jnp.zeros((l, LANES - before - n), v.dtype)], axis=1)


def kernel(x, ffn1_norm, ffn1_w_gu, ffn1_w_down, mix_norm, w_in, q_lat_norm, kv_lat_norm, w_uq, w_uk, w_uv, q_norm, k_norm, w_pool, pool_scale, w_out, ffn2_norm, ffn2_w_gu, ffn2_w_down, loss_target, m_ffn1_norm, m_ffn1_w_gu, m_ffn1_w_down, m_mix_norm, m_w_in, m_q_lat_norm, m_kv_lat_norm, m_w_uq, m_w_uk, m_w_uv, m_q_norm, m_k_norm, m_w_pool, m_pool_scale, m_w_out, m_ffn2_norm, m_ffn2_w_gu, m_ffn2_w_down, v_ffn1_norm, v_ffn1_w_gu, v_ffn1_w_down, v_mix_norm, v_w_in, v_q_lat_norm, v_kv_lat_norm, v_w_uq, v_w_uk, v_w_uv, v_q_norm, v_k_norm, v_w_pool, v_pool_scale, v_w_out, v_ffn2_norm, v_ffn2_w_gu, v_ffn2_w_down):
    given = dict(locals())
    wts = {n: given[n] for n in WEIGHTS}
    mom1 = {n: given["m_" + n] for n in WEIGHTS}
    mom2 = {n: given["v_" + n] for n in WEIGHTS}

    depth, d = ffn1_norm.shape
    seq = x.shape[1]
    dff = ffn1_w_down.shape[1] * N_DEV
    ql, kvl, head_dim = q_lat_norm.shape[1], kv_lat_norm.shape[1], q_norm.shape[1]
    heads = w_uq.shape[2] * N_DEV // head_dim
    nope = w_uk.shape[2] * N_DEV // heads
    vh = w_uv.shape[2] * N_DEV // heads
    groups, gdim = w_pool.shape[1], w_pool.shape[2]
    pool_w = groups * gdim
    assert gdim == LANES and groups == len(POOL_WINDOWS) and head_dim <= LANES and vh <= LANES
    assert d % LANES == 0 and ql % LANES == 0 and kvl % LANES == 0 and seq % HALO == 0
    dm = _Dims(d, ql, kvl, heads, head_dim, nope, vh, pool_w, seq)

    gathered = _all_gather([wts[n].astype(MXU_DTYPE) for n in SHARDED])
    full = {n: _to_full(n, g) for n, g in zip(SHARDED, gathered)}

    zpad = jnp.zeros((depth, d, LANES), MXU_DTYPE)
    win_p = jnp.concatenate(
        [full["w_in"][..., :dm.o_pe], zpad[..., :nope], full["w_in"][..., dm.o_pe:dm.o_pe + dm.rope],
         zpad[..., :LANES - nope - dm.rope], full["w_in"][..., dm.o_pe + dm.rope:]], axis=-1)
    wuq_p = _pad_heads(full["w_uq"], heads, head_dim)
    wuk_p = _pad_heads(full["w_uk"], heads, nope)
    wuv, wo = full["w_uv"], full["w_out"]
    gq_p, gk_p = _pad_lanes(q_norm, 0), _pad_lanes(k_norm, 0)
    wpool_c = w_pool.astype(MXU_DTYPE)
    rc, rs1, rs2 = _rope_tables(seq, nope, dm.rope)
    row = lambda a, l: a[l][None, :]

    h = x[0]
    saved = []
    for l in range(depth):
        x0 = h
        x1 = _ffn_fwd(x0, row(ffn1_norm, l), full["ffn1_w_gu"][l], full["ffn1_w_down"][l])
        q, k, v, zp = _mixin_fwd(dm, x1, row(mix_norm, l), win_p[l], row(q_lat_norm, l), row(kv_lat_norm, l),
                                 wuq_p[l], wuk_p[l], wuv[l], row(gq_p, l), row(gk_p, l), rc, rs1, rs2)
        o, lse = _attn_fwd(dm, q, k, v)
        x2 = _mixout_fwd(dm, x1, zp, o, wo[l], wpool_c[l], row(pool_scale, l))
        h = _ffn_fwd(x2, row(ffn2_norm, l), full["ffn2_w_gu"][l], full["ffn2_w_down"][l])
        saved.append((x0, x1, x2, q, k, v, zp, o, lse))

    part, g = _loss_head(h, loss_target[0])
    loss = lax.psum(0.5 / d * jnp.sum(part), ("x", "y", "c"))

    def ffn_grads(grads, prefix, l, xin, gout):
        wgu, wd = full[prefix + "_w_gu"][l], full[prefix + "_w_down"][l]
        gain = row(given[prefix + "_norm"], l)
        hh, act, dg, du = _ffn_bwd_act(xin, gain, wgu, wd.T, gout)
        gin, dgain = _ffn_bwd_in(xin, gain, wgu.T, dg, du, gout)
        grads[prefix + "_norm"] = dgain[0]
        grads[prefix + "_w_gu"] = jnp.concatenate(
            [_wgrad(hh, dg, 1.0, 1024, 1408), _wgrad(hh, du, 1.0, 1024, 1408)], axis=1)
        grads[prefix + "_w_down"] = _wgrad(act, gout, 0.5, 1408, 1024)
        return gin

    received = [None] * depth
    pending = None
    for l in reversed(range(depth)):
        x0, x1, x2, q, k, v, zp, o, lse = saved[l]
        grads = {}
        g = ffn_grads(grads, "ffn2", l, x2, g)
        do, delta, dmixed, dwo, dwpool, dps = _mixout_bwd(dm, g, zp, o, wo[l], wpool_c[l], row(pool_scale, l))
        if pending is None:
            (dq, dk, dv), _ = _attn_bwd(dm, q, k, v, do, lse, delta)
        else:
            (dq, dk, dv), received[l + 1] = _attn_bwd(dm, q, k, v, do, lse, delta, *pending)
        dz, dwuq, dwuk, dwuv, dgql, dgkvl, dgq, dgk = _mla_in_bwd(
            dm, x1, row(mix_norm, l), win_p[l], row(q_lat_norm, l), row(kv_lat_norm, l), wuq_p[l], wuk_p[l],
            wuv[l], row(gq_p, l), row(gk_p, l), rc, rs1, rs2, dq, dk, dv, dmixed)
        g, dwin, dgmix = _rms_proj_bwd(x1, row(mix_norm, l), win_p[l], dz, g)
        grads["w_out"] = dwo
        grads["w_pool"] = dwpool
        grads["pool_scale"] = dps[0]
        grads["w_uq"] = _unpad_heads(dwuq, heads, head_dim)
        grads["w_uk"] = _unpad_heads(dwuk, heads, nope)
        grads["w_uv"] = dwuv
        grads["q_lat_norm"] = dgql[0]
        grads["kv_lat_norm"] = dgkvl[0]
        grads["q_norm"] = dgq[0, :head_dim]
        grads["k_norm"] = dgk[0, :head_dim]
        grads["w_in"] = jnp.concatenate(
            [dwin[:, :dm.o_pe], dwin[:, dm.o_pe + nope:dm.o_pe + nope + dm.rope], dwin[:, dm.o_pool:]], axis=1)
        grads["mix_norm"] = dgmix[0]
        g = ffn_grads(grads, "ffn1", l, x0, g)
        pending = ([_to_shards(n, grads[n][None])[:, 0].astype(WIRE_DTYPE) for n in SHARDED],
                   [_pack_rows([grads[n] for n in REPLICATED], 8)])
    received[0] = _grad_exchange(*pending)

    outs = [{}, {}, {}, {}]
    for t, n in enumerate(SHARDED):
        nl, a, b = wts[n].shape
        parts = jnp.stack([received[l][t] for l in range(depth)], axis=1)
        results = _adamw(parts.reshape(N_DEV, nl * a, b),
                         *[src[n].reshape(nl * a, b) for src in (wts, mom1, mom2)])
        for out, r in zip(outs, results):
            out[n] = r.reshape(nl, a, b)
    repl_shapes = [wts[n].shape[1:] for n in REPLICATED]
    per_layer = []
    for l in range(depth):
        results = _adamw(received[l][-1],
                         *[_pack_rows([src[n][l] for n in REPLICATED], 8) for src in (wts, mom1, mom2)])
        per_layer.append([_unpack(r, repl_shapes) for r in results])
    for i, out in enumerate(outs):
        for t, n in enumerate(REPLICATED):
            out[n] = jnp.stack([per_layer[l][i][t] for l in range(depth)])

    return (loss, g[None], *[out[n] for out in outs for n in WEIGHTS])
```

```python
import functools

import jax
import jax.numpy as jnp
from jax import lax
from jax.experimental import pallas as pl
from jax.experimental.pallas import tpu as pltpu

F32 = jnp.float32
MXU_DTYPE = jnp.bfloat16
WIRE_DTYPE = jnp.bfloat16
EPS = 1e-6
LOG2E, LN2 = 1.4426950408889634, 0.6931471805599453
ROPE_THETA = 10000.0
POOL_WINDOWS = (2, 4, 8, 16)
ADAM_LR, ADAM_B1, ADAM_B2, ADAM_EPS, ADAM_WD, ADAM_STEP = 0.001, 0.9, 0.999, 1e-08, 0.01, 10

LANES = 128
HALO = 64
PACK_W = 1024
N_DEV = 8
VMEM_LIMIT = 56 * 1024 * 1024
MESH = pl.DeviceIdType.MESH

LOSS_TM = 1024
FFN_FWD_TM, FFN_FWD_TN = 512, 1408
FFN_BWD_TM, FFN_BWD_TN = 512, 1408
FFN_IN_TM = 256
MIX_TM = 256
ATTN_CH = 1024
WG_BK = 2048
ADAM_BLOCK = 128 * 1024

SHARDED = ("ffn1_w_gu", "ffn1_w_down", "w_in", "w_uq", "w_uk", "w_uv", "w_out", "ffn2_w_gu", "ffn2_w_down")
ROW_SHARDED = ("ffn1_w_down", "w_out", "ffn2_w_down")
REPLICATED = ("ffn1_norm", "mix_norm", "q_lat_norm", "kv_lat_norm", "q_norm", "k_norm", "w_pool", "pool_scale",
              "ffn2_norm")
WEIGHTS = ("ffn1_norm", "ffn1_w_gu", "ffn1_w_down", "mix_norm", "w_in", "q_lat_norm", "kv_lat_norm", "w_uq", "w_uk",
           "w_uv", "q_norm", "k_norm", "w_pool", "pool_scale", "w_out", "ffn2_norm", "ffn2_w_gu", "ffn2_w_down")


def _tile(n, pref):
    if n <= pref:
        return n
    t = pref - pref % 8
    while n % t:
        t -= 8
    return t


def _params(*sem):
    return pltpu.CompilerParams(dimension_semantics=sem, vmem_limit_bytes=VMEM_LIMIT)


def _mm(a, b):
    return jnp.dot(a.astype(MXU_DTYPE), b.astype(MXU_DTYPE), preferred_element_type=F32)


def _mm_nt(a, b):
    return lax.dot_general(a.astype(MXU_DTYPE), b.astype(MXU_DTYPE), (((1,), (1,)), ((), ())),
                           preferred_element_type=F32)


def _mm_tn(a, b):
    return lax.dot_general(a.astype(MXU_DTYPE), b.astype(MXU_DTYPE), (((0,), (0,)), ((), ())),
                           preferred_element_type=F32)


def _rstd(x, n):
    return lax.rsqrt(jnp.sum(x * x, axis=-1, keepdims=True) / n + EPS)


def _rms_bwd(dy, xhat, r, gain, n):
    dxh = dy * gain
    return r * (dxh - xhat * (jnp.sum(dxh * xhat, axis=-1, keepdims=True) / n))


def _colsum(x):
    return jnp.sum(x, axis=0, keepdims=True)


def _split3(x):
    hi = x.astype(MXU_DTYPE)
    r1 = x - hi.astype(F32)
    mid = r1.astype(MXU_DTYPE)
    lo = (r1 - mid.astype(F32)).astype(MXU_DTYPE)
    return jnp.concatenate([hi, mid, lo], axis=1)


def _sum3(x):
    n = x.shape[1] // 3
    return (x[:, :n] + x[:, n:2 * n]) + x[:, 2 * n:]


def _pool_mixed(prev, main, nxt, row0, seq):
    tm = main.shape[0]
    k = tm + 2 * HALO
    ext = jnp.concatenate([prev, main, nxt], axis=0)
    s_i = lax.broadcasted_iota(jnp.int32, (tm, k), 0) + row0
    t_j = lax.broadcasted_iota(jnp.int32, (tm, k), 1) + (row0 - HALO)
    s_v = lax.broadcasted_iota(jnp.int32, (tm, 1), 0) + row0
    inside = (t_j >= 0) & (t_j < seq)
    outs = []
    for g, w in enumerate(POOL_WINDOWS):
        left, right = w // 2, w - 1 - w // 2
        band = jnp.where((t_j >= s_i - left) & (t_j <= s_i + right) & inside, 1.0, 0.0).astype(MXU_DTYPE)
        sl = slice(g * LANES, (g + 1) * LANES)
        wsum = _sum3(jnp.dot(band, _split3(ext[:, sl]), preferred_element_type=F32))
        cnt = (jnp.minimum(s_v + right + 1, seq) - jnp.maximum(s_v - left, 0)).astype(F32)
        outs.append(wsum / cnt - main[:, sl])
    return jnp.concatenate(outs, axis=1)


def _pool_mixed_t(prev, main, nxt, row0, seq):
    tm = main.shape[0]
    k = tm + 2 * HALO
    ext = jnp.concatenate([prev, main, nxt], axis=0)
    t_i = lax.broadcasted_iota(jnp.int32, (tm, k), 0) + row0
    s_j = lax.broadcasted_iota(jnp.int32, (tm, k), 1) + (row0 - HALO)
    s_v = lax.broadcasted_iota(jnp.int32, (k, 1), 0) + (row0 - HALO)
    inside = (s_j >= 0) & (s_j < seq)
    inside_v = (s_v >= 0) & (s_v < seq)
    outs = []
    for g, w in enumerate(POOL_WINDOWS):
        left, right = w // 2, w - 1 - w // 2
        band = jnp.where((s_j >= t_i - right) & (s_j <= t_i + left) & inside, 1.0, 0.0).astype(MXU_DTYPE)
        sl = slice(g * LANES, (g + 1) * LANES)
        cnt = (jnp.minimum(s_v + right + 1, seq) - jnp.maximum(s_v - left, 0)).astype(F32)
        scaled = jnp.where(inside_v, ext[:, sl] / jnp.maximum(cnt, 1.0), 0.0)
        outs.append(_sum3(jnp.dot(band, _split3(scaled), preferred_element_type=F32)) - main[:, sl])
    return jnp.concatenate(outs, axis=1)


def _halo_specs(tm, width, seq):
    per = tm // HALO
    last = seq // HALO - 1
    prev = pl.BlockSpec((HALO, width), lambda i: (jnp.maximum(i * per - 1, 0), 0))
    nxt = pl.BlockSpec((HALO, width), lambda i: (jnp.minimum((i + 1) * per, last), 0))
    return prev, nxt


def _rope(x, c, s1, s2, half):
    return x * c + pltpu.roll(x, half, 1) * s1 + pltpu.roll(x, LANES - half, 1) * s2


def _rope_t(d, c, s1, s2, half):
    return d * c + pltpu.roll(d * s1, LANES - half, 1) + pltpu.roll(d * s2, half, 1)


def _rope_tables(seq, nope, rope):
    half = rope // 2
    pos = jnp.arange(seq, dtype=F32)
    inv = ROPE_THETA ** (-jnp.arange(0, rope, 2, dtype=F32) / rope)
    ang = pos[:, None] * inv[None, :]
    cos, sin = jnp.cos(ang), jnp.sin(ang)
    zeros = lambda n: jnp.zeros((seq, n), F32)
    ones = lambda n: jnp.ones((seq, n), F32)
    tail = LANES - nope - rope
    c = jnp.concatenate([ones(nope), cos, cos, ones(tail)], axis=1)
    s1 = jnp.concatenate([zeros(nope + half), sin, zeros(tail)], axis=1)
    s2 = jnp.concatenate([zeros(nope), -sin, zeros(half + tail)], axis=1)
    return c, s1, s2


def _ffn_fwd(x, gain, wgu, wd):
    seq, d = x.shape
    f = wd.shape[0]
    tm, tn = _tile(seq, FFN_FWD_TM), _tile(f, FFN_FWD_TN)
    nk = f // tn

    def body(x_ref, g_ref, wg_ref, wu_ref, wd_ref, o_ref, h_sc, acc_sc):
        k = pl.program_id(1)

        @pl.when(k == 0)
        def _():
            xv = x_ref[...]
            h_sc[...] = (xv * _rstd(xv, d) * g_ref[...]).astype(MXU_DTYPE)
            acc_sc[...] = jnp.zeros_like(acc_sc)

        h = h_sc[...]
        g = jnp.dot(h, wg_ref[...], preferred_element_type=F32)
        u = jnp.dot(h, wu_ref[...], preferred_element_type=F32)
        a = g * (1.0 / (1.0 + jnp.exp(-g))) * u
        acc_sc[...] += _mm(a, wd_ref[...])

        @pl.when(k == nk - 1)
        def _():
            o_ref[...] = x_ref[...] + 0.5 * acc_sc[...]

    return pl.pallas_call(
        body, name="ffn_fwd", grid=(seq // tm, nk),
        in_specs=[pl.BlockSpec((tm, d), lambda i, k: (i, 0)),
                  pl.BlockSpec((1, d), lambda i, k: (0, 0)),
                  pl.BlockSpec((d, tn), lambda i, k: (0, k)),
                  pl.BlockSpec((d, tn), lambda i, k: (0, k + nk)),
                  pl.BlockSpec((tn, d), lambda i, k: (k, 0))],
        out_specs=pl.BlockSpec((tm, d), lambda i, k: (i, 0)),
        out_shape=jax.ShapeDtypeStruct((seq, d), F32),
        scratch_shapes=[pltpu.VMEM((tm, d), MXU_DTYPE), pltpu.VMEM((tm, d), F32)],
        compiler_params=_params("parallel", "arbitrary"),
    )(x, gain, wgu, wgu, wd)


def _ffn_bwd_act(x, gain, wgu, wdt, dout):
    seq, d = x.shape
    f = wdt.shape[1]
    tm, tn = _tile(seq, FFN_BWD_TM), _tile(f, FFN_BWD_TN)
    nk = f // tn

    def body(x_ref, g_ref, wg_ref, wu_ref, wdt_ref, do_ref, h_ref, act_ref, dg_ref, du_ref, dy_sc):
        @pl.when(pl.program_id(1) == 0)
        def _():
            xv = x_ref[...]
            h_ref[...] = (xv * _rstd(xv, d) * g_ref[...]).astype(MXU_DTYPE)
            dy_sc[...] = (0.5 * do_ref[...]).astype(MXU_DTYPE)

        h = h_ref[...]
        g = jnp.dot(h, wg_ref[...], preferred_element_type=F32)
        u = jnp.dot(h, wu_ref[...], preferred_element_type=F32)
        sig = 1.0 / (1.0 + jnp.exp(-g))
        silu = g * sig
        act_ref[...] = (silu * u).astype(MXU_DTYPE)
        da = jnp.dot(dy_sc[...], wdt_ref[...], preferred_element_type=F32)
        du_ref[...] = (da * silu).astype(MXU_DTYPE)
        dg_ref[...] = (da * u * (sig * (1.0 + g * (1.0 - sig)))).astype(MXU_DTYPE)

    row = lambda i, k: (i, 0)
    col = lambda i, k: (i, k)
    return pl.pallas_call(
        body, name="ffn_bwd_act", grid=(seq // tm, nk),
        in_specs=[pl.BlockSpec((tm, d), row),
                  pl.BlockSpec((1, d), lambda i, k: (0, 0)),
                  pl.BlockSpec((d, tn), lambda i, k: (0, k)),
                  pl.BlockSpec((d, tn), lambda i, k: (0, k + nk)),
                  pl.BlockSpec((d, tn), lambda i, k: (0, k)),
                  pl.BlockSpec((tm, d), row)],
        out_specs=[pl.BlockSpec((tm, d), row), pl.BlockSpec((tm, tn), col), pl.BlockSpec((tm, tn), col),
                   pl.BlockSpec((tm, tn), col)],
        out_shape=[jax.ShapeDtypeStruct((seq, d), MXU_DTYPE), jax.ShapeDtypeStruct((seq, f), MXU_DTYPE),
                   jax.ShapeDtypeStruct((seq, f), MXU_DTYPE), jax.ShapeDtypeStruct((seq, f), MXU_DTYPE)],
        scratch_shapes=[pltpu.VMEM((tm, d), MXU_DTYPE)],
        compiler_params=_params("parallel", "arbitrary"),
    )(x, gain, wgu, wgu, wdt, dout)


def _ffn_bwd_in(x, gain, wgut, dg, du, dout):
    seq, d = x.shape
    f = dg.shape[1]
    tm = _tile(seq, FFN_IN_TM)

    def body(x_ref, g_ref, wgut_ref, dg_ref, du_ref, do_ref, dx_ref, dgain_ref):
        @pl.when(pl.program_id(0) == 0)
        def _():
            dgain_ref[...] = jnp.zeros_like(dgain_ref)

        dh = (jnp.dot(dg_ref[...], wgut_ref[:f, :], preferred_element_type=F32)
              + jnp.dot(du_ref[...], wgut_ref[f:, :], preferred_element_type=F32))
        xv = x_ref[...]
        r = _rstd(xv, d)
        xhat = xv * r
        dgain_ref[...] += _colsum(dh * xhat)
        dx_ref[...] = do_ref[...] + _rms_bwd(dh, xhat, r, g_ref[...], d)

    row = lambda i: (i, 0)
    return pl.pallas_call(
        body, name="ffn_bwd_in", grid=(seq // tm,),
        in_specs=[pl.BlockSpec((tm, d), row), pl.BlockSpec((1, d), lambda i: (0, 0)),
                  pl.BlockSpec((2 * f, d), lambda i: (0, 0)), pl.BlockSpec((tm, f), row), pl.BlockSpec((tm, f), row),
                  pl.BlockSpec((tm, d), row)],
        out_specs=[pl.BlockSpec((tm, d), row), pl.BlockSpec((1, d), lambda i: (0, 0))],
        out_shape=[jax.ShapeDtypeStruct((seq, d), F32), jax.ShapeDtypeStruct((1, d), F32)],
        compiler_params=_params("arbitrary"),
    )(x, gain, wgut, dg, du, dout)


def _wgrad(a, b, scale, bm, bn):
    seq, m = a.shape
    n = b.shape[1]
    bm, bn, bk = _tile(m, bm), _tile(n, bn), _tile(seq, WG_BK)
    ns = seq // bk

    def body(a_ref, b_ref, o_ref):
        @pl.when(pl.program_id(2) == 0)
        def _():
            o_ref[...] = jnp.zeros_like(o_ref)

        o_ref[...] += scale * _mm_tn(a_ref[...], b_ref[...])

    return pl.pallas_call(
        body, name="wgrad", grid=(m // bm, n // bn, ns),
        in_specs=[pl.BlockSpec((bk, bm), lambda i, j, s: (s, i)),
                  pl.BlockSpec((bk, bn), lambda i, j, s: (s, j))],
        out_specs=pl.BlockSpec((bm, bn), lambda i, j, s: (i, j)),
        out_shape=jax.ShapeDtypeStruct((m, n), F32),
        compiler_params=_params("parallel", "parallel", "arbitrary"),
    )(a, b)


class _Dims:
    def __init__(self, d, ql, kvl, heads, head_dim, nope, vh, pool_w, seq):
        self.d, self.ql, self.kvl, self.heads, self.head_dim, self.nope, self.vh, self.pool_w = (
            d, ql, kvl, heads, head_dim, nope, vh, pool_w)
        self.rope = head_dim - nope
        self.half = self.rope // 2
        self.hw = heads * LANES
        self.vw = heads * vh
        self.ch = _tile(seq, ATTN_CH)
        self.nch = seq // self.ch
        self.o_kv = ql
        self.o_pe = ql + kvl
        self.o_pool = ql + kvl + LANES
        self.zw = self.o_pool + pool_w
        self.scale = head_dim ** -0.5


def _chunk_spec(dm, rows, tm):
    per = dm.ch // tm
    return pl.BlockSpec((dm.heads, 1, rows, tm), lambda i: (0, i // per, 0, i % per))


def _mixin_fwd(dm, x, gmix, win, gql, gkvl, wuq, wuk, wuv, gq, gk, rc, rs1, rs2):
    seq, d = x.shape
    tm = _tile(seq, MIX_TM)

    def body(x_ref, gmix_ref, win_ref, gql_ref, gkvl_ref, wuq_ref, wuk_ref, wuv_ref, gq_ref, gk_ref,
             rc_ref, rs1_ref, rs2_ref, q_ref, k_ref, v_ref, zp_ref):
        xv = x_ref[...]
        z = _mm(xv * _rstd(xv, d) * gmix_ref[...], win_ref[...])
        cq, ckv = z[:, :dm.o_kv], z[:, dm.o_kv:dm.o_pe]
        kpe = z[:, dm.o_pe:dm.o_pool]
        zp_ref[...] = z[:, dm.o_pool:]
        cqn = cq * _rstd(cq, dm.ql) * gql_ref[...]
        ckvn = ckv * _rstd(ckv, dm.kvl) * gkvl_ref[...]
        q = _mm(cqn, wuq_ref[...])
        kn = _mm(ckvn, wuk_ref[...])
        v_ref[...] = _mm(ckvn, wuv_ref[...]).T.reshape(dm.heads, 1, dm.vh, tm).astype(MXU_DTYPE)
        c, s1, s2 = rc_ref[...], rs1_ref[...], rs2_ref[...]
        for h in range(dm.heads):
            sl = slice(h * LANES, (h + 1) * LANES)
            qh = q[:, sl]
            qn = qh * _rstd(qh, dm.head_dim) * gq_ref[...]
            q_ref[:, sl] = (_rope(qn, c, s1, s2, dm.half) * dm.scale * LOG2E).astype(MXU_DTYPE)
            kh = kn[:, sl] + kpe
            kk = kh * _rstd(kh, dm.head_dim) * gk_ref[...]
            k_ref[:, sl] = _rope(kk, c, s1, s2, dm.half).astype(MXU_DTYPE)

    row = lambda i: (i, 0)
    full = lambda a: pl.BlockSpec(a.shape, lambda i: (0,) * a.ndim)
    return pl.pallas_call(
        body, name="mixin_fwd", grid=(seq // tm,),
        in_specs=[pl.BlockSpec((tm, d), row), full(gmix), full(win), full(gql), full(gkvl), full(wuq), full(wuk),
                  full(wuv), full(gq), full(gk),
                  pl.BlockSpec((tm, LANES), row), pl.BlockSpec((tm, LANES), row), pl.BlockSpec((tm, LANES), row)],
        out_specs=[pl.BlockSpec((tm, dm.hw), row), pl.BlockSpec((tm, dm.hw), row), _chunk_spec(dm, dm.vh, tm),
                   pl.BlockSpec((tm, dm.pool_w), row)],
        out_shape=[jax.ShapeDtypeStruct((seq, dm.hw), MXU_DTYPE),
                   jax.ShapeDtypeStruct((seq, dm.hw), MXU_DTYPE),
                   jax.ShapeDtypeStruct((dm.heads, dm.nch, dm.vh, dm.ch), MXU_DTYPE),
                   jax.ShapeDtypeStruct((seq, dm.pool_w), F32)],
        compiler_params=_params("parallel"),
    )(x, gmix, win, gql, gkvl, wuq, wuk, wuv, gq, gk, rc, rs1, rs2)


def _mla_in_bwd(dm, x, gmix, win, gql, gkvl, wuq, wuk, wuv, gq, gk, rc, rs1, rs2, dq, dk, dv, dmixed):
    seq, d = x.shape
    tm = _tile(seq, MIX_TM)

    def body(x_ref, gmix_ref, win_ref, gql_ref, gkvl_ref, wuq_ref, wuk_ref, wuv_ref, gq_ref, gk_ref,
             rc_ref, rs1_ref, rs2_ref, dq_ref, dk_ref, dv_ref, dmp_ref, dm_ref, dmn_ref,
             dz_ref, dwuq_ref, dwuk_ref, dwuv_ref, dgql_ref, dgkvl_ref, dgq_ref, dgk_ref):
        i = pl.program_id(0)

        @pl.when(i == 0)
        def _():
            for ref in (dwuq_ref, dwuk_ref, dwuv_ref, dgql_ref, dgkvl_ref, dgq_ref, dgk_ref):
                ref[...] = jnp.zeros_like(ref)

        xv = x_ref[...]
        z = _mm(xv * _rstd(xv, d) * gmix_ref[...], win_ref[...])
        cq, ckv = z[:, :dm.o_kv], z[:, dm.o_kv:dm.o_pe]
        kpe = z[:, dm.o_pe:dm.o_pool]
        r_q, r_kv = _rstd(cq, dm.ql), _rstd(ckv, dm.kvl)
        cqh, ckvh = cq * r_q, ckv * r_kv
        cqn = (cqh * gql_ref[...]).astype(MXU_DTYPE)
        ckvn = (ckvh * gkvl_ref[...]).astype(MXU_DTYPE)
        q = _mm(cqn, wuq_ref[...])
        kn = _mm(ckvn, wuk_ref[...])
        c, s1, s2 = rc_ref[...], rs1_ref[...], rs2_ref[...]
        dqv = dq_ref[...].reshape(dm.hw, tm).T
        dq_pre, dk_pre = [], []
        dkpe = jnp.zeros((tm, LANES), F32)
        dgq = jnp.zeros((1, LANES), F32)
        dgk = jnp.zeros((1, LANES), F32)
        for h in range(dm.heads):
            sl = slice(h * LANES, (h + 1) * LANES)
            qh = q[:, sl]
            rq = _rstd(qh, dm.head_dim)
            xq = qh * rq
            dqn = _rope_t(dqv[:, sl] * dm.scale, c, s1, s2, dm.half)
            dgq += _colsum(dqn * xq)
            dq_pre.append(_rms_bwd(dqn, xq, rq, gq_ref[...], dm.head_dim))
            kh = kn[:, sl] + kpe
            rk = _rstd(kh, dm.head_dim)
            xk = kh * rk
            dkn = _rope_t(dk_ref[:, sl], c, s1, s2, dm.half)
            dgk += _colsum(dkn * xk)
            dkh = _rms_bwd(dkn, xk, rk, gk_ref[...], dm.head_dim)
            dk_pre.append(dkh)
            dkpe += dkh
        dgq_ref[...] += dgq
        dgk_ref[...] += dgk
        dq_pre = jnp.concatenate(dq_pre, axis=1).astype(MXU_DTYPE)
        dk_pre = jnp.concatenate(dk_pre, axis=1).astype(MXU_DTYPE)
        dvv = dv_ref[...].reshape(dm.vw, tm).T.astype(MXU_DTYPE)
        dwuq_ref[...] += _mm_tn(cqn, dq_pre)
        dwuk_ref[...] += _mm_tn(ckvn, dk_pre)
        dwuv_ref[...] += _mm_tn(ckvn, dvv)
        dcqn = _mm_nt(dq_pre, wuq_ref[...])
        dckvn = _mm_nt(dk_pre, wuk_ref[...]) + _mm_nt(dvv, wuv_ref[...])
        dgql_ref[...] += _colsum(dcqn * cqh)
        dgkvl_ref[...] += _colsum(dckvn * ckvh)
        dcq = _rms_bwd(dcqn, cqh, r_q, gql_ref[...], dm.ql)
        dckv = _rms_bwd(dckvn, ckvh, r_kv, gkvl_ref[...], dm.kvl)
        dzp = _pool_mixed_t(dmp_ref[...], dm_ref[...], dmn_ref[...], i * tm, seq)
        dz_ref[...] = jnp.concatenate([dcq, dckv, dkpe, dzp], axis=1).astype(MXU_DTYPE)

    row = lambda i: (i, 0)
    full = lambda a: pl.BlockSpec(a.shape, lambda i: (0,) * a.ndim)
    acc = lambda shape: pl.BlockSpec(shape, lambda i: (0, 0))
    prev, nxt = _halo_specs(tm, dm.pool_w, seq)
    shapes = [(seq, dm.zw), wuq.shape, wuk.shape, wuv.shape, (1, dm.ql), (1, dm.kvl), (1, LANES), (1, LANES)]
    return pl.pallas_call(
        body, name="mla_in_bwd", grid=(seq // tm,),
        in_specs=[pl.BlockSpec((tm, d), row), full(gmix), full(win), full(gql), full(gkvl), full(wuq), full(wuk),
                  full(wuv), full(gq), full(gk),
                  pl.BlockSpec((tm, LANES), row), pl.BlockSpec((tm, LANES), row), pl.BlockSpec((tm, LANES), row),
                  _chunk_spec(dm, LANES, tm), pl.BlockSpec((tm, dm.hw), row), _chunk_spec(dm, dm.vh, tm),
                  prev, pl.BlockSpec((tm, dm.pool_w), row), nxt],
        out_specs=[pl.BlockSpec((tm, dm.zw), row)] + [acc(s) for s in shapes[1:]],
        out_shape=[jax.ShapeDtypeStruct(shapes[0], MXU_DTYPE)] + [jax.ShapeDtypeStruct(s, F32) for s in shapes[1:]],
        compiler_params=_params("arbitrary"),
    )(x, gmix, win, gql, gkvl, wuq, wuk, wuv, gq, gk, rc, rs1, rs2, dq, dk, dv, dmixed, dmixed, dmixed)


def _rms_proj_bwd(x, gain, w, dz, gin):
    seq, d = x.shape
    n = w.shape[1]
    tm = _tile(seq, MIX_TM)

    def body(x_ref, g_ref, w_ref, dz_ref, gin_ref, gout_ref, dw_ref, dgain_ref):
        @pl.when(pl.program_id(0) == 0)
        def _():
            dw_ref[...] = jnp.zeros_like(dw_ref)
            dgain_ref[...] = jnp.zeros_like(dgain_ref)

        xv = x_ref[...]
        r = _rstd(xv, d)
        xhat = xv * r
        dzv = dz_ref[...]
        dh = _mm_nt(dzv, w_ref[...])
        dw_ref[...] += _mm_tn(xhat * g_ref[...], dzv)
        dgain_ref[...] += _colsum(dh * xhat)
        gout_ref[...] = gin_ref[...] + _rms_bwd(dh, xhat, r, g_ref[...], d)

    row = lambda i: (i, 0)
    return pl.pallas_call(
        body, name="rms_proj_bwd", grid=(seq // tm,),
        in_specs=[pl.BlockSpec((tm, d), row), pl.BlockSpec((1, d), lambda i: (0, 0)),
                  pl.BlockSpec((d, n), lambda i: (0, 0)), pl.BlockSpec((tm, n), row), pl.BlockSpec((tm, d), row)],
        out_specs=[pl.BlockSpec((tm, d), row), pl.BlockSpec((d, n), lambda i: (0, 0)),
                   pl.BlockSpec((1, d), lambda i: (0, 0))],
        out_shape=[jax.ShapeDtypeStruct((seq, d), F32), jax.ShapeDtypeStruct((d, n), F32),
                   jax.ShapeDtypeStruct((1, d), F32)],
        compiler_params=_params("arbitrary"),
    )(x, gain, w, dz, gin)


def _pool_branch(mixed, wpool_ref):
    groups = mixed.shape[1] // LANES
    return jnp.concatenate(
        [_mm(mixed[:, g * LANES:(g + 1) * LANES], wpool_ref[g]) for g in range(groups)], axis=1)


def _mixout_fwd(dm, x, zp, ot, wo, wpool, pscale):
    seq, d = x.shape
    tm = _tile(seq, MIX_TM)

    def body(x_ref, zpp_ref, zp_ref, zpn_ref, ot_ref, wo_ref, wpool_ref, ps_ref, out_ref):
        mixed = _pool_mixed(zpp_ref[...], zp_ref[...], zpn_ref[...], pl.program_id(0) * tm, seq)
        b = _pool_branch(mixed, wpool_ref) * ps_ref[...]
        a = _mm_tn(ot_ref[...].reshape(dm.vw, tm), wo_ref[:dm.vw, :])
        out_ref[...] = x_ref[...] + a + _mm(b, wo_ref[dm.vw:, :])

    row = lambda i: (i, 0)
    full = lambda a: pl.BlockSpec(a.shape, lambda i: (0,) * a.ndim)
    prev, nxt = _halo_specs(tm, dm.pool_w, seq)
    return pl.pallas_call(
        body, name="mixout_fwd", grid=(seq // tm,),
        in_specs=[pl.BlockSpec((tm, d), row), prev, pl.BlockSpec((tm, dm.pool_w), row), nxt,
                  _chunk_spec(dm, dm.vh, tm), full(wo), full(wpool), full(pscale)],
        out_specs=pl.BlockSpec((tm, d), row),
        out_shape=jax.ShapeDtypeStruct((seq, d), F32),
        compiler_params=_params("parallel"),
    )(x, zp, zp, zp, ot, wo, wpool, pscale)


def _mixout_bwd(dm, g, zp, ot, wo, wpool, pscale):
    seq, d = g.shape
    tm = _tile(seq, MIX_TM)
    groups = dm.pool_w // LANES

    def body(g_ref, zpp_ref, zp_ref, zpn_ref, ot_ref, wo_ref, wpool_ref, ps_ref,
             dot_ref, delta_ref, dmixed_ref, dwo_ref, dwpool_ref, dps_ref):
        i = pl.program_id(0)

        @pl.when(i == 0)
        def _():
            for ref in (dwo_ref, dwpool_ref, dps_ref):
                ref[...] = jnp.zeros_like(ref)

        gv = g_ref[...].astype(MXU_DTYPE)
        otv = ot_ref[...].reshape(dm.vw, tm)
        dat = _mm_nt(wo_ref[:dm.vw, :], gv)
        db = _mm_nt(gv, wo_ref[dm.vw:, :])
        mixed = _pool_mixed(zpp_ref[...], zp_ref[...], zpn_ref[...], i * tm, seq).astype(MXU_DTYPE)
        y = _pool_branch(mixed, wpool_ref)
        b = (y * ps_ref[...]).astype(MXU_DTYPE)
        dwo_ref[:dm.vw, :] += _mm(otv, gv)
        dwo_ref[dm.vw:, :] += _mm_tn(b, gv)
        dps_ref[...] += _colsum(db * y)
        dy = (db * ps_ref[...]).astype(MXU_DTYPE)
        dmx = []
        for gi in range(groups):
            sl = slice(gi * LANES, (gi + 1) * LANES)
            dmx.append(_mm_nt(dy[:, sl], wpool_ref[gi]))
            dwpool_ref[gi] += _mm_tn(mixed[:, sl], dy[:, sl])
        dmixed_ref[...] = jnp.concatenate(dmx, axis=1)
        dot_ref[...] = dat.reshape(dm.heads, 1, dm.vh, tm).astype(MXU_DTYPE)
        prod = dat * otv.astype(F32)
        for h in range(dm.heads):
            delta_ref[h, 0] = jnp.broadcast_to(_colsum(prod[h * dm.vh:(h + 1) * dm.vh]), (8, tm))

    row = lambda i: (i, 0)
    full = lambda a: pl.BlockSpec(a.shape, lambda i: (0,) * a.ndim)
    prev, nxt = _halo_specs(tm, dm.pool_w, seq)
    return pl.pallas_call(
        body, name="mixout_bwd", grid=(seq // tm,),
        in_specs=[pl.BlockSpec((tm, d), row), prev, pl.BlockSpec((tm, dm.pool_w), row), nxt,
                  _chunk_spec(dm, dm.vh, tm), full(wo), full(wpool), full(pscale)],
        out_specs=[_chunk_spec(dm, dm.vh, tm), _chunk_spec(dm, 8, tm), pl.BlockSpec((tm, dm.pool_w), row),
                   full(wo), full(wpool), full(pscale)],
        out_shape=[jax.ShapeDtypeStruct((dm.heads, dm.nch, dm.vh, dm.ch), MXU_DTYPE),
                   jax.ShapeDtypeStruct((dm.heads, dm.nch, 8, dm.ch), F32),
                   jax.ShapeDtypeStruct((seq, dm.pool_w), F32), jax.ShapeDtypeStruct(wo.shape, F32),
                   jax.ShapeDtypeStruct(wpool.shape, F32), jax.ShapeDtypeStruct(pscale.shape, F32)],
        compiler_params=_params("arbitrary"),
    )(g, zp, zp, zp, ot, wo, wpool, pscale)


def _attn_fwd(dm, q, k, vt, common=()):
    seq, ch, nch = k.shape[0], dm.ch, dm.nch
    n_ride = len(common)

    def body(*refs):
        q_ref, k_ref, vt_ref = refs[:3]
        ride_in = refs[3:3 + n_ride]
        ot_ref, lse_ref = refs[3 + n_ride:5 + n_ride]
        ride_out = refs[5 + n_ride:5 + 2 * n_ride]
        m_sc, l_sc, acc_sc, st0, st1, pt0, pt1 = refs[5 + 2 * n_ride:12 + 2 * n_ride]
        sems = refs[12 + 2 * n_ride:]
        first = (pl.program_id(0) == 0) & (pl.program_id(1) == 0)
        last = (pl.program_id(0) == dm.heads - 1) & (pl.program_id(1) == nch - 1)

        if n_ride:
            @pl.when(first)
            def _():
                for cp in _exchange_copies((), ride_in, ride_out, *sems):
                    cp.start()

        m_sc[...] = jnp.full_like(m_sc, -jnp.inf)
        l_sc[...] = jnp.zeros_like(l_sc)
        acc_sc[...] = jnp.zeros_like(acc_sc)
        qv = q_ref[...]

        def scores(j):
            return _mm_nt(k_ref[pl.ds(pl.multiple_of(j * ch, ch), ch), :], qv)

        def stage(j, st_cur, st_nxt, pt_cur, pt_prev):
            st_nxt[...] = scores(jnp.minimum(j + 1, nch - 1))
            acc = acc_sc[...] + _mm(vt_ref[0, jnp.maximum(j - 1, 0)], pt_prev[...])
            st = st_cur[...]
            m_prev = m_sc[...]
            m_new = jnp.maximum(m_prev, jnp.max(st, axis=0, keepdims=True))
            alpha = jnp.exp2(m_prev - m_new)
            pt = jnp.exp2(st - m_new)
            pt_cur[...] = pt.astype(MXU_DTYPE)
            l_sc[...] = alpha * l_sc[...] + _colsum(pt)
            acc_sc[...] = alpha * acc
            m_sc[...] = m_new

        st0[...] = scores(0)
        pt1[...] = jnp.zeros_like(pt1)

        def pair(jj, carry):
            stage(2 * jj, st0, st1, pt0, pt1)
            stage(2 * jj + 1, st1, st0, pt1, pt0)
            return carry

        lax.fori_loop(0, nch // 2, pair, 0)
        acc = acc_sc[...] + _mm(vt_ref[0, nch - 1], pt1[...])
        ot_ref[0, 0] = (acc / l_sc[...]).astype(MXU_DTYPE)
        lse_ref[0, 0] = jnp.broadcast_to(m_sc[...] + jnp.log(l_sc[...]) * LOG2E, (8, ch))

        if n_ride:
            @pl.when(last)
            def _():
                _exchange_wait(_exchange_copies((), ride_in, ride_out, *sems))

    assert nch % 2 == 0
    chunk = lambda rows: pl.BlockSpec((1, 1, rows, ch), lambda h, i: (h, i, 0, 0))
    anywhere = [pl.BlockSpec(memory_space=pl.ANY)] * n_ride
    out = pl.pallas_call(
        body, name="attn_fwd_gather" if n_ride else "attn_fwd", grid=(dm.heads, nch),
        in_specs=[pl.BlockSpec((ch, LANES), lambda h, i: (i, h)),
                  pl.BlockSpec((seq, LANES), lambda h, i: (0, h)),
                  pl.BlockSpec((1, nch, dm.vh, ch), lambda h, i: (h, 0, 0, 0))] + anywhere,
        out_specs=[chunk(dm.vh), chunk(8)] + anywhere,
        out_shape=[jax.ShapeDtypeStruct((dm.heads, nch, dm.vh, ch), MXU_DTYPE),
                   jax.ShapeDtypeStruct((dm.heads, nch, 8, ch), F32)] + _exchange_out_shapes((), common),
        scratch_shapes=[pltpu.VMEM((1, ch), F32), pltpu.VMEM((1, ch), F32), pltpu.VMEM((dm.vh, ch), F32),
                        pltpu.VMEM((ch, ch), F32), pltpu.VMEM((ch, ch), F32),
                        pltpu.VMEM((ch, ch), MXU_DTYPE), pltpu.VMEM((ch, ch), MXU_DTYPE)]
        + (_exchange_sems(n_ride) if n_ride else []),
        compiler_params=_params("arbitrary", "arbitrary") if n_ride else _params("parallel", "parallel"),
    )(q, k, vt, *common)
    return out[:2], out[2:]


def _attn_bwd(dm, q, k, vt, dot, lse, delta, blocks=(), common=()):
    seq, ch, nch = q.shape[0], dm.ch, dm.nch
    n_ride = len(blocks) + len(common)

    def body(*refs):
        k_ref, vt_ref, q_ref, dot_ref, lse_ref, delta_ref = refs[:6]
        ride_in = refs[6:6 + n_ride]
        dqt_ref, dk_ref, dvt_ref = refs[6 + n_ride:9 + n_ride]
        ride_out = refs[9 + n_ride:9 + 2 * n_ride]
        h, j = pl.program_id(0), pl.program_id(1)

        if n_ride:
            sems = refs[9 + 2 * n_ride:]

            @pl.when((h == 0) & (j == 0))
            def _():
                for cp in _exchange_copies(ride_in[:len(blocks)], ride_in[len(blocks):], ride_out, *sems):
                    cp.start()

        kv = k_ref[...]
        kt = kv.astype(F32).T.astype(MXU_DTYPE)
        vtv = vt_ref[0, 0]
        dk_ref[...] = jnp.zeros_like(dk_ref)
        dvt_ref[...] = jnp.zeros_like(dvt_ref)

        @pl.when(j == 0)
        def _():
            dqt_ref[...] = jnp.zeros_like(dqt_ref)

        def stage(i, carry):
            qi = q_ref[pl.ds(pl.multiple_of(i * ch, ch), ch), :]
            doti = dot_ref[0, i]
            pt = jnp.exp2(_mm_nt(kv, qi) - lse_ref[0, i][:1]).astype(MXU_DTYPE)
            dst = pt * (_mm_tn(vtv, doti) - delta_ref[0, i][:1]).astype(MXU_DTYPE)
            dvt_ref[0, 0] += _mm_nt(doti, pt)
            dk_ref[...] += _mm(dst, qi)
            dqt_ref[0, i] += _mm(kt, dst)
            return carry

        lax.fori_loop(0, nch, stage, 0)
        dk_ref[...] = dk_ref[...] * LN2

        if n_ride:
            @pl.when((h == dm.heads - 1) & (j == nch - 1))
            def _():
                copies = _exchange_copies(ride_in[:len(blocks)], ride_in[len(blocks):], ride_out, *sems)
                _exchange_wait(copies)

    whole = lambda rows: pl.BlockSpec((1, nch, rows, ch), lambda h, j: (h, 0, 0, 0))
    chunk = lambda rows: pl.BlockSpec((1, 1, rows, ch), lambda h, j: (h, j, 0, 0))
    anywhere = [pl.BlockSpec(memory_space=pl.ANY)] * n_ride
    out = pl.pallas_call(
        body, name="attn_bwd_exchange" if n_ride else "attn_bwd", grid=(dm.heads, nch),
        in_specs=[pl.BlockSpec((ch, LANES), lambda h, j: (j, h)), chunk(dm.vh),
                  pl.BlockSpec((seq, LANES), lambda h, j: (0, h)), whole(dm.vh), whole(8), whole(8)] + anywhere,
        out_specs=[whole(LANES), pl.BlockSpec((ch, LANES), lambda h, j: (j, h)), chunk(dm.vh)] + anywhere,
        out_shape=[jax.ShapeDtypeStruct((dm.heads, nch, LANES, ch), F32), jax.ShapeDtypeStruct(q.shape, F32),
                   jax.ShapeDtypeStruct((dm.heads, nch, dm.vh, ch), F32)] + _exchange_out_shapes(blocks, common),
        scratch_shapes=_exchange_sems(n_ride) if n_ride else [],
        compiler_params=_params("arbitrary", "arbitrary"),
    )(k, vt, q, dot, lse, delta, *blocks, *common)
    return out[:3], out[3:]


def _loss_head(y, target):
    seq, d = y.shape
    tm = _tile(seq, LOSS_TM)

    def body(y_ref, t_ref, part_ref, dy_ref):
        @pl.when(pl.program_id(0) == 0)
        def _():
            part_ref[...] = jnp.zeros_like(part_ref)

        err = y_ref[...] - t_ref[...]
        part_ref[...] += _colsum(err * err)
        dy_ref[...] = err / d

    row = lambda i: (i, 0)
    return pl.pallas_call(
        body, name="loss_head", grid=(seq // tm,),
        in_specs=[pl.BlockSpec((tm, d), row), pl.BlockSpec((tm, d), row)],
        out_specs=[pl.BlockSpec((1, d), lambda i: (0, 0)), pl.BlockSpec((tm, d), row)],
        out_shape=[jax.ShapeDtypeStruct((1, d), F32), jax.ShapeDtypeStruct((seq, d), F32)],
        compiler_params=_params("arbitrary"),
    )(y, target)


def _my_place():
    return lax.axis_index("x"), lax.axis_index("y"), lax.axis_index("c")


def _all_gather(shards):
    n = len(shards)

    def body(*refs):
        x_refs, out_refs = refs[:n], refs[n:2 * n]
        send_sems, recv_sems, local_sems = refs[2 * n:]
        x, y, c = _my_place()
        me, sibling = (x, y, c), (x, y, 1 - c)
        chips = [(1 - x, y), (x, 1 - y), (1 - x, 1 - y)]

        def slot(t, px, py, pc):
            return out_refs[t].at[4 * px + 2 * py + pc]

        def copy(t, k, block, to, src=None):
            return pltpu.make_async_remote_copy(
                src_ref=slot(t, *block) if src is None else src, dst_ref=slot(t, *block),
                send_sem=send_sems.at[t, k], recv_sem=recv_sems.at[t, k], device_id=to, device_id_type=MESH)

        mine = [pltpu.make_async_copy(x_refs[t], slot(t, *me), local_sems.at[t]) for t in range(n)]
        started = []
        for t in range(n):
            mine[t].start()
            first = [copy(t, 0, me, sibling, src=x_refs[t])]
            first += [copy(t, 1 + j, me, (*chip, c), src=x_refs[t]) for j, chip in enumerate(chips)]
            for cp in first:
                cp.start()
            started += first
        for j, chip in enumerate(chips):
            for t in range(n):
                copy(t, 1 + j, (*chip, c), me).wait_recv()
                passed = copy(t, 4 + j, (*chip, c), sibling)
                passed.start()
                started.append(passed)
        for t in range(n):
            copy(t, 0, sibling, me).wait_recv()
            for j, chip in enumerate(chips):
                copy(t, 4 + j, (*chip, 1 - c), me).wait_recv()
        for cp in started:
            cp.wait_send()
        for cp in mine:
            cp.wait()

    return pl.pallas_call(
        body, name="weights_all_gather",
        out_shape=[jax.ShapeDtypeStruct((N_DEV,) + s.shape, s.dtype) for s in shards],
        in_specs=[pl.BlockSpec(memory_space=pl.ANY)] * n,
        out_specs=[pl.BlockSpec(memory_space=pl.ANY)] * n,
        scratch_shapes=[pltpu.SemaphoreType.DMA((n, 7)), pltpu.SemaphoreType.DMA((n, 7)),
                        pltpu.SemaphoreType.DMA((n,))],
    )(*shards)


def _exchange_copies(g_refs, c_refs, out_refs, send_sems, recv_sems, local_sems):
    x, y, c = _my_place()
    me = 4 * x + 2 * y + c
    srcs = [lambda dev, r=r: r.at[dev] for r in g_refs] + [lambda dev, r=r: r for r in c_refs]
    copies = [pltpu.make_async_copy(srcs[t](me), out_refs[t].at[me], local_sems.at[t]) for t in range(len(srcs))]
    for k in range(1, N_DEV):
        px = 1 - x if k & 4 else x
        py = 1 - y if k & 2 else y
        pc = 1 - c if k & 1 else c
        for t in range(len(srcs)):
            copies.append(pltpu.make_async_remote_copy(
                src_ref=srcs[t](4 * px + 2 * py + pc), dst_ref=out_refs[t].at[me],
                send_sem=send_sems.at[t, k - 1], recv_sem=recv_sems.at[t, k - 1],
                device_id=(px, py, pc), device_id_type=MESH))
    return copies


def _exchange_wait(copies):
    n_local = len(copies) // N_DEV
    for cp in copies[n_local:]:
        cp.wait_recv()
    for cp in copies[n_local:]:
        cp.wait_send()
    for cp in copies[:n_local]:
        cp.wait()


def _exchange_out_shapes(blocks, common):
    return ([jax.ShapeDtypeStruct(b.shape, b.dtype) for b in blocks]
            + [jax.ShapeDtypeStruct((N_DEV,) + a.shape, a.dtype) for a in common])


def _exchange_sems(n):
    return [pltpu.SemaphoreType.DMA((n, 7)), pltpu.SemaphoreType.DMA((n, 7)), pltpu.SemaphoreType.DMA((n,))]


def _grad_exchange(blocks, common):
    n, nc = len(blocks), len(common)

    def body(*refs):
        copies = _exchange_copies(refs[:n], refs[n:n + nc], refs[n + nc:2 * (n + nc)], *refs[2 * (n + nc):])
        for cp in copies:
            cp.start()
        _exchange_wait(copies)

    return pl.pallas_call(
        body, name="grad_exchange",
        out_shape=_exchange_out_shapes(blocks, common),
        in_specs=[pl.BlockSpec(memory_space=pl.ANY)] * (n + nc),
        out_specs=[pl.BlockSpec(memory_space=pl.ANY)] * (n + nc),
        scratch_shapes=_exchange_sems(n + nc),
    )(*blocks, *common)


def _adamw(parts, w, m, v):
    rows, width = w.shape
    tr = _tile(rows, max(8, ADAM_BLOCK // width // 8 * 8))

    def body(p_ref, w_ref, m_ref, v_ref, g_ref, d_ref, nm_ref, nv_ref):
        g = p_ref[0].astype(F32)
        for s in range(1, N_DEV):
            g = g + p_ref[s].astype(F32)
        nm = ADAM_B1 * m_ref[...] + (1.0 - ADAM_B1) * g
        nv = ADAM_B2 * v_ref[...] + (1.0 - ADAM_B2) * (g * g)
        m_hat = nm / (1.0 - ADAM_B1 ** ADAM_STEP)
        v_hat = nv / (1.0 - ADAM_B2 ** ADAM_STEP)
        g_ref[...] = g
        d_ref[...] = -ADAM_LR * (m_hat / (jnp.sqrt(v_hat) + ADAM_EPS) + ADAM_WD * w_ref[...])
        nm_ref[...] = nm
        nv_ref[...] = nv

    row = pl.BlockSpec((tr, width), lambda i: (i, 0))
    return pl.pallas_call(
        body, name="adamw", grid=(rows // tr,),
        in_specs=[pl.BlockSpec((N_DEV, tr, width), lambda i: (0, i, 0)), row, row, row],
        out_specs=[row] * 4,
        out_shape=[jax.ShapeDtypeStruct(w.shape, F32)] * 4,
        compiler_params=_params("parallel"),
    )(parts, w, m, v)


def _pack_rows(flat_parts, multiple):
    flat = jnp.concatenate([p.reshape(-1) for p in flat_parts])
    chunk = multiple * PACK_W
    pad = (-flat.shape[0]) % chunk
    if pad:
        flat = jnp.concatenate([flat, jnp.zeros((pad,), flat.dtype)])
    return flat.reshape(-1, PACK_W)


def _unpack(packed, shapes):
    flat = packed.reshape(-1)
    out, off = [], 0
    for shape in shapes:
        size = 1
        for s in shape:
            size *= s
        out.append(flat[off:off + size].reshape(shape))
        off += size
    return out


def _to_full(name, g):
    n, l, a, b = g.shape
    if name in ROW_SHARDED:
        return jnp.transpose(g, (1, 0, 2, 3)).reshape(l, n * a, b)
    return jnp.transpose(g, (1, 2, 0, 3)).reshape(l, a, n * b)


def _to_shards(name, full):
    l, a, b = full.shape
    if name in ROW_SHARDED:
        return jnp.transpose(full.reshape(l, N_DEV, a // N_DEV, b), (1, 0, 2, 3))
    return jnp.transpose(full.reshape(l, a, N_DEV, b // N_DEV), (2, 0, 1, 3))


def _pad_heads(w, heads, real):
    lead = w.shape[:-1]
    w = w.reshape(lead + (heads, real))
    w = jnp.concatenate([w, jnp.zeros(lead + (heads, LANES - real), w.dtype)], axis=-1)
    return w.reshape(lead + (heads * LANES,))


def _unpad_heads(w, heads, real):
    lead = w.shape[:-1]
    return w.reshape(lead + (heads, LANES))[..., :real].reshape(lead + (heads * real,))


def _pad_lanes(v, before):
    l, n = v.shape
    return jnp.concatenate([jnp.zeros((l, before), v.dtype), v, jnp.zeros((l, LANES - before - n), v.dtype)], axis=1)


def kernel(x, ffn1_norm, ffn1_w_gu, ffn1_w_down, mix_norm, w_in, q_lat_norm, kv_lat_norm, w_uq, w_uk, w_uv, q_norm, k_norm, w_pool, pool_scale, w_out, ffn2_norm, ffn2_w_gu, ffn2_w_down, loss_target, m_ffn1_norm, m_ffn1_w_gu, m_ffn1_w_down, m_mix_norm, m_w_in, m_q_lat_norm, m_kv_lat_norm, m_w_uq, m_w_uk, m_w_uv, m_q_norm, m_k_norm, m_w_pool, m_pool_scale, m_w_out, m_ffn2_norm, m_ffn2_w_gu, m_ffn2_w_down, v_ffn1_norm, v_ffn1_w_gu, v_ffn1_w_down, v_mix_norm, v_w_in, v_q_lat_norm, v_kv_lat_norm, v_w_uq, v_w_uk, v_w_uv, v_q_norm, v_k_norm, v_w_pool, v_pool_scale, v_w_out, v_ffn2_norm, v_ffn2_w_gu, v_ffn2_w_down):
    given = dict(locals())
    wts = {n: given[n] for n in WEIGHTS}
    mom1 = {n: given["m_" + n] for n in WEIGHTS}
    mom2 = {n: given["v_" + n] for n in WEIGHTS}

    depth, d = ffn1_norm.shape
    seq = x.shape[1]
    dff = ffn1_w_down.shape[1] * N_DEV
    ql, kvl, head_dim = q_lat_norm.shape[1], kv_lat_norm.shape[1], q_norm.shape[1]
    heads = w_uq.shape[2] * N_DEV // head_dim
    nope = w_uk.shape[2] * N_DEV // heads
    vh = w_uv.shape[2] * N_DEV // heads
    groups, gdim = w_pool.shape[1], w_pool.shape[2]
    pool_w = groups * gdim
    assert gdim == LANES and groups == len(POOL_WINDOWS) and head_dim <= LANES and vh <= LANES
    assert d % LANES == 0 and ql % LANES == 0 and kvl % LANES == 0 and seq % HALO == 0
    dm = _Dims(d, ql, kvl, heads, head_dim, nope, vh, pool_w, seq)

    def with_padded(full):
        zpad = jnp.zeros((full["w_in"].shape[0], d, LANES), MXU_DTYPE)
        full["win_p"] = jnp.concatenate(
            [full["w_in"][..., :dm.o_pe], zpad[..., :nope], full["w_in"][..., dm.o_pe:dm.o_pe + dm.rope],
             zpad[..., :LANES - nope - dm.rope], full["w_in"][..., dm.o_pe + dm.rope:]], axis=-1)
        full["wuq_p"] = _pad_heads(full["w_uq"], heads, head_dim)
        full["wuk_p"] = _pad_heads(full["w_uk"], heads, nope)
        return full

    later = [wts[n][1:].astype(MXU_DTYPE) for n in SHARDED] if depth > 1 else []
    first = _all_gather([wts[n][:1].astype(MXU_DTYPE) for n in SHARDED])
    groups = [with_padded({n: _to_full(n, g) for n, g in zip(SHARDED, first)}), None]

    def layer_weights(l):
        group, i = (groups[0], 0) if l == 0 else (groups[1], l - 1)
        return {n: a[i] for n, a in group.items()}

    gq_p, gk_p = _pad_lanes(q_norm, 0), _pad_lanes(k_norm, 0)
    wpool_c = w_pool.astype(MXU_DTYPE)
    rc, rs1, rs2 = _rope_tables(seq, nope, dm.rope)
    row = lambda a, l: a[l][None, :]

    h = x[0]
    saved = []
    for l in range(depth):
        w = layer_weights(l)
        x0 = h
        x1 = _ffn_fwd(x0, row(ffn1_norm, l), w["ffn1_w_gu"], w["ffn1_w_down"])
        q, k, v, zp = _mixin_fwd(dm, x1, row(mix_norm, l), w["win_p"], row(q_lat_norm, l), row(kv_lat_norm, l),
                                 w["wuq_p"], w["wuk_p"], w["w_uv"], row(gq_p, l), row(gk_p, l), rc, rs1, rs2)
        (o, lse), got = _attn_fwd(dm, q, k, v, later if l == 0 else ())
        if l == 0 and later:
            groups[1] = with_padded({n: _to_full(n, g) for n, g in zip(SHARDED, got)})
        x2 = _mixout_fwd(dm, x1, zp, o, w["w_out"], wpool_c[l], row(pool_scale, l))
        h = _ffn_fwd(x2, row(ffn2_norm, l), w["ffn2_w_gu"], w["ffn2_w_down"])
        saved.append((x0, x1, x2, q, k, v, zp, o, lse))

    part, g = _loss_head(h, loss_target[0])
    loss = lax.psum(0.5 / d * jnp.sum(part), ("x", "y", "c"))

    def ffn_grads(grads, w, prefix, l, xin, gout):
        wgu, wd = w[prefix + "_w_gu"], w[prefix + "_w_down"]
        gain = row(given[prefix + "_norm"], l)
        hh, act, dg, du = _ffn_bwd_act(xin, gain, wgu, wd.T, gout)
        gin, dgain = _ffn_bwd_in(xin, gain, wgu.T, dg, du, gout)
        grads[prefix + "_norm"] = dgain[0]
        grads[prefix + "_w_gu"] = jnp.concatenate(
            [_wgrad(hh, dg, 1.0, 1024, 1408), _wgrad(hh, du, 1.0, 1024, 1408)], axis=1)
        grads[prefix + "_w_down"] = _wgrad(act, gout, 0.5, 1408, 1024)
        return gin

    received = [None] * depth
    pending = None
    for l in reversed(range(depth)):
        x0, x1, x2, q, k, v, zp, o, lse = saved[l]
        w = layer_weights(l)
        grads = {}
        g = ffn_grads(grads, w, "ffn2", l, x2, g)
        do, delta, dmixed, dwo, dwpool, dps = _mixout_bwd(dm, g, zp, o, w["w_out"], wpool_c[l], row(pool_scale, l))
        if pending is None:
            (dq, dk, dv), _ = _attn_bwd(dm, q, k, v, do, lse, delta)
        else:
            (dq, dk, dv), received[l + 1] = _attn_bwd(dm, q, k, v, do, lse, delta, *pending)
        dz, dwuq, dwuk, dwuv, dgql, dgkvl, dgq, dgk = _mla_in_bwd(
            dm, x1, row(mix_norm, l), w["win_p"], row(q_lat_norm, l), row(kv_lat_norm, l), w["wuq_p"], w["wuk_p"],
            w["w_uv"], row(gq_p, l), row(gk_p, l), rc, rs1, rs2, dq, dk, dv, dmixed)
        g, dwin, dgmix = _rms_proj_bwd(x1, row(mix_norm, l), w["win_p"], dz, g)
        grads["w_out"] = dwo
        grads["w_pool"] = dwpool
        grads["pool_scale"] = dps[0]
        grads["w_uq"] = _unpad_heads(dwuq, heads, head_dim)
        grads["w_uk"] = _unpad_heads(dwuk, heads, nope)
        grads["w_uv"] = dwuv
        grads["q_lat_norm"] = dgql[0]
        grads["kv_lat_norm"] = dgkvl[0]
        grads["q_norm"] = dgq[0, :head_dim]
        grads["k_norm"] = dgk[0, :head_dim]
        grads["w_in"] = jnp.concatenate(
            [dwin[:, :dm.o_pe], dwin[:, dm.o_pe + nope:dm.o_pe + nope + dm.rope], dwin[:, dm.o_pool:]], axis=1)
        grads["mix_norm"] = dgmix[0]
        g = ffn_grads(grads, w, "ffn1", l, x0, g)
        pending = ([_to_shards(n, grads[n][None])[:, 0].astype(WIRE_DTYPE) for n in SHARDED],
                   [_pack_rows([grads[n] for n in REPLICATED], 8)])
    received[0] = _grad_exchange(*pending)

    outs = [{}, {}, {}, {}]
    for t, n in enumerate(SHARDED):
        nl, a, b = wts[n].shape
        parts = jnp.stack([received[l][t] for l in range(depth)], axis=1)
        results = _adamw(parts.reshape(N_DEV, nl * a, b),
                         *[src[n].reshape(nl * a, b) for src in (wts, mom1, mom2)])
        for out, r in zip(outs, results):
            out[n] = r.reshape(nl, a, b)
    repl_shapes = [wts[n].shape[1:] for n in REPLICATED]
    per_layer = []
    for l in range(depth):
        results = _adamw(received[l][-1],
                         *[_pack_rows([src[n][l] for n in REPLICATED], 8) for src in (wts, mom1, mom2)])
        per_layer.append([_unpack(r, repl_shapes) for r in results])
    for i, out in enumerate(outs):
        for t, n in enumerate(REPLICATED):
            out[n] = jnp.stack([per_layer[l][i][t] for l in range(depth)])

    return (loss, g[None], *[out[n] for out in outs for n in WEIGHTS])
```

```python
import functools

import jax
import jax.numpy as jnp
from jax import lax
from jax.experimental import pallas as pl
from jax.experimental.pallas import tpu as pltpu

F32 = jnp.float32
MXU_DTYPE = jnp.bfloat16
WIRE_DTYPE = jnp.bfloat16
EPS = 1e-6
LOG2E, LN2 = 1.4426950408889634, 0.6931471805599453
ROPE_THETA = 10000.0
POOL_WINDOWS = (2, 4, 8, 16)
ADAM_LR, ADAM_B1, ADAM_B2, ADAM_EPS, ADAM_WD, ADAM_STEP = 0.001, 0.9, 0.999, 1e-08, 0.01, 10

LANES = 128
HALO = 64
PACK_W = 1024
N_DEV = 8
VMEM_LIMIT = 56 * 1024 * 1024
MESH = pl.DeviceIdType.MESH

LOSS_TM = 1024
FFN_FWD_TM, FFN_FWD_TN = 512, 1408
FFN_BWD_TM, FFN_BWD_TN = 512, 1408
FFN_IN_TM = 256
MIX_TM = 256
ATTN_CH = 1024
WG_BK = 2048
ADAM_BLOCK = 128 * 1024

SHARDED = ("ffn1_w_gu", "ffn1_w_down", "w_in", "w_uq", "w_uk", "w_uv", "w_out", "ffn2_w_gu", "ffn2_w_down")
ROW_SHARDED = ("ffn1_w_down", "w_out", "ffn2_w_down")
REPLICATED = ("ffn1_norm", "mix_norm", "q_lat_norm", "kv_lat_norm", "q_norm", "k_norm", "w_pool", "pool_scale",
              "ffn2_norm")
WEIGHTS = ("ffn1_norm", "ffn1_w_gu", "ffn1_w_down", "mix_norm", "w_in", "q_lat_norm", "kv_lat_norm", "w_uq", "w_uk",
           "w_uv", "q_norm", "k_norm", "w_pool", "pool_scale", "w_out", "ffn2_norm", "ffn2_w_gu", "ffn2_w_down")


def _tile(n, pref):
    if n <= pref:
        return n
    t = pref - pref % 8
    while n % t:
        t -= 8
    return t


def _params(*sem):
    return pltpu.CompilerParams(dimension_semantics=sem, vmem_limit_bytes=VMEM_LIMIT)


def _mm(a, b):
    return jnp.dot(a.astype(MXU_DTYPE), b.astype(MXU_DTYPE), preferred_element_type=F32)


def _mm_nt(a, b):
    return lax.dot_general(a.astype(MXU_DTYPE), b.astype(MXU_DTYPE), (((1,), (1,)), ((), ())),
                           preferred_element_type=F32)


def _mm_tn(a, b):
    return lax.dot_general(a.astype(MXU_DTYPE), b.astype(MXU_DTYPE), (((0,), (0,)), ((), ())),
                           preferred_element_type=F32)


def _rstd(x, n):
    return lax.rsqrt(jnp.sum(x * x, axis=-1, keepdims=True) / n + EPS)


def _rms_bwd(dy, xhat, r, gain, n):
    dxh = dy * gain
    return r * (dxh - xhat * (jnp.sum(dxh * xhat, axis=-1, keepdims=True) / n))


def _colsum(x):
    return jnp.sum(x, axis=0, keepdims=True)


def _split3(x):
    hi = x.astype(MXU_DTYPE)
    r1 = x - hi.astype(F32)
    mid = r1.astype(MXU_DTYPE)
    lo = (r1 - mid.astype(F32)).astype(MXU_DTYPE)
    return jnp.concatenate([hi, mid, lo], axis=1)


def _sum3(x):
    n = x.shape[1] // 3
    return (x[:, :n] + x[:, n:2 * n]) + x[:, 2 * n:]


def _pool_mixed(prev, main, nxt, row0, seq):
    tm = main.shape[0]
    k = tm + 2 * HALO
    ext = jnp.concatenate([prev, main, nxt], axis=0)
    s_i = lax.broadcasted_iota(jnp.int32, (tm, k), 0) + row0
    t_j = lax.broadcasted_iota(jnp.int32, (tm, k), 1) + (row0 - HALO)
    s_v = lax.broadcasted_iota(jnp.int32, (tm, 1), 0) + row0
    inside = (t_j >= 0) & (t_j < seq)
    outs = []
    for g, w in enumerate(POOL_WINDOWS):
        left, right = w // 2, w - 1 - w // 2
        band = jnp.where((t_j >= s_i - left) & (t_j <= s_i + right) & inside, 1.0, 0.0).astype(MXU_DTYPE)
        sl = slice(g * LANES, (g + 1) * LANES)
        wsum = _sum3(jnp.dot(band, _split3(ext[:, sl]), preferred_element_type=F32))
        cnt = (jnp.minimum(s_v + right + 1, seq) - jnp.maximum(s_v - left, 0)).astype(F32)
        outs.append(wsum / cnt - main[:, sl])
    return jnp.concatenate(outs, axis=1)


def _pool_mixed_t(prev, main, nxt, row0, seq):
    tm = main.shape[0]
    k = tm + 2 * HALO
    ext = jnp.concatenate([prev, main, nxt], axis=0)
    t_i = lax.broadcasted_iota(jnp.int32, (tm, k), 0) + row0
    s_j = lax.broadcasted_iota(jnp.int32, (tm, k), 1) + (row0 - HALO)
    s_v = lax.broadcasted_iota(jnp.int32, (k, 1), 0) + (row0 - HALO)
    inside = (s_j >= 0) & (s_j < seq)
    inside_v = (s_v >= 0) & (s_v < seq)
    outs = []
    for g, w in enumerate(POOL_WINDOWS):
        left, right = w // 2, w - 1 - w // 2
        band = jnp.where((s_j >= t_i - right) & (s_j <= t_i + left) & inside, 1.0, 0.0).astype(MXU_DTYPE)
        sl = slice(g * LANES, (g + 1) * LANES)
        cnt = (jnp.minimum(s_v + right + 1, seq) - jnp.maximum(s_v - left, 0)).astype(F32)
        scaled = jnp.where(inside_v, ext[:, sl] / jnp.maximum(cnt, 1.0), 0.0)
        outs.append(_sum3(jnp.dot(band, _split3(scaled), preferred_element_type=F32)) - main[:, sl])
    return jnp.concatenate(outs, axis=1)


def _halo_specs(tm, width, seq):
    per = tm // HALO
    last = seq // HALO - 1
    prev = pl.BlockSpec((HALO, width), lambda i: (jnp.maximum(i * per - 1, 0), 0))
    nxt = pl.BlockSpec((HALO, width), lambda i: (jnp.minimum((i + 1) * per, last), 0))
    return prev, nxt


def _rope(x, c, s1, s2, half):
    return x * c + pltpu.roll(x, half, 1) * s1 + pltpu.roll(x, LANES - half, 1) * s2


def _rope_t(d, c, s1, s2, half):
    return d * c + pltpu.roll(d * s1, LANES - half, 1) + pltpu.roll(d * s2, half, 1)


def _rope_tables(seq, nope, rope):
    half = rope // 2
    pos = jnp.arange(seq, dtype=F32)
    inv = ROPE_THETA ** (-jnp.arange(0, rope, 2, dtype=F32) / rope)
    ang = pos[:, None] * inv[None, :]
    cos, sin = jnp.cos(ang), jnp.sin(ang)
    zeros = lambda n: jnp.zeros((seq, n), F32)
    ones = lambda n: jnp.ones((seq, n), F32)
    tail = LANES - nope - rope
    c = jnp.concatenate([ones(nope), cos, cos, ones(tail)], axis=1)
    s1 = jnp.concatenate([zeros(nope + half), sin, zeros(tail)], axis=1)
    s2 = jnp.concatenate([zeros(nope), -sin, zeros(half + tail)], axis=1)
    return c, s1, s2


def _ffn_fwd(x, gain, wgu, wd):
    seq, d = x.shape
    f = wd.shape[0]
    tm, tn = _tile(seq, FFN_FWD_TM), _tile(f, FFN_FWD_TN)
    nk = f // tn

    def body(x_ref, g_ref, wg_ref, wu_ref, wd_ref, o_ref, h_sc, acc_sc):
        k = pl.program_id(1)

        @pl.when(k == 0)
        def _():
            xv = x_ref[...]
            h_sc[...] = (xv * _rstd(xv, d) * g_ref[...]).astype(MXU_DTYPE)
            acc_sc[...] = jnp.zeros_like(acc_sc)

        h = h_sc[...]
        g = jnp.dot(h, wg_ref[...], preferred_element_type=F32)
        u = jnp.dot(h, wu_ref[...], preferred_element_type=F32)
        a = g * (1.0 / (1.0 + jnp.exp(-g))) * u
        acc_sc[...] += _mm(a, wd_ref[...])

        @pl.when(k == nk - 1)
        def _():
            o_ref[...] = x_ref[...] + 0.5 * acc_sc[...]

    return pl.pallas_call(
        body, name="ffn_fwd", grid=(seq // tm, nk),
        in_specs=[pl.BlockSpec((tm, d), lambda i, k: (i, 0)),
                  pl.BlockSpec((1, d), lambda i, k: (0, 0)),
                  pl.BlockSpec((d, tn), lambda i, k: (0, k)),
                  pl.BlockSpec((d, tn), lambda i, k: (0, k + nk)),
                  pl.BlockSpec((tn, d), lambda i, k: (k, 0))],
        out_specs=pl.BlockSpec((tm, d), lambda i, k: (i, 0)),
        out_shape=jax.ShapeDtypeStruct((seq, d), F32),
        scratch_shapes=[pltpu.VMEM((tm, d), MXU_DTYPE), pltpu.VMEM((tm, d), F32)],
        compiler_params=_params("parallel", "arbitrary"),
    )(x, gain, wgu, wgu, wd)


def _ffn_bwd_act(x, gain, wgu, wdt, dout):
    seq, d = x.shape
    f = wdt.shape[1]
    tm, tn = _tile(seq, FFN_BWD_TM), _tile(f, FFN_BWD_TN)
    nk = f // tn

    def body(x_ref, g_ref, wg_ref, wu_ref, wdt_ref, do_ref, h_ref, act_ref, dg_ref, du_ref, dy_sc):
        @pl.when(pl.program_id(1) == 0)
        def _():
            xv = x_ref[...]
            h_ref[...] = (xv * _rstd(xv, d) * g_ref[...]).astype(MXU_DTYPE)
            dy_sc[...] = (0.5 * do_ref[...]).astype(MXU_DTYPE)

        h = h_ref[...]
        g = jnp.dot(h, wg_ref[...], preferred_element_type=F32)
        u = jnp.dot(h, wu_ref[...], preferred_element_type=F32)
        sig = 1.0 / (1.0 + jnp.exp(-g))
        silu = g * sig
        act_ref[...] = (silu * u).astype(MXU_DTYPE)
        da = jnp.dot(dy_sc[...], wdt_ref[...], preferred_element_type=F32)
        du_ref[...] = (da * silu).astype(MXU_DTYPE)
        dg_ref[...] = (da * u * (sig * (1.0 + g * (1.0 - sig)))).astype(MXU_DTYPE)

    row = lambda i, k: (i, 0)
    col = lambda i, k: (i, k)
    return pl.pallas_call(
        body, name="ffn_bwd_act", grid=(seq // tm, nk),
        in_specs=[pl.BlockSpec((tm, d), row),
                  pl.BlockSpec((1, d), lambda i, k: (0, 0)),
                  pl.BlockSpec((d, tn), lambda i, k: (0, k)),
                  pl.BlockSpec((d, tn), lambda i, k: (0, k + nk)),
                  pl.BlockSpec((d, tn), lambda i, k: (0, k)),
                  pl.BlockSpec((tm, d), row)],
        out_specs=[pl.BlockSpec((tm, d), row), pl.BlockSpec((tm, tn), col), pl.BlockSpec((tm, tn), col),
                   pl.BlockSpec((tm, tn), col)],
        out_shape=[jax.ShapeDtypeStruct((seq, d), MXU_DTYPE), jax.ShapeDtypeStruct((seq, f), MXU_DTYPE),
                   jax.ShapeDtypeStruct((seq, f), MXU_DTYPE), jax.ShapeDtypeStruct((seq, f), MXU_DTYPE)],
        scratch_shapes=[pltpu.VMEM((tm, d), MXU_DTYPE)],
        compiler_params=_params("parallel", "arbitrary"),
    )(x, gain, wgu, wgu, wdt, dout)


def _ffn_bwd_in(x, gain, wgut, dg, du, dout):
    seq, d = x.shape
    f = dg.shape[1]
    tm = _tile(seq, FFN_IN_TM)

    def body(x_ref, g_ref, wgut_ref, dg_ref, du_ref, do_ref, dx_ref, dgain_ref):
        @pl.when(pl.program_id(0) == 0)
        def _():
            dgain_ref[...] = jnp.zeros_like(dgain_ref)

        dh = (jnp.dot(dg_ref[...], wgut_ref[:f, :], preferred_element_type=F32)
              + jnp.dot(du_ref[...], wgut_ref[f:, :], preferred_element_type=F32))
        xv = x_ref[...]
        r = _rstd(xv, d)
        xhat = xv * r
        dgain_ref[...] += _colsum(dh * xhat)
        dx_ref[...] = do_ref[...] + _rms_bwd(dh, xhat, r, g_ref[...], d)

    row = lambda i: (i, 0)
    return pl.pallas_call(
        body, name="ffn_bwd_in", grid=(seq // tm,),
        in_specs=[pl.BlockSpec((tm, d), row), pl.BlockSpec((1, d), lambda i: (0, 0)),
                  pl.BlockSpec((2 * f, d), lambda i: (0, 0)), pl.BlockSpec((tm, f), row), pl.BlockSpec((tm, f), row),
                  pl.BlockSpec((tm, d), row)],
        out_specs=[pl.BlockSpec((tm, d), row), pl.BlockSpec((1, d), lambda i: (0, 0))],
        out_shape=[jax.ShapeDtypeStruct((seq, d), F32), jax.ShapeDtypeStruct((1, d), F32)],
        compiler_params=_params("arbitrary"),
    )(x, gain, wgut, dg, du, dout)


def _wgrad(a, b, scale, bm, bn):
    seq, m = a.shape
    n = b.shape[1]
    bm, bn, bk = _tile(m, bm), _tile(n, bn), _tile(seq, WG_BK)
    ns = seq // bk

    def body(a_ref, b_ref, o_ref):
        @pl.when(pl.program_id(2) == 0)
        def _():
            o_ref[...] = jnp.zeros_like(o_ref)

        o_ref[...] += scale * _mm_tn(a_ref[...], b_ref[...])

    return pl.pallas_call(
        body, name="wgrad", grid=(m // bm, n // bn, ns),
        in_specs=[pl.BlockSpec((bk, bm), lambda i, j, s: (s, i)),
                  pl.BlockSpec((bk, bn), lambda i, j, s: (s, j))],
        out_specs=pl.BlockSpec((bm, bn), lambda i, j, s: (i, j)),
        out_shape=jax.ShapeDtypeStruct((m, n), F32),
        compiler_params=_params("parallel", "parallel", "arbitrary"),
    )(a, b)


class _Dims:
    def __init__(self, d, ql, kvl, heads, head_dim, nope, vh, pool_w, seq):
        self.d, self.ql, self.kvl, self.heads, self.head_dim, self.nope, self.vh, self.pool_w = (
            d, ql, kvl, heads, head_dim, nope, vh, pool_w)
        self.rope = head_dim - nope
        self.half = self.rope // 2
        self.hw = heads * LANES
        self.vw = heads * vh
        self.ch = _tile(seq, ATTN_CH)
        self.nch = seq // self.ch
        self.o_kv = ql
        self.o_pe = ql + kvl
        self.o_pool = ql + kvl + LANES
        self.zw = self.o_pool + pool_w
        self.scale = head_dim ** -0.5


def _chunk_spec(dm, rows, tm):
    per = dm.ch // tm
    return pl.BlockSpec((dm.heads, 1, rows, tm), lambda i: (0, i // per, 0, i % per))


def _mixin_fwd(dm, x, gmix, win, gql, gkvl, wuq, wuk, wuv, gq, gk, rc, rs1, rs2):
    seq, d = x.shape
    tm = _tile(seq, MIX_TM)

    def body(x_ref, gmix_ref, win_ref, gql_ref, gkvl_ref, wuq_ref, wuk_ref, wuv_ref, gq_ref, gk_ref,
             rc_ref, rs1_ref, rs2_ref, q_ref, k_ref, v_ref, zp_ref):
        xv = x_ref[...]
        z = _mm(xv * _rstd(xv, d) * gmix_ref[...], win_ref[...])
        cq, ckv = z[:, :dm.o_kv], z[:, dm.o_kv:dm.o_pe]
        kpe = z[:, dm.o_pe:dm.o_pool]
        zp_ref[...] = z[:, dm.o_pool:]
        cqn = cq * _rstd(cq, dm.ql) * gql_ref[...]
        ckvn = ckv * _rstd(ckv, dm.kvl) * gkvl_ref[...]
        q = _mm(cqn, wuq_ref[...])
        kn = _mm(ckvn, wuk_ref[...])
        v_ref[...] = _mm(ckvn, wuv_ref[...]).T.reshape(dm.heads, 1, dm.vh, tm).astype(MXU_DTYPE)
        c, s1, s2 = rc_ref[...], rs1_ref[...], rs2_ref[...]
        for h in range(dm.heads):
            sl = slice(h * LANES, (h + 1) * LANES)
            qh = q[:, sl]
            qn = qh * _rstd(qh, dm.head_dim) * gq_ref[...]
            q_ref[:, sl] = (_rope(qn, c, s1, s2, dm.half) * dm.scale * LOG2E).astype(MXU_DTYPE)
            kh = kn[:, sl] + kpe
            kk = kh * _rstd(kh, dm.head_dim) * gk_ref[...]
            k_ref[:, sl] = _rope(kk, c, s1, s2, dm.half).astype(MXU_DTYPE)

    row = lambda i: (i, 0)
    full = lambda a: pl.BlockSpec(a.shape, lambda i: (0,) * a.ndim)
    return pl.pallas_call(
        body, name="mixin_fwd", grid=(seq // tm,),
        in_specs=[pl.BlockSpec((tm, d), row), full(gmix), full(win), full(gql), full(gkvl), full(wuq), full(wuk),
                  full(wuv), full(gq), full(gk),
                  pl.BlockSpec((tm, LANES), row), pl.BlockSpec((tm, LANES), row), pl.BlockSpec((tm, LANES), row)],
        out_specs=[pl.BlockSpec((tm, dm.hw), row), pl.BlockSpec((tm, dm.hw), row), _chunk_spec(dm, dm.vh, tm),
                   pl.BlockSpec((tm, dm.pool_w), row)],
        out_shape=[jax.ShapeDtypeStruct((seq, dm.hw), MXU_DTYPE),
                   jax.ShapeDtypeStruct((seq, dm.hw), MXU_DTYPE),
                   jax.ShapeDtypeStruct((dm.heads, dm.nch, dm.vh, dm.ch), MXU_DTYPE),
                   jax.ShapeDtypeStruct((seq, dm.pool_w), F32)],
        compiler_params=_params("parallel"),
    )(x, gmix, win, gql, gkvl, wuq, wuk, wuv, gq, gk, rc, rs1, rs2)


def _mla_in_bwd(dm, x, gmix, win, gql, gkvl, wuq, wuk, wuv, gq, gk, rc, rs1, rs2, dq, dk, dv, dmixed):
    seq, d = x.shape
    tm = _tile(seq, MIX_TM)

    def body(x_ref, gmix_ref, win_ref, gql_ref, gkvl_ref, wuq_ref, wuk_ref, wuv_ref, gq_ref, gk_ref,
             rc_ref, rs1_ref, rs2_ref, dq_ref, dk_ref, dv_ref, dmp_ref, dm_ref, dmn_ref,
             dz_ref, dwuq_ref, dwuk_ref, dwuv_ref, dgql_ref, dgkvl_ref, dgq_ref, dgk_ref):
        i = pl.program_id(0)

        @pl.when(i == 0)
        def _():
            for ref in (dwuq_ref, dwuk_ref, dwuv_ref, dgql_ref, dgkvl_ref, dgq_ref, dgk_ref):
                ref[...] = jnp.zeros_like(ref)

        xv = x_ref[...]
        z = _mm(xv * _rstd(xv, d) * gmix_ref[...], win_ref[...])
        cq, ckv = z[:, :dm.o_kv], z[:, dm.o_kv:dm.o_pe]
        kpe = z[:, dm.o_pe:dm.o_pool]
        r_q, r_kv = _rstd(cq, dm.ql), _rstd(ckv, dm.kvl)
        cqh, ckvh = cq * r_q, ckv * r_kv
        cqn = (cqh * gql_ref[...]).astype(MXU_DTYPE)
        ckvn = (ckvh * gkvl_ref[...]).astype(MXU_DTYPE)
        q = _mm(cqn, wuq_ref[...])
        kn = _mm(ckvn, wuk_ref[...])
        c, s1, s2 = rc_ref[...], rs1_ref[...], rs2_ref[...]
        dqv = dq_ref[...].reshape(dm.hw, tm).T
        dq_pre, dk_pre = [], []
        dkpe = jnp.zeros((tm, LANES), F32)
        dgq = jnp.zeros((1, LANES), F32)
        dgk = jnp.zeros((1, LANES), F32)
        for h in range(dm.heads):
            sl = slice(h * LANES, (h + 1) * LANES)
            qh = q[:, sl]
            rq = _rstd(qh, dm.head_dim)
            xq = qh * rq
            dqn = _rope_t(dqv[:, sl] * dm.scale, c, s1, s2, dm.half)
            dgq += _colsum(dqn * xq)
            dq_pre.append(_rms_bwd(dqn, xq, rq, gq_ref[...], dm.head_dim))
            kh = kn[:, sl] + kpe
            rk = _rstd(kh, dm.head_dim)
            xk = kh * rk
            dkn = _rope_t(dk_ref[:, sl], c, s1, s2, dm.half)
            dgk += _colsum(dkn * xk)
            dkh = _rms_bwd(dkn, xk, rk, gk_ref[...], dm.head_dim)
            dk_pre.append(dkh)
            dkpe += dkh
        dgq_ref[...] += dgq
        dgk_ref[...] += dgk
        dq_pre = jnp.concatenate(dq_pre, axis=1).astype(MXU_DTYPE)
        dk_pre = jnp.concatenate(dk_pre, axis=1).astype(MXU_DTYPE)
        dvv = dv_ref[...].reshape(dm.vw, tm).T.astype(MXU_DTYPE)
        dwuq_ref[...] += _mm_tn(cqn, dq_pre)
        dwuk_ref[...] += _mm_tn(ckvn, dk_pre)
        dwuv_ref[...] += _mm_tn(ckvn, dvv)
        dcqn = _mm_nt(dq_pre, wuq_ref[...])
        dckvn = _mm_nt(dk_pre, wuk_ref[...]) + _mm_nt(dvv, wuv_ref[...])
        dgql_ref[...] += _colsum(dcqn * cqh)
        dgkvl_ref[...] += _colsum(dckvn * ckvh)
        dcq = _rms_bwd(dcqn, cqh, r_q, gql_ref[...], dm.ql)
        dckv = _rms_bwd(dckvn, ckvh, r_kv, gkvl_ref[...], dm.kvl)
        dzp = _pool_mixed_t(dmp_ref[...], dm_ref[...], dmn_ref[...], i * tm, seq)
        dz_ref[...] = jnp.concatenate([dcq, dckv, dkpe, dzp], axis=1).astype(MXU_DTYPE)

    row = lambda i: (i, 0)
    full = lambda a: pl.BlockSpec(a.shape, lambda i: (0,) * a.ndim)
    acc = lambda shape: pl.BlockSpec(shape, lambda i: (0, 0))
    prev, nxt = _halo_specs(tm, dm.pool_w, seq)
    shapes = [(seq, dm.zw), wuq.shape, wuk.shape, wuv.shape, (1, dm.ql), (1, dm.kvl), (1, LANES), (1, LANES)]
    return pl.pallas_call(
        body, name="mla_in_bwd", grid=(seq // tm,),
        in_specs=[pl.BlockSpec((tm, d), row), full(gmix), full(win), full(gql), full(gkvl), full(wuq), full(wuk),
                  full(wuv), full(gq), full(gk),
                  pl.BlockSpec((tm, LANES), row), pl.BlockSpec((tm, LANES), row), pl.BlockSpec((tm, LANES), row),
                  _chunk_spec(dm, LANES, tm), pl.BlockSpec((tm, dm.hw), row), _chunk_spec(dm, dm.vh, tm),
                  prev, pl.BlockSpec((tm, dm.pool_w), row), nxt],
        out_specs=[pl.BlockSpec((tm, dm.zw), row)] + [acc(s) for s in shapes[1:]],
        out_shape=[jax.ShapeDtypeStruct(shapes[0], MXU_DTYPE)] + [jax.ShapeDtypeStruct(s, F32) for s in shapes[1:]],
        compiler_params=_params("arbitrary"),
    )(x, gmix, win, gql, gkvl, wuq, wuk, wuv, gq, gk, rc, rs1, rs2, dq, dk, dv, dmixed, dmixed, dmixed)


def _rms_proj_bwd(x, gain, w, dz, gin):
    seq, d = x.shape
    n = w.shape[1]
    tm = _tile(seq, MIX_TM)

    def body(x_ref, g_ref, w_ref, dz_ref, gin_ref, gout_ref, dw_ref, dgain_ref):
        @pl.when(pl.program_id(0) == 0)
        def _():
            dw_ref[...] = jnp.zeros_like(dw_ref)
            dgain_ref[...] = jnp.zeros_like(dgain_ref)

        xv = x_ref[...]
        r = _rstd(xv, d)
        xhat = xv * r
        dzv = dz_ref[...]
        dh = _mm_nt(dzv, w_ref[...])
        dw_ref[...] += _mm_tn(xhat * g_ref[...], dzv)
        dgain_ref[...] += _colsum(dh * xhat)
        gout_ref[...] = gin_ref[...] + _rms_bwd(dh, xhat, r, g_ref[...], d)

    row = lambda i: (i, 0)
    return pl.pallas_call(
        body, name="rms_proj_bwd", grid=(seq // tm,),
        in_specs=[pl.BlockSpec((tm, d), row), pl.BlockSpec((1, d), lambda i: (0, 0)),
                  pl.BlockSpec((d, n), lambda i: (0, 0)), pl.BlockSpec((tm, n), row), pl.BlockSpec((tm, d), row)],
        out_specs=[pl.BlockSpec((tm, d), row), pl.BlockSpec((d, n), lambda i: (0, 0)),
                   pl.BlockSpec((1, d), lambda i: (0, 0))],
        out_shape=[jax.ShapeDtypeStruct((seq, d), F32), jax.ShapeDtypeStruct((d, n), F32),
                   jax.ShapeDtypeStruct((1, d), F32)],
        compiler_params=_params("arbitrary"),
    )(x, gain, w, dz, gin)


def _pool_branch(mixed, wpool_ref):
    groups = mixed.shape[1] // LANES
    return jnp.concatenate(
        [_mm(mixed[:, g * LANES:(g + 1) * LANES], wpool_ref[g]) for g in range(groups)], axis=1)


def _mixout_fwd(dm, x, zp, ot, wo, wpool, pscale):
    seq, d = x.shape
    tm = _tile(seq, MIX_TM)

    def body(x_ref, zpp_ref, zp_ref, zpn_ref, ot_ref, wo_ref, wpool_ref, ps_ref, out_ref):
        mixed = _pool_mixed(zpp_ref[...], zp_ref[...], zpn_ref[...], pl.program_id(0) * tm, seq)
        b = _pool_branch(mixed, wpool_ref) * ps_ref[...]
        a = _mm_tn(ot_ref[...].reshape(dm.vw, tm), wo_ref[:dm.vw, :])
        out_ref[...] = x_ref[...] + a + _mm(b, wo_ref[dm.vw:, :])

    row = lambda i: (i, 0)
    full = lambda a: pl.BlockSpec(a.shape, lambda i: (0,) * a.ndim)
    prev, nxt = _halo_specs(tm, dm.pool_w, seq)
    return pl.pallas_call(
        body, name="mixout_fwd", grid=(seq // tm,),
        in_specs=[pl.BlockSpec((tm, d), row), prev, pl.BlockSpec((tm, dm.pool_w), row), nxt,
                  _chunk_spec(dm, dm.vh, tm), full(wo), full(wpool), full(pscale)],
        out_specs=pl.BlockSpec((tm, d), row),
        out_shape=jax.ShapeDtypeStruct((seq, d), F32),
        compiler_params=_params("parallel"),
    )(x, zp, zp, zp, ot, wo, wpool, pscale)


def _mixout_bwd(dm, g, zp, ot, wo, wpool, pscale):
    seq, d = g.shape
    tm = _tile(seq, MIX_TM)
    groups = dm.pool_w // LANES

    def body(g_ref, zpp_ref, zp_ref, zpn_ref, ot_ref, wo_ref, wpool_ref, ps_ref,
             dot_ref, delta_ref, dmixed_ref, dwo_ref, dwpool_ref, dps_ref):
        i = pl.program_id(0)

        @pl.when(i == 0)
        def _():
            for ref in (dwo_ref, dwpool_ref, dps_ref):
                ref[...] = jnp.zeros_like(ref)

        gv = g_ref[...].astype(MXU_DTYPE)
        otv = ot_ref[...].reshape(dm.vw, tm)
        dat = _mm_nt(wo_ref[:dm.vw, :], gv)
        db = _mm_nt(gv, wo_ref[dm.vw:, :])
        mixed = _pool_mixed(zpp_ref[...], zp_ref[...], zpn_ref[...], i * tm, seq).astype(MXU_DTYPE)
        y = _pool_branch(mixed, wpool_ref)
        b = (y * ps_ref[...]).astype(MXU_DTYPE)
        dwo_ref[:dm.vw, :] += _mm(otv, gv)
        dwo_ref[dm.vw:, :] += _mm_tn(b, gv)
        dps_ref[...] += _colsum(db * y)
        dy = (db * ps_ref[...]).astype(MXU_DTYPE)
        dmx = []
        for gi in range(groups):
            sl = slice(gi * LANES, (gi + 1) * LANES)
            dmx.append(_mm_nt(dy[:, sl], wpool_ref[gi]))
            dwpool_ref[gi] += _mm_tn(mixed[:, sl], dy[:, sl])
        dmixed_ref[...] = jnp.concatenate(dmx, axis=1)
        dot_ref[...] = dat.reshape(dm.heads, 1, dm.vh, tm).astype(MXU_DTYPE)
        prod = dat * otv.astype(F32)
        for h in range(dm.heads):
            delta_ref[h, 0] = jnp.broadcast_to(_colsum(prod[h * dm.vh:(h + 1) * dm.vh]), (8, tm))

    row = lambda i: (i, 0)
    full = lambda a: pl.BlockSpec(a.shape, lambda i: (0,) * a.ndim)
    prev, nxt = _halo_specs(tm, dm.pool_w, seq)
    return pl.pallas_call(
        body, name="mixout_bwd", grid=(seq // tm,),
        in_specs=[pl.BlockSpec((tm, d), row), prev, pl.BlockSpec((tm, dm.pool_w), row), nxt,
                  _chunk_spec(dm, dm.vh, tm), full(wo), full(wpool), full(pscale)],
        out_specs=[_chunk_spec(dm, dm.vh, tm), _chunk_spec(dm, 8, tm), pl.BlockSpec((tm, dm.pool_w), row),
                   full(wo), full(wpool), full(pscale)],
        out_shape=[jax.ShapeDtypeStruct((dm.heads, dm.nch, dm.vh, dm.ch), MXU_DTYPE),
                   jax.ShapeDtypeStruct((dm.heads, dm.nch, 8, dm.ch), F32),
                   jax.ShapeDtypeStruct((seq, dm.pool_w), F32), jax.ShapeDtypeStruct(wo.shape, F32),
                   jax.ShapeDtypeStruct(wpool.shape, F32), jax.ShapeDtypeStruct(pscale.shape, F32)],
        compiler_params=_params("arbitrary"),
    )(g, zp, zp, zp, ot, wo, wpool, pscale)


def _attn_fwd(dm, q, k, vt, common=()):
    seq, ch, nch = k.shape[0], dm.ch, dm.nch
    kc = ch // 2
    n_ride = len(common)

    def body(*refs):
        q_ref, k_ref, vt_ref = refs[:3]
        ride_in = refs[3:3 + n_ride]
        ot_ref, lse_ref = refs[3 + n_ride:5 + n_ride]
        ride_out = refs[5 + n_ride:5 + 2 * n_ride]
        m_sc, l_sc, acc_sc, st0, st1, pt0, pt1 = refs[5 + 2 * n_ride:12 + 2 * n_ride]
        sems = refs[12 + 2 * n_ride:]
        first = (pl.program_id(0) == 0) & (pl.program_id(1) == 0)
        last = (pl.program_id(0) == dm.heads - 1) & (pl.program_id(1) == nch - 1)

        if n_ride:
            @pl.when(first)
            def _():
                for cp in _exchange_copies((), ride_in, ride_out, *sems):
                    cp.start()

        m_sc[...] = jnp.full_like(m_sc, -jnp.inf)
        l_sc[...] = jnp.zeros_like(l_sc)
        acc_sc[...] = jnp.zeros_like(acc_sc)
        qv = q_ref[...]

        def scores(row):
            return _mm_nt(k_ref[pl.ds(pl.multiple_of(row, kc), kc), :], qv)

        def stage(next_row, vt_prev, st_cur, st_nxt, pt_cur, pt_prev):
            st_nxt[...] = scores(next_row)
            acc = acc_sc[...] + _mm(vt_prev, pt_prev[...])
            st = st_cur[...]
            m_prev = m_sc[...]
            m_new = jnp.maximum(m_prev, jnp.max(st, axis=0, keepdims=True))
            alpha = jnp.exp2(m_prev - m_new)
            pt = jnp.exp2(st - m_new)
            pt_cur[...] = pt.astype(MXU_DTYPE)
            l_sc[...] = alpha * l_sc[...] + _colsum(pt)
            acc_sc[...] = alpha * acc
            m_sc[...] = m_new

        st0[...] = scores(0)
        pt1[...] = jnp.zeros_like(pt1)

        def pair(jj, carry):
            stage(jj * ch + kc, vt_ref[0, jnp.maximum(jj - 1, 0), :, kc:], st0, st1, pt0, pt1)
            stage(jnp.minimum(jj + 1, nch - 1) * ch, vt_ref[0, jj, :, :kc], st1, st0, pt1, pt0)
            return carry

        lax.fori_loop(0, nch, pair, 0)
        acc = acc_sc[...] + _mm(vt_ref[0, nch - 1, :, kc:], pt1[...])
        ot_ref[0, 0] = (acc / l_sc[...]).astype(MXU_DTYPE)
        lse_ref[0, 0] = jnp.broadcast_to(m_sc[...] + jnp.log(l_sc[...]) * LOG2E, (8, ch))

        if n_ride:
            @pl.when(last)
            def _():
                _exchange_wait(_exchange_copies((), ride_in, ride_out, *sems))

    assert nch % 2 == 0
    chunk = lambda rows: pl.BlockSpec((1, 1, rows, ch), lambda h, i: (h, i, 0, 0))
    anywhere = [pl.BlockSpec(memory_space=pl.ANY)] * n_ride
    out = pl.pallas_call(
        body, name="attn_fwd_gather" if n_ride else "attn_fwd", grid=(dm.heads, nch),
        in_specs=[pl.BlockSpec((ch, LANES), lambda h, i: (i, h)),
                  pl.BlockSpec((seq, LANES), lambda h, i: (0, h)),
                  pl.BlockSpec((1, nch, dm.vh, ch), lambda h, i: (h, 0, 0, 0))] + anywhere,
        out_specs=[chunk(dm.vh), chunk(8)] + anywhere,
        out_shape=[jax.ShapeDtypeStruct((dm.heads, nch, dm.vh, ch), MXU_DTYPE),
                   jax.ShapeDtypeStruct((dm.heads, nch, 8, ch), F32)] + _exchange_out_shapes((), common),
        scratch_shapes=[pltpu.VMEM((1, ch), F32), pltpu.VMEM((1, ch), F32), pltpu.VMEM((dm.vh, ch), F32),
                        pltpu.VMEM((kc, ch), F32), pltpu.VMEM((kc, ch), F32),
                        pltpu.VMEM((kc, ch), MXU_DTYPE), pltpu.VMEM((kc, ch), MXU_DTYPE)]
        + (_exchange_sems(n_ride) if n_ride else []),
        compiler_params=_params("arbitrary", "arbitrary") if n_ride else _params("parallel", "parallel"),
    )(q, k, vt, *common)
    return out[:2], out[2:]


def _attn_bwd(dm, q, k, vt, dot, lse, delta, blocks=(), common=()):
    seq, ch, nch = q.shape[0], dm.ch, dm.nch
    n_ride = len(blocks) + len(common)

    def body(*refs):
        k_ref, vt_ref, q_ref, dot_ref, lse_ref, delta_ref = refs[:6]
        ride_in = refs[6:6 + n_ride]
        dqt_ref, dk_ref, dvt_ref = refs[6 + n_ride:9 + n_ride]
        ride_out = refs[9 + n_ride:9 + 2 * n_ride]
        h, j = pl.program_id(0), pl.program_id(1)

        if n_ride:
            sems = refs[9 + 2 * n_ride:]

            @pl.when((h == 0) & (j == 0))
            def _():
                for cp in _exchange_copies(ride_in[:len(blocks)], ride_in[len(blocks):], ride_out, *sems):
                    cp.start()

        kv = k_ref[...]
        kt = kv.astype(F32).T.astype(MXU_DTYPE)
        vtv = vt_ref[0, 0]
        dk_ref[...] = jnp.zeros_like(dk_ref)
        dvt_ref[...] = jnp.zeros_like(dvt_ref)

        @pl.when(j == 0)
        def _():
            dqt_ref[...] = jnp.zeros_like(dqt_ref)

        def stage(i, carry):
            qi = q_ref[pl.ds(pl.multiple_of(i * ch, ch), ch), :]
            doti = dot_ref[0, i]
            pt = jnp.exp2(_mm_nt(kv, qi) - lse_ref[0, i][:1]).astype(MXU_DTYPE)
            dst = pt * (_mm_tn(vtv, doti) - delta_ref[0, i][:1]).astype(MXU_DTYPE)
            dvt_ref[0, 0] += _mm_nt(doti, pt)
            dk_ref[...] += _mm(dst, qi)
            dqt_ref[0, i] += _mm(kt, dst)
            return carry

        lax.fori_loop(0, nch, stage, 0)
        dk_ref[...] = dk_ref[...] * LN2

        if n_ride:
            @pl.when((h == dm.heads - 1) & (j == nch - 1))
            def _():
                copies = _exchange_copies(ride_in[:len(blocks)], ride_in[len(blocks):], ride_out, *sems)
                _exchange_wait(copies)

    whole = lambda rows: pl.BlockSpec((1, nch, rows, ch), lambda h, j: (h, 0, 0, 0))
    chunk = lambda rows: pl.BlockSpec((1, 1, rows, ch), lambda h, j: (h, j, 0, 0))
    anywhere = [pl.BlockSpec(memory_space=pl.ANY)] * n_ride
    out = pl.pallas_call(
        body, name="attn_bwd_exchange" if n_ride else "attn_bwd", grid=(dm.heads, nch),
        in_specs=[pl.BlockSpec((ch, LANES), lambda h, j: (j, h)), chunk(dm.vh),
                  pl.BlockSpec((seq, LANES), lambda h, j: (0, h)), whole(dm.vh), whole(8), whole(8)] + anywhere,
        out_specs=[whole(LANES), pl.BlockSpec((ch, LANES), lambda h, j: (j, h)), chunk(dm.vh)] + anywhere,
        out_shape=[jax.ShapeDtypeStruct((dm.heads, nch, LANES, ch), F32), jax.ShapeDtypeStruct(q.shape, F32),
                   jax.ShapeDtypeStruct((dm.heads, nch, dm.vh, ch), F32)] + _exchange_out_shapes(blocks, common),
        scratch_shapes=_exchange_sems(n_ride) if n_ride else [],
        compiler_params=_params("arbitrary", "arbitrary"),
    )(k, vt, q, dot, lse, delta, *blocks, *common)
    return out[:3], out[3:]


def _loss_head(y, target):
    seq, d = y.shape
    tm = _tile(seq, LOSS_TM)

    def body(y_ref, t_ref, part_ref, dy_ref):
        @pl.when(pl.program_id(0) == 0)
        def _():
            part_ref[...] = jnp.zeros_like(part_ref)

        err = y_ref[...] - t_ref[...]
        part_ref[...] += _colsum(err * err)
        dy_ref[...] = err / d

    row = lambda i: (i, 0)
    return pl.pallas_call(
        body, name="loss_head", grid=(seq // tm,),
        in_specs=[pl.BlockSpec((tm, d), row), pl.BlockSpec((tm, d), row)],
        out_specs=[pl.BlockSpec((1, d), lambda i: (0, 0)), pl.BlockSpec((tm, d), row)],
        out_shape=[jax.ShapeDtypeStruct((1, d), F32), jax.ShapeDtypeStruct((seq, d), F32)],
        compiler_params=_params("arbitrary"),
    )(y, target)


def _my_place():
    return lax.axis_index("x"), lax.axis_index("y"), lax.axis_index("c")


def _all_gather(shards):
    n = len(shards)

    def body(*refs):
        x_refs, out_refs = refs[:n], refs[n:2 * n]
        send_sems, recv_sems, local_sems = refs[2 * n:]
        x, y, c = _my_place()
        me, sibling = (x, y, c), (x, y, 1 - c)
        chips = [(1 - x, y), (x, 1 - y), (1 - x, 1 - y)]

        def slot(t, px, py, pc):
            return out_refs[t].at[4 * px + 2 * py + pc]

        def copy(t, k, block, to, src=None):
            return pltpu.make_async_remote_copy(
                src_ref=slot(t, *block) if src is None else src, dst_ref=slot(t, *block),
                send_sem=send_sems.at[t, k], recv_sem=recv_sems.at[t, k], device_id=to, device_id_type=MESH)

        mine = [pltpu.make_async_copy(x_refs[t], slot(t, *me), local_sems.at[t]) for t in range(n)]
        started = []
        for t in range(n):
            mine[t].start()
            first = [copy(t, 0, me, sibling, src=x_refs[t])]
            first += [copy(t, 1 + j, me, (*chip, c), src=x_refs[t]) for j, chip in enumerate(chips)]
            for cp in first:
                cp.start()
            started += first
        for j, chip in enumerate(chips):
            for t in range(n):
                copy(t, 1 + j, (*chip, c), me).wait_recv()
                passed = copy(t, 4 + j, (*chip, c), sibling)
                passed.start()
                started.append(passed)
        for t in range(n):
            copy(t, 0, sibling, me).wait_recv()
            for j, chip in enumerate(chips):
                copy(t, 4 + j, (*chip, 1 - c), me).wait_recv()
        for cp in started:
            cp.wait_send()
        for cp in mine:
            cp.wait()

    return pl.pallas_call(
        body, name="weights_all_gather",
        out_shape=[jax.ShapeDtypeStruct((N_DEV,) + s.shape, s.dtype) for s in shards],
        in_specs=[pl.BlockSpec(memory_space=pl.ANY)] * n,
        out_specs=[pl.BlockSpec(memory_space=pl.ANY)] * n,
        scratch_shapes=[pltpu.SemaphoreType.DMA((n, 7)), pltpu.SemaphoreType.DMA((n, 7)),
                        pltpu.SemaphoreType.DMA((n,))],
    )(*shards)


def _exchange_copies(g_refs, c_refs, out_refs, send_sems, recv_sems, local_sems):
    x, y, c = _my_place()
    me = 4 * x + 2 * y + c
    srcs = [lambda dev, r=r: r.at[dev] for r in g_refs] + [lambda dev, r=r: r for r in c_refs]
    copies = [pltpu.make_async_copy(srcs[t](me), out_refs[t].at[me], local_sems.at[t]) for t in range(len(srcs))]
    for k in range(1, N_DEV):
        px = 1 - x if k & 4 else x
        py = 1 - y if k & 2 else y
        pc = 1 - c if k & 1 else c
        for t in range(len(srcs)):
            copies.append(pltpu.make_async_remote_copy(
                src_ref=srcs[t](4 * px + 2 * py + pc), dst_ref=out_refs[t].at[me],
                send_sem=send_sems.at[t, k - 1], recv_sem=recv_sems.at[t, k - 1],
                device_id=(px, py, pc), device_id_type=MESH))
    return copies


def _exchange_wait(copies):
    n_local = len(copies) // N_DEV
    for cp in copies[n_local:]:
        cp.wait_recv()
    for cp in copies[n_local:]:
        cp.wait_send()
    for cp in copies[:n_local]:
        cp.wait()


def _exchange_out_shapes(blocks, common):
    return ([jax.ShapeDtypeStruct(b.shape, b.dtype) for b in blocks]
            + [jax.ShapeDtypeStruct((N_DEV,) + a.shape, a.dtype) for a in common])


def _exchange_sems(n):
    return [pltpu.SemaphoreType.DMA((n, 7)), pltpu.SemaphoreType.DMA((n, 7)), pltpu.SemaphoreType.DMA((n,))]


def _grad_exchange(blocks, common):
    n, nc = len(blocks), len(common)

    def body(*refs):
        copies = _exchange_copies(refs[:n], refs[n:n + nc], refs[n + nc:2 * (n + nc)], *refs[2 * (n + nc):])
        for cp in copies:
            cp.start()
        _exchange_wait(copies)

    return pl.pallas_call(
        body, name="grad_exchange",
        out_shape=_exchange_out_shapes(blocks, common),
        in_specs=[pl.BlockSpec(memory_space=pl.ANY)] * (n + nc),
        out_specs=[pl.BlockSpec(memory_space=pl.ANY)] * (n + nc),
        scratch_shapes=_exchange_sems(n + nc),
    )(*blocks, *common)


def _adamw(parts, w, m, v):
    rows, width = w.shape
    tr = _tile(rows, max(8, ADAM_BLOCK // width // 8 * 8))

    def body(p_ref, w_ref, m_ref, v_ref, g_ref, d_ref, nm_ref, nv_ref):
        g = p_ref[0].astype(F32)
        for s in range(1, N_DEV):
            g = g + p_ref[s].astype(F32)
        nm = ADAM_B1 * m_ref[...] + (1.0 - ADAM_B1) * g
        nv = ADAM_B2 * v_ref[...] + (1.0 - ADAM_B2) * (g * g)
        m_hat = nm / (1.0 - ADAM_B1 ** ADAM_STEP)
        v_hat = nv / (1.0 - ADAM_B2 ** ADAM_STEP)
        g_ref[...] = g
        d_ref[...] = -ADAM_LR * (m_hat / (jnp.sqrt(v_hat) + ADAM_EPS) + ADAM_WD * w_ref[...])
        nm_ref[...] = nm
        nv_ref[...] = nv

    row = pl.BlockSpec((tr, width), lambda i: (i, 0))
    return pl.pallas_call(
        body, name="adamw", grid=(rows // tr,),
        in_specs=[pl.BlockSpec((N_DEV, tr, width), lambda i: (0, i, 0)), row, row, row],
        out_specs=[row] * 4,
        out_shape=[jax.ShapeDtypeStruct(w.shape, F32)] * 4,
        compiler_params=_params("parallel"),
    )(parts, w, m, v)


def _pack_rows(flat_parts, multiple):
    flat = jnp.concatenate([p.reshape(-1) for p in flat_parts])
    chunk = multiple * PACK_W
    pad = (-flat.shape[0]) % chunk
    if pad:
        flat = jnp.concatenate([flat, jnp.zeros((pad,), flat.dtype)])
    return flat.reshape(-1, PACK_W)


def _unpack(packed, shapes):
    flat = packed.reshape(-1)
    out, off = [], 0
    for shape in shapes:
        size = 1
        for s in shape:
            size *= s
        out.append(flat[off:off + size].reshape(shape))
        off += size
    return out


def _to_full(name, g):
    n, l, a, b = g.shape
    if name in ROW_SHARDED:
        return jnp.transpose(g, (1, 0, 2, 3)).reshape(l, n * a, b)
    return jnp.transpose(g, (1, 2, 0, 3)).reshape(l, a, n * b)


def _to_shards(name, full):
    l, a, b = full.shape
    if name in ROW_SHARDED:
        return jnp.transpose(full.reshape(l, N_DEV, a // N_DEV, b), (1, 0, 2, 3))
    return jnp.transpose(full.reshape(l, a, N_DEV, b // N_DEV), (2, 0, 1, 3))


def _pad_heads(w, heads, real):
    lead = w.shape[:-1]
    w = w.reshape(lead + (heads, real))
    w = jnp.concatenate([w, jnp.zeros(lead + (heads, LANES - real), w.dtype)], axis=-1)
    return w.reshape(lead + (heads * LANES,))


def _unpad_heads(w, heads, real):
    lead = w.shape[:-1]
    return w.reshape(lead + (heads, LANES))[..., :real].reshape(lead + (heads * real,))


def _pad_lanes(v, before):
    l, n = v.shape
    return jnp.concatenate([jnp.zeros((l, before), v.dtype), v, jnp.zeros((l, LANES - before - n), v.dtype)], axis=1)


def kernel(x, ffn1_norm, ffn1_w_gu, ffn1_w_down, mix_norm, w_in, q_lat_norm, kv_lat_norm, w_uq, w_uk, w_uv, q_norm, k_norm, w_pool, pool_scale, w_out, ffn2_norm, ffn2_w_gu, ffn2_w_down, loss_target, m_ffn1_norm, m_ffn1_w_gu, m_ffn1_w_down, m_mix_norm, m_w_in, m_q_lat_norm, m_kv_lat_norm, m_w_uq, m_w_uk, m_w_uv, m_q_norm, m_k_norm, m_w_pool, m_pool_scale, m_w_out, m_ffn2_norm, m_ffn2_w_gu, m_ffn2_w_down, v_ffn1_norm, v_ffn1_w_gu, v_ffn1_w_down, v_mix_norm, v_w_in, v_q_lat_norm, v_kv_lat_norm, v_w_uq, v_w_uk, v_w_uv, v_q_norm, v_k_norm, v_w_pool, v_pool_scale, v_w_out, v_ffn2_norm, v_ffn2_w_gu, v_ffn2_w_down):
    given = dict(locals())
    wts = {n: given[n] for n in WEIGHTS}
    mom1 = {n: given["m_" + n] for n in WEIGHTS}
    mom2 = {n: given["v_" + n] for n in WEIGHTS}

    depth, d = ffn1_norm.shape
    seq = x.shape[1]
    dff = ffn1_w_down.shape[1] * N_DEV
    ql, kvl, head_dim = q_lat_norm.shape[1], kv_lat_norm.shape[1], q_norm.shape[1]
    heads = w_uq.shape[2] * N_DEV // head_dim
    nope = w_uk.shape[2] * N_DEV // heads
    vh = w_uv.shape[2] * N_DEV // heads
    groups, gdim = w_pool.shape[1], w_pool.shape[2]
    pool_w = groups * gdim
    assert gdim == LANES and groups == len(POOL_WINDOWS) and head_dim <= LANES and vh <= LANES
    assert d % LANES == 0 and ql % LANES == 0 and kvl % LANES == 0 and seq % HALO == 0
    dm = _Dims(d, ql, kvl, heads, head_dim, nope, vh, pool_w, seq)

    def with_padded(full):
        zpad = jnp.zeros((full["w_in"].shape[0], d, LANES), MXU_DTYPE)
        full["win_p"] = jnp.concatenate(
            [full["w_in"][..., :dm.o_pe], zpad[..., :nope], full["w_in"][..., dm.o_pe:dm.o_pe + dm.rope],
             zpad[..., :LANES - nope - dm.rope], full["w_in"][..., dm.o_pe + dm.rope:]], axis=-1)
        full["wuq_p"] = _pad_heads(full["w_uq"], heads, head_dim)
        full["wuk_p"] = _pad_heads(full["w_uk"], heads, nope)
        return full

    later = [wts[n][1:].astype(MXU_DTYPE) for n in SHARDED] if depth > 1 else []
    first = _all_gather([wts[n][:1].astype(MXU_DTYPE) for n in SHARDED])
    groups = [with_padded({n: _to_full(n, g) for n, g in zip(SHARDED, first)}), None]

    def layer_weights(l):
        group, i = (groups[0], 0) if l == 0 else (groups[1], l - 1)
        return {n: a[i] for n, a in group.items()}

    gq_p, gk_p = _pad_lanes(q_norm, 0), _pad_lanes(k_norm, 0)
    wpool_c = w_pool.astype(MXU_DTYPE)
    rc, rs1, rs2 = _rope_tables(seq, nope, dm.rope)
    row = lambda a, l: a[l][None, :]

    h = x[0]
    saved = []
    for l in range(depth):
        w = layer_weights(l)
        x0 = h
        x1 = _ffn_fwd(x0, row(ffn1_norm, l), w["ffn1_w_gu"], w["ffn1_w_down"])
        q, k, v, zp = _mixin_fwd(dm, x1, row(mix_norm, l), w["win_p"], row(q_lat_norm, l), row(kv_lat_norm, l),
                                 w["wuq_p"], w["wuk_p"], w["w_uv"], row(gq_p, l), row(gk_p, l), rc, rs1, rs2)
        (o, lse), got = _attn_fwd(dm, q, k, v, later if l == 0 else ())
        if l == 0 and later:
            groups[1] = with_padded({n: _to_full(n, g) for n, g in zip(SHARDED, got)})
        x2 = _mixout_fwd(dm, x1, zp, o, w["w_out"], wpool_c[l], row(pool_scale, l))
        h = _ffn_fwd(x2, row(ffn2_norm, l), w["ffn2_w_gu"], w["ffn2_w_down"])
        saved.append((x0, x1, x2, q, k, v, zp, o, lse))

    part, g = _loss_head(h, loss_target[0])
    loss = lax.psum(0.5 / d * jnp.sum(part), ("x", "y", "c"))

    def ffn_grads(grads, w, prefix, l, xin, gout):
        wgu, wd = w[prefix + "_w_gu"], w[prefix + "_w_down"]
        gain = row(given[prefix + "_norm"], l)
        hh, act, dg, du = _ffn_bwd_act(xin, gain, wgu, wd.T, gout)
        gin, dgain = _ffn_bwd_in(xin, gain, wgu.T, dg, du, gout)
        grads[prefix + "_norm"] = dgain[0]
        grads[prefix + "_w_gu"] = jnp.concatenate(
            [_wgrad(hh, dg, 1.0, 1024, 1408), _wgrad(hh, du, 1.0, 1024, 1408)], axis=1)
        grads[prefix + "_w_down"] = _wgrad(act, gout, 0.5, 1408, 1024)
        return gin

    received = [None] * depth
    pending = None
    for l in reversed(range(depth)):
        x0, x1, x2, q, k, v, zp, o, lse = saved[l]
        w = layer_weights(l)
        grads = {}
        g = ffn_grads(grads, w, "ffn2", l, x2, g)
        do, delta, dmixed, dwo, dwpool, dps = _mixout_bwd(dm, g, zp, o, w["w_out"], wpool_c[l], row(pool_scale, l))
        if pending is None:
            (dq, dk, dv), _ = _attn_bwd(dm, q, k, v, do, lse, delta)
        else:
            (dq, dk, dv), received[l + 1] = _attn_bwd(dm, q, k, v, do, lse, delta, *pending)
        dz, dwuq, dwuk, dwuv, dgql, dgkvl, dgq, dgk = _mla_in_bwd(
            dm, x1, row(mix_norm, l), w["win_p"], row(q_lat_norm, l), row(kv_lat_norm, l), w["wuq_p"], w["wuk_p"],
            w["w_uv"], row(gq_p, l), row(gk_p, l), rc, rs1, rs2, dq, dk, dv, dmixed)
        g, dwin, dgmix = _rms_proj_bwd(x1, row(mix_norm, l), w["win_p"], dz, g)
        grads["w_out"] = dwo
        grads["w_pool"] = dwpool
        grads["pool_scale"] = dps[0]
        grads["w_uq"] = _unpad_heads(dwuq, heads, head_dim)
        grads["w_uk"] = _unpad_heads(dwuk, heads, nope)
        grads["w_uv"] = dwuv
        grads["q_lat_norm"] = dgql[0]
        grads["kv_lat_norm"] = dgkvl[0]
        grads["q_norm"] = dgq[0, :head_dim]
        grads["k_norm"] = dgk[0, :head_dim]
        grads["w_in"] = jnp.concatenate(
            [dwin[:, :dm.o_pe], dwin[:, dm.o_pe + nope:dm.o_pe + nope + dm.rope], dwin[:, dm.o_pool:]], axis=1)
        grads["mix_norm"] = dgmix[0]
        g = ffn_grads(grads, w, "ffn1", l, x0, g)
        pending = ([_to_shards(n, grads[n][None])[:, 0].astype(WIRE_DTYPE) for n in SHARDED],
                   [_pack_rows([grads[n] for n in REPLICATED], 8)])
    received[0] = _grad_exchange(*pending)

    outs = [{}, {}, {}, {}]
    for t, n in enumerate(SHARDED):
        nl, a, b = wts[n].shape
        parts = jnp.stack([received[l][t] for l in range(depth)], axis=1)
        results = _adamw(parts.reshape(N_DEV, nl * a, b),
                         *[src[n].reshape(nl * a, b) for src in (wts, mom1, mom2)])
        for out, r in zip(outs, results):
            out[n] = r.reshape(nl, a, b)
    repl_shapes = [wts[n].shape[1:] for n in REPLICATED]
    per_layer = []
    for l in range(depth):
        results = _adamw(received[l][-1],
                         *[_pack_rows([src[n][l] for n in REPLICATED], 8) for src in (wts, mom1, mom2)])
        per_layer.append([_unpack(r, repl_shapes) for r in results])
    for i, out in enumerate(outs):
        for t, n in enumerate(REPLICATED):
            out[n] = jnp.stack([per_layer[l][i][t] for l in range(depth)])

    return (loss, g[None], *[out[n] for out in outs for n in WEIGHTS])
```

```python
import functools

import jax
import jax.numpy as jnp
from jax import lax
from jax.experimental import pallas as pl
from jax.experimental.pallas import tpu as pltpu

F32 = jnp.float32
MXU_DTYPE = jnp.bfloat16
WIRE_DTYPE = jnp.bfloat16
EPS = 1e-6
LOG2E, LN2 = 1.4426950408889634, 0.6931471805599453
ROPE_THETA = 10000.0
POOL_WINDOWS = (2, 4, 8, 16)
ADAM_LR, ADAM_B1, ADAM_B2, ADAM_EPS, ADAM_WD, ADAM_STEP = 0.001, 0.9, 0.999, 1e-08, 0.01, 10

LANES = 128
HALO = 64
PACK_W = 1024
N_DEV = 8
VMEM_LIMIT = 56 * 1024 * 1024
MESH = pl.DeviceIdType.MESH

LOSS_TM = 1024
FFN_FWD_TM, FFN_FWD_TN = 512, 1408
FFN_BWD_TM, FFN_BWD_TN = 512, 1408
FFN_IN_TM = 512
MIX_TM = 256
MIX_BWD_TM = 512
ATTN_CH = 1024
WG_BK = 2048
ADAM_BLOCK = 128 * 1024

SHARDED = ("ffn1_w_gu", "ffn1_w_down", "w_in", "w_uq", "w_uk", "w_uv", "w_out", "ffn2_w_gu", "ffn2_w_down")
ROW_SHARDED = ("ffn1_w_down", "w_out", "ffn2_w_down")
REPLICATED = ("ffn1_norm", "mix_norm", "q_lat_norm", "kv_lat_norm", "q_norm", "k_norm", "w_pool", "pool_scale",
              "ffn2_norm")
WEIGHTS = ("ffn1_norm", "ffn1_w_gu", "ffn1_w_down", "mix_norm", "w_in", "q_lat_norm", "kv_lat_norm", "w_uq", "w_uk",
           "w_uv", "q_norm", "k_norm", "w_pool", "pool_scale", "w_out", "ffn2_norm", "ffn2_w_gu", "ffn2_w_down")


def _tile(n, pref):
    if n <= pref:
        return n
    t = pref - pref % 8
    while n % t:
        t -= 8
    return t


def _params(*sem):
    return pltpu.CompilerParams(dimension_semantics=sem, vmem_limit_bytes=VMEM_LIMIT)


def _mm(a, b):
    return jnp.dot(a.astype(MXU_DTYPE), b.astype(MXU_DTYPE), preferred_element_type=F32)


def _mm_nt(a, b):
    return lax.dot_general(a.astype(MXU_DTYPE), b.astype(MXU_DTYPE), (((1,), (1,)), ((), ())),
                           preferred_element_type=F32)


def _mm_tn(a, b):
    return lax.dot_general(a.astype(MXU_DTYPE), b.astype(MXU_DTYPE), (((0,), (0,)), ((), ())),
                           preferred_element_type=F32)


def _rstd(x, n):
    return lax.rsqrt(jnp.sum(x * x, axis=-1, keepdims=True) / n + EPS)


def _rms_bwd(dy, xhat, r, gain, n):
    dxh = dy * gain
    return r * (dxh - xhat * (jnp.sum(dxh * xhat, axis=-1, keepdims=True) / n))


def _colsum(x):
    return jnp.sum(x, axis=0, keepdims=True)


def _split3(x):
    hi = x.astype(MXU_DTYPE)
    r1 = x - hi.astype(F32)
    mid = r1.astype(MXU_DTYPE)
    lo = (r1 - mid.astype(F32)).astype(MXU_DTYPE)
    return jnp.concatenate([hi, mid, lo], axis=1)


def _sum3(x):
    n = x.shape[1] // 3
    return (x[:, :n] + x[:, n:2 * n]) + x[:, 2 * n:]


def _pool_mixed(prev, main, nxt, row0, seq):
    tm = main.shape[0]
    k = tm + 2 * HALO
    ext = jnp.concatenate([prev, main, nxt], axis=0)
    s_i = lax.broadcasted_iota(jnp.int32, (tm, k), 0) + row0
    t_j = lax.broadcasted_iota(jnp.int32, (tm, k), 1) + (row0 - HALO)
    s_v = lax.broadcasted_iota(jnp.int32, (tm, 1), 0) + row0
    inside = (t_j >= 0) & (t_j < seq)
    outs = []
    for g, w in enumerate(POOL_WINDOWS):
        left, right = w // 2, w - 1 - w // 2
        band = jnp.where((t_j >= s_i - left) & (t_j <= s_i + right) & inside, 1.0, 0.0).astype(MXU_DTYPE)
        sl = slice(g * LANES, (g + 1) * LANES)
        wsum = _sum3(jnp.dot(band, _split3(ext[:, sl]), preferred_element_type=F32))
        cnt = (jnp.minimum(s_v + right + 1, seq) - jnp.maximum(s_v - left, 0)).astype(F32)
        outs.append(wsum / cnt - main[:, sl])
    return jnp.concatenate(outs, axis=1)


def _pool_mixed_t(prev, main, nxt, row0, seq):
    tm = main.shape[0]
    k = tm + 2 * HALO
    ext = jnp.concatenate([prev, main, nxt], axis=0)
    t_i = lax.broadcasted_iota(jnp.int32, (tm, k), 0) + row0
    s_j = lax.broadcasted_iota(jnp.int32, (tm, k), 1) + (row0 - HALO)
    s_v = lax.broadcasted_iota(jnp.int32, (k, 1), 0) + (row0 - HALO)
    inside = (s_j >= 0) & (s_j < seq)
    inside_v = (s_v >= 0) & (s_v < seq)
    outs = []
    for g, w in enumerate(POOL_WINDOWS):
        left, right = w // 2, w - 1 - w // 2
        band = jnp.where((s_j >= t_i - right) & (s_j <= t_i + left) & inside, 1.0, 0.0).astype(MXU_DTYPE)
        sl = slice(g * LANES, (g + 1) * LANES)
        cnt = (jnp.minimum(s_v + right + 1, seq) - jnp.maximum(s_v - left, 0)).astype(F32)
        scaled = jnp.where(inside_v, ext[:, sl] / jnp.maximum(cnt, 1.0), 0.0)
        outs.append(_sum3(jnp.dot(band, _split3(scaled), preferred_element_type=F32)) - main[:, sl])
    return jnp.concatenate(outs, axis=1)


def _halo_specs(tm, width, seq):
    per = tm // HALO
    last = seq // HALO - 1
    prev = pl.BlockSpec((HALO, width), lambda i: (jnp.maximum(i * per - 1, 0), 0))
    nxt = pl.BlockSpec((HALO, width), lambda i: (jnp.minimum((i + 1) * per, last), 0))
    return prev, nxt


def _rope(x, c, s1, s2, half):
    return x * c + pltpu.roll(x, half, 1) * s1 + pltpu.roll(x, LANES - half, 1) * s2


def _rope_t(d, c, s1, s2, half):
    return d * c + pltpu.roll(d * s1, LANES - half, 1) + pltpu.roll(d * s2, half, 1)


def _rope_tables(seq, nope, rope):
    half = rope // 2
    pos = jnp.arange(seq, dtype=F32)
    inv = ROPE_THETA ** (-jnp.arange(0, rope, 2, dtype=F32) / rope)
    ang = pos[:, None] * inv[None, :]
    cos, sin = jnp.cos(ang), jnp.sin(ang)
    zeros = lambda n: jnp.zeros((seq, n), F32)
    ones = lambda n: jnp.ones((seq, n), F32)
    tail = LANES - nope - rope
    c = jnp.concatenate([ones(nope), cos, cos, ones(tail)], axis=1)
    s1 = jnp.concatenate([zeros(nope + half), sin, zeros(tail)], axis=1)
    s2 = jnp.concatenate([zeros(nope), -sin, zeros(half + tail)], axis=1)
    return c, s1, s2


def _ffn_fwd(x, gain, wgu, wd):
    seq, d = x.shape
    f = wd.shape[0]
    tm, tn = _tile(seq, FFN_FWD_TM), _tile(f, FFN_FWD_TN)
    nk = f // tn

    def body(x_ref, g_ref, wg_ref, wu_ref, wd_ref, o_ref, h_sc, acc_sc):
        k = pl.program_id(1)

        @pl.when(k == 0)
        def _():
            xv = x_ref[...]
            h_sc[...] = (xv * _rstd(xv, d) * g_ref[...]).astype(MXU_DTYPE)
            acc_sc[...] = jnp.zeros_like(acc_sc)

        h = h_sc[...]
        g = jnp.dot(h, wg_ref[...], preferred_element_type=F32)
        u = jnp.dot(h, wu_ref[...], preferred_element_type=F32)
        a = g * (1.0 / (1.0 + jnp.exp(-g))) * u
        acc_sc[...] += _mm(a, wd_ref[...])

        @pl.when(k == nk - 1)
        def _():
            o_ref[...] = x_ref[...] + 0.5 * acc_sc[...]

    return pl.pallas_call(
        body, name="ffn_fwd", grid=(seq // tm, nk),
        in_specs=[pl.BlockSpec((tm, d), lambda i, k: (i, 0)),
                  pl.BlockSpec((1, d), lambda i, k: (0, 0)),
                  pl.BlockSpec((d, tn), lambda i, k: (0, k)),
                  pl.BlockSpec((d, tn), lambda i, k: (0, k + nk)),
                  pl.BlockSpec((tn, d), lambda i, k: (k, 0))],
        out_specs=pl.BlockSpec((tm, d), lambda i, k: (i, 0)),
        out_shape=jax.ShapeDtypeStruct((seq, d), F32),
        scratch_shapes=[pltpu.VMEM((tm, d), MXU_DTYPE), pltpu.VMEM((tm, d), F32)],
        compiler_params=_params("parallel", "arbitrary"),
    )(x, gain, wgu, wgu, wd)


def _ffn_bwd_act(x, gain, wgu, wdt, dout):
    seq, d = x.shape
    f = wdt.shape[1]
    tm, tn = _tile(seq, FFN_BWD_TM), _tile(f, FFN_BWD_TN)
    nk = f // tn

    def body(x_ref, g_ref, wg_ref, wu_ref, wdt_ref, do_ref, h_ref, act_ref, dg_ref, du_ref, dy_sc):
        @pl.when(pl.program_id(1) == 0)
        def _():
            xv = x_ref[...]
            h_ref[...] = (xv * _rstd(xv, d) * g_ref[...]).astype(MXU_DTYPE)
            dy_sc[...] = (0.5 * do_ref[...]).astype(MXU_DTYPE)

        h = h_ref[...]
        g = jnp.dot(h, wg_ref[...], preferred_element_type=F32)
        u = jnp.dot(h, wu_ref[...], preferred_element_type=F32)
        sig = 1.0 / (1.0 + jnp.exp(-g))
        silu = g * sig
        act_ref[...] = (silu * u).astype(MXU_DTYPE)
        da = jnp.dot(dy_sc[...], wdt_ref[...], preferred_element_type=F32)
        du_ref[...] = (da * silu).astype(MXU_DTYPE)
        dg_ref[...] = (da * u * (sig * (1.0 + g * (1.0 - sig)))).astype(MXU_DTYPE)

    row = lambda i, k: (i, 0)
    col = lambda i, k: (i, k)
    return pl.pallas_call(
        body, name="ffn_bwd_act", grid=(seq // tm, nk),
        in_specs=[pl.BlockSpec((tm, d), row),
                  pl.BlockSpec((1, d), lambda i, k: (0, 0)),
                  pl.BlockSpec((d, tn), lambda i, k: (0, k)),
                  pl.BlockSpec((d, tn), lambda i, k: (0, k + nk)),
                  pl.BlockSpec((d, tn), lambda i, k: (0, k)),
                  pl.BlockSpec((tm, d), row)],
        out_specs=[pl.BlockSpec((tm, d), row), pl.BlockSpec((tm, tn), col), pl.BlockSpec((tm, tn), col),
                   pl.BlockSpec((tm, tn), col)],
        out_shape=[jax.ShapeDtypeStruct((seq, d), MXU_DTYPE), jax.ShapeDtypeStruct((seq, f), MXU_DTYPE),
                   jax.ShapeDtypeStruct((seq, f), MXU_DTYPE), jax.ShapeDtypeStruct((seq, f), MXU_DTYPE)],
        scratch_shapes=[pltpu.VMEM((tm, d), MXU_DTYPE)],
        compiler_params=_params("parallel", "arbitrary"),
    )(x, gain, wgu, wgu, wdt, dout)


def _ffn_bwd_in(x, gain, wgut, dg, du, dout):
    seq, d = x.shape
    f = dg.shape[1]
    tm = _tile(seq, FFN_IN_TM)

    def body(x_ref, g_ref, wgut_ref, dg_ref, du_ref, do_ref, dx_ref, dgain_ref):
        @pl.when(pl.program_id(0) == 0)
        def _():
            dgain_ref[...] = jnp.zeros_like(dgain_ref)

        dh = (jnp.dot(dg_ref[...], wgut_ref[:f, :], preferred_element_type=F32)
              + jnp.dot(du_ref[...], wgut_ref[f:, :], preferred_element_type=F32))
        xv = x_ref[...]
        r = _rstd(xv, d)
        xhat = xv * r
        dgain_ref[...] += _colsum(dh * xhat)
        dx_ref[...] = do_ref[...] + _rms_bwd(dh, xhat, r, g_ref[...], d)

    row = lambda i: (i, 0)
    return pl.pallas_call(
        body, name="ffn_bwd_in", grid=(seq // tm,),
        in_specs=[pl.BlockSpec((tm, d), row), pl.BlockSpec((1, d), lambda i: (0, 0)),
                  pl.BlockSpec((2 * f, d), lambda i: (0, 0)), pl.BlockSpec((tm, f), row), pl.BlockSpec((tm, f), row),
                  pl.BlockSpec((tm, d), row)],
        out_specs=[pl.BlockSpec((tm, d), row), pl.BlockSpec((1, d), lambda i: (0, 0))],
        out_shape=[jax.ShapeDtypeStruct((seq, d), F32), jax.ShapeDtypeStruct((1, d), F32)],
        compiler_params=_params("arbitrary"),
    )(x, gain, wgut, dg, du, dout)


def _wgrad(a, b, scale, bm, bn):
    seq, m = a.shape
    n = b.shape[1]
    bm, bn, bk = _tile(m, bm), _tile(n, bn), _tile(seq, WG_BK)
    ns = seq // bk

    def body(a_ref, b_ref, o_ref):
        @pl.when(pl.program_id(2) == 0)
        def _():
            o_ref[...] = jnp.zeros_like(o_ref)

        o_ref[...] += scale * _mm_tn(a_ref[...], b_ref[...])

    return pl.pallas_call(
        body, name="wgrad", grid=(m // bm, n // bn, ns),
        in_specs=[pl.BlockSpec((bk, bm), lambda i, j, s: (s, i)),
                  pl.BlockSpec((bk, bn), lambda i, j, s: (s, j))],
        out_specs=pl.BlockSpec((bm, bn), lambda i, j, s: (i, j)),
        out_shape=jax.ShapeDtypeStruct((m, n), F32),
        compiler_params=_params("parallel", "parallel", "arbitrary"),
    )(a, b)


class _Dims:
    def __init__(self, d, ql, kvl, heads, head_dim, nope, vh, pool_w, seq):
        self.d, self.ql, self.kvl, self.heads, self.head_dim, self.nope, self.vh, self.pool_w = (
            d, ql, kvl, heads, head_dim, nope, vh, pool_w)
        self.rope = head_dim - nope
        self.half = self.rope // 2
        self.hw = heads * LANES
        self.vw = heads * vh
        self.ch = _tile(seq, ATTN_CH)
        self.nch = seq // self.ch
        self.o_kv = ql
        self.o_pe = ql + kvl
        self.o_pool = ql + kvl + LANES
        self.zw = self.o_pool + pool_w
        self.scale = head_dim ** -0.5


def _chunk_spec(dm, rows, tm):
    per = dm.ch // tm
    return pl.BlockSpec((dm.heads, 1, rows, tm), lambda i: (0, i // per, 0, i % per))


def _mixin_fwd(dm, x, gmix, win, gql, gkvl, wuq, wuk, wuv, gq, gk, rc, rs1, rs2):
    seq, d = x.shape
    tm = _tile(seq, MIX_TM)

    def body(x_ref, gmix_ref, win_ref, gql_ref, gkvl_ref, wuq_ref, wuk_ref, wuv_ref, gq_ref, gk_ref,
             rc_ref, rs1_ref, rs2_ref, q_ref, k_ref, v_ref, zp_ref):
        xv = x_ref[...]
        z = _mm(xv * _rstd(xv, d) * gmix_ref[...], win_ref[...])
        cq, ckv = z[:, :dm.o_kv], z[:, dm.o_kv:dm.o_pe]
        kpe = z[:, dm.o_pe:dm.o_pool]
        zp_ref[...] = z[:, dm.o_pool:]
        cqn = cq * _rstd(cq, dm.ql) * gql_ref[...]
        ckvn = ckv * _rstd(ckv, dm.kvl) * gkvl_ref[...]
        q = _mm(cqn, wuq_ref[...])
        kn = _mm(ckvn, wuk_ref[...])
        v_ref[...] = _mm(ckvn, wuv_ref[...]).T.reshape(dm.heads, 1, dm.vh, tm).astype(MXU_DTYPE)
        c, s1, s2 = rc_ref[...], rs1_ref[...], rs2_ref[...]
        for h in range(dm.heads):
            sl = slice(h * LANES, (h + 1) * LANES)
            qh = q[:, sl]
            qn = qh * _rstd(qh, dm.head_dim) * gq_ref[...]
            q_ref[:, sl] = (_rope(qn, c, s1, s2, dm.half) * dm.scale * LOG2E).astype(MXU_DTYPE)
            kh = kn[:, sl] + kpe
            kk = kh * _rstd(kh, dm.head_dim) * gk_ref[...]
            k_ref[:, sl] = _rope(kk, c, s1, s2, dm.half).astype(MXU_DTYPE)

    row = lambda i: (i, 0)
    full = lambda a: pl.BlockSpec(a.shape, lambda i: (0,) * a.ndim)
    return pl.pallas_call(
        body, name="mixin_fwd", grid=(seq // tm,),
        in_specs=[pl.BlockSpec((tm, d), row), full(gmix), full(win), full(gql), full(gkvl), full(wuq), full(wuk),
                  full(wuv), full(gq), full(gk),
                  pl.BlockSpec((tm, LANES), row), pl.BlockSpec((tm, LANES), row), pl.BlockSpec((tm, LANES), row)],
        out_specs=[pl.BlockSpec((tm, dm.hw), row), pl.BlockSpec((tm, dm.hw), row), _chunk_spec(dm, dm.vh, tm),
                   pl.BlockSpec((tm, dm.pool_w), row)],
        out_shape=[jax.ShapeDtypeStruct((seq, dm.hw), MXU_DTYPE),
                   jax.ShapeDtypeStruct((seq, dm.hw), MXU_DTYPE),
                   jax.ShapeDtypeStruct((dm.heads, dm.nch, dm.vh, dm.ch), MXU_DTYPE),
                   jax.ShapeDtypeStruct((seq, dm.pool_w), F32)],
        compiler_params=_params("parallel"),
    )(x, gmix, win, gql, gkvl, wuq, wuk, wuv, gq, gk, rc, rs1, rs2)


def _mla_in_bwd(dm, x, gmix, win, gql, gkvl, wuq, wuk, wuv, gq, gk, rc, rs1, rs2, dq, dk, dv, dmixed):
    seq, d = x.shape
    tm = _tile(seq, MIX_TM)

    def body(x_ref, gmix_ref, win_ref, gql_ref, gkvl_ref, wuq_ref, wuk_ref, wuv_ref, gq_ref, gk_ref,
             rc_ref, rs1_ref, rs2_ref, dq_ref, dk_ref, dv_ref, dmp_ref, dm_ref, dmn_ref,
             dz_ref, dwuq_ref, dwuk_ref, dwuv_ref, dgql_ref, dgkvl_ref, dgq_ref, dgk_ref):
        i = pl.program_id(0)

        @pl.when(i == 0)
        def _():
            for ref in (dwuq_ref, dwuk_ref, dwuv_ref, dgql_ref, dgkvl_ref, dgq_ref, dgk_ref):
                ref[...] = jnp.zeros_like(ref)

        xv = x_ref[...]
        z = _mm(xv * _rstd(xv, d) * gmix_ref[...], win_ref[...])
        cq, ckv = z[:, :dm.o_kv], z[:, dm.o_kv:dm.o_pe]
        kpe = z[:, dm.o_pe:dm.o_pool]
        r_q, r_kv = _rstd(cq, dm.ql), _rstd(ckv, dm.kvl)
        cqh, ckvh = cq * r_q, ckv * r_kv
        cqn = (cqh * gql_ref[...]).astype(MXU_DTYPE)
        ckvn = (ckvh * gkvl_ref[...]).astype(MXU_DTYPE)
        q = _mm(cqn, wuq_ref[...])
        kn = _mm(ckvn, wuk_ref[...])
        c, s1, s2 = rc_ref[...], rs1_ref[...], rs2_ref[...]
        dqv = dq_ref[...].reshape(dm.hw, tm).T
        dq_pre, dk_pre = [], []
        dkpe = jnp.zeros((tm, LANES), F32)
        dgq = jnp.zeros((1, LANES), F32)
        dgk = jnp.zeros((1, LANES), F32)
        for h in range(dm.heads):
            sl = slice(h * LANES, (h + 1) * LANES)
            qh = q[:, sl]
            rq = _rstd(qh, dm.head_dim)
            xq = qh * rq
            dqn = _rope_t(dqv[:, sl] * dm.scale, c, s1, s2, dm.half)
            dgq += _colsum(dqn * xq)
            dq_pre.append(_rms_bwd(dqn, xq, rq, gq_ref[...], dm.head_dim))
            kh = kn[:, sl] + kpe
            rk = _rstd(kh, dm.head_dim)
            xk = kh * rk
            dkn = _rope_t(dk_ref[:, sl], c, s1, s2, dm.half)
            dgk += _colsum(dkn * xk)
            dkh = _rms_bwd(dkn, xk, rk, gk_ref[...], dm.head_dim)
            dk_pre.append(dkh)
            dkpe += dkh
        dgq_ref[...] += dgq
        dgk_ref[...] += dgk
        dq_pre = jnp.concatenate(dq_pre, axis=1).astype(MXU_DTYPE)
        dk_pre = jnp.concatenate(dk_pre, axis=1).astype(MXU_DTYPE)
        dvv = dv_ref[...].reshape(dm.vw, tm).T.astype(MXU_DTYPE)
        dwuq_ref[...] += _mm_tn(cqn, dq_pre)
        dwuk_ref[...] += _mm_tn(ckvn, dk_pre)
        dwuv_ref[...] += _mm_tn(ckvn, dvv)
        dcqn = _mm_nt(dq_pre, wuq_ref[...])
        dckvn = _mm_nt(dk_pre, wuk_ref[...]) + _mm_nt(dvv, wuv_ref[...])
        dgql_ref[...] += _colsum(dcqn * cqh)
        dgkvl_ref[...] += _colsum(dckvn * ckvh)
        dcq = _rms_bwd(dcqn, cqh, r_q, gql_ref[...], dm.ql)
        dckv = _rms_bwd(dckvn, ckvh, r_kv, gkvl_ref[...], dm.kvl)
        dzp = _pool_mixed_t(dmp_ref[...], dm_ref[...], dmn_ref[...], i * tm, seq)
        dz_ref[...] = jnp.concatenate([dcq, dckv, dkpe, dzp], axis=1).astype(MXU_DTYPE)

    row = lambda i: (i, 0)
    full = lambda a: pl.BlockSpec(a.shape, lambda i: (0,) * a.ndim)
    acc = lambda shape: pl.BlockSpec(shape, lambda i: (0, 0))
    prev, nxt = _halo_specs(tm, dm.pool_w, seq)
    shapes = [(seq, dm.zw), wuq.shape, wuk.shape, wuv.shape, (1, dm.ql), (1, dm.kvl), (1, LANES), (1, LANES)]
    return pl.pallas_call(
        body, name="mla_in_bwd", grid=(seq // tm,),
        in_specs=[pl.BlockSpec((tm, d), row), full(gmix), full(win), full(gql), full(gkvl), full(wuq), full(wuk),
                  full(wuv), full(gq), full(gk),
                  pl.BlockSpec((tm, LANES), row), pl.BlockSpec((tm, LANES), row), pl.BlockSpec((tm, LANES), row),
                  _chunk_spec(dm, LANES, tm), pl.BlockSpec((tm, dm.hw), row), _chunk_spec(dm, dm.vh, tm),
                  prev, pl.BlockSpec((tm, dm.pool_w), row), nxt],
        out_specs=[pl.BlockSpec((tm, dm.zw), row)] + [acc(s) for s in shapes[1:]],
        out_shape=[jax.ShapeDtypeStruct(shapes[0], MXU_DTYPE)] + [jax.ShapeDtypeStruct(s, F32) for s in shapes[1:]],
        compiler_params=_params("arbitrary"),
    )(x, gmix, win, gql, gkvl, wuq, wuk, wuv, gq, gk, rc, rs1, rs2, dq, dk, dv, dmixed, dmixed, dmixed)


def _rms_proj_bwd(x, gain, w, dz, gin):
    seq, d = x.shape
    n = w.shape[1]
    tm = _tile(seq, MIX_BWD_TM)

    def body(x_ref, g_ref, w_ref, dz_ref, gin_ref, gout_ref, dw_ref, dgain_ref):
        @pl.when(pl.program_id(0) == 0)
        def _():
            dw_ref[...] = jnp.zeros_like(dw_ref)
            dgain_ref[...] = jnp.zeros_like(dgain_ref)

        xv = x_ref[...]
        r = _rstd(xv, d)
        xhat = xv * r
        dzv = dz_ref[...]
        dh = _mm_nt(dzv, w_ref[...])
        dw_ref[...] += _mm_tn(xhat * g_ref[...], dzv)
        dgain_ref[...] += _colsum(dh * xhat)
        gout_ref[...] = gin_ref[...] + _rms_bwd(dh, xhat, r, g_ref[...], d)

    row = lambda i: (i, 0)
    return pl.pallas_call(
        body, name="rms_proj_bwd", grid=(seq // tm,),
        in_specs=[pl.BlockSpec((tm, d), row), pl.BlockSpec((1, d), lambda i: (0, 0)),
                  pl.BlockSpec((d, n), lambda i: (0, 0)), pl.BlockSpec((tm, n), row), pl.BlockSpec((tm, d), row)],
        out_specs=[pl.BlockSpec((tm, d), row), pl.BlockSpec((d, n), lambda i: (0, 0)),
                   pl.BlockSpec((1, d), lambda i: (0, 0))],
        out_shape=[jax.ShapeDtypeStruct((seq, d), F32), jax.ShapeDtypeStruct((d, n), F32),
                   jax.ShapeDtypeStruct((1, d), F32)],
        compiler_params=_params("arbitrary"),
    )(x, gain, w, dz, gin)


def _pool_branch(mixed, wpool_ref):
    groups = mixed.shape[1] // LANES
    return jnp.concatenate(
        [_mm(mixed[:, g * LANES:(g + 1) * LANES], wpool_ref[g]) for g in range(groups)], axis=1)


def _mixout_fwd(dm, x, zp, ot, wo, wpool, pscale):
    seq, d = x.shape
    tm = _tile(seq, MIX_TM)

    def body(x_ref, zpp_ref, zp_ref, zpn_ref, ot_ref, wo_ref, wpool_ref, ps_ref, out_ref):
        mixed = _pool_mixed(zpp_ref[...], zp_ref[...], zpn_ref[...], pl.program_id(0) * tm, seq)
        b = _pool_branch(mixed, wpool_ref) * ps_ref[...]
        a = _mm_tn(ot_ref[...].reshape(dm.vw, tm), wo_ref[:dm.vw, :])
        out_ref[...] = x_ref[...] + a + _mm(b, wo_ref[dm.vw:, :])

    row = lambda i: (i, 0)
    full = lambda a: pl.BlockSpec(a.shape, lambda i: (0,) * a.ndim)
    prev, nxt = _halo_specs(tm, dm.pool_w, seq)
    return pl.pallas_call(
        body, name="mixout_fwd", grid=(seq // tm,),
        in_specs=[pl.BlockSpec((tm, d), row), prev, pl.BlockSpec((tm, dm.pool_w), row), nxt,
                  _chunk_spec(dm, dm.vh, tm), full(wo), full(wpool), full(pscale)],
        out_specs=pl.BlockSpec((tm, d), row),
        out_shape=jax.ShapeDtypeStruct((seq, d), F32),
        compiler_params=_params("parallel"),
    )(x, zp, zp, zp, ot, wo, wpool, pscale)


def _mixout_bwd(dm, g, zp, ot, wo, wpool, pscale):
    seq, d = g.shape
    tm = _tile(seq, MIX_BWD_TM)
    groups = dm.pool_w // LANES

    def body(g_ref, zpp_ref, zp_ref, zpn_ref, ot_ref, wo_ref, wpool_ref, ps_ref,
             dot_ref, delta_ref, dmixed_ref, dwo_ref, dwpool_ref, dps_ref):
        i = pl.program_id(0)

        @pl.when(i == 0)
        def _():
            for ref in (dwo_ref, dwpool_ref, dps_ref):
                ref[...] = jnp.zeros_like(ref)

        gv = g_ref[...].astype(MXU_DTYPE)
        otv = ot_ref[...].reshape(dm.vw, tm)
        dat = _mm_nt(wo_ref[:dm.vw, :], gv)
        db = _mm_nt(gv, wo_ref[dm.vw:, :])
        mixed = _pool_mixed(zpp_ref[...], zp_ref[...], zpn_ref[...], i * tm, seq).astype(MXU_DTYPE)
        y = _pool_branch(mixed, wpool_ref)
        b = (y * ps_ref[...]).astype(MXU_DTYPE)
        dwo_ref[:dm.vw, :] += _mm(otv, gv)
        dwo_ref[dm.vw:, :] += _mm_tn(b, gv)
        dps_ref[...] += _colsum(db * y)
        dy = (db * ps_ref[...]).astype(MXU_DTYPE)
        dmx = []
        for gi in range(groups):
            sl = slice(gi * LANES, (gi + 1) * LANES)
            dmx.append(_mm_nt(dy[:, sl], wpool_ref[gi]))
            dwpool_ref[gi] += _mm_tn(mixed[:, sl], dy[:, sl])
        dmixed_ref[...] = jnp.concatenate(dmx, axis=1)
        dot_ref[...] = dat.reshape(dm.heads, 1, dm.vh, tm).astype(MXU_DTYPE)
        prod = dat * otv.astype(F32)
        for h in range(dm.heads):
            delta_ref[h, 0] = jnp.broadcast_to(_colsum(prod[h * dm.vh:(h + 1) * dm.vh]), (8, tm))

    row = lambda i: (i, 0)
    full = lambda a: pl.BlockSpec(a.shape, lambda i: (0,) * a.ndim)
    prev, nxt = _halo_specs(tm, dm.pool_w, seq)
    return pl.pallas_call(
        body, name="mixout_bwd", grid=(seq // tm,),
        in_specs=[pl.BlockSpec((tm, d), row), prev, pl.BlockSpec((tm, dm.pool_w), row), nxt,
                  _chunk_spec(dm, dm.vh, tm), full(wo), full(wpool), full(pscale)],
        out_specs=[_chunk_spec(dm, dm.vh, tm), _chunk_spec(dm, 8, tm), pl.BlockSpec((tm, dm.pool_w), row),
                   full(wo), full(wpool), full(pscale)],
        out_shape=[jax.ShapeDtypeStruct((dm.heads, dm.nch, dm.vh, dm.ch), MXU_DTYPE),
                   jax.ShapeDtypeStruct((dm.heads, dm.nch, 8, dm.ch), F32),
                   jax.ShapeDtypeStruct((seq, dm.pool_w), F32), jax.ShapeDtypeStruct(wo.shape, F32),
                   jax.ShapeDtypeStruct(wpool.shape, F32), jax.ShapeDtypeStruct(pscale.shape, F32)],
        compiler_params=_params("arbitrary"),
    )(g, zp, zp, zp, ot, wo, wpool, pscale)


def _attn_fwd(dm, q, k, vt, common=()):
    seq, ch, nch = k.shape[0], dm.ch, dm.nch
    n_ride = len(common)

    def body(*refs):
        q_ref, k_ref, vt_ref = refs[:3]
        ride_in = refs[3:3 + n_ride]
        ot_ref, lse_ref = refs[3 + n_ride:5 + n_ride]
        ride_out = refs[5 + n_ride:5 + 2 * n_ride]
        m_sc, l_sc, acc_sc, st0, st1, pt0, pt1 = refs[5 + 2 * n_ride:12 + 2 * n_ride]
        sems = refs[12 + 2 * n_ride:]
        first = (pl.program_id(0) == 0) & (pl.program_id(1) == 0)
        last = (pl.program_id(0) == dm.heads - 1) & (pl.program_id(1) == nch - 1)

        if n_ride:
            @pl.when(first)
            def _():
                for cp in _exchange_copies((), ride_in, ride_out, *sems):
                    cp.start()

        m_sc[...] = jnp.full_like(m_sc, -jnp.inf)
        l_sc[...] = jnp.zeros_like(l_sc)
        acc_sc[...] = jnp.zeros_like(acc_sc)
        qv = q_ref[...]

        def scores(j):
            return _mm_nt(k_ref[pl.ds(pl.multiple_of(j * ch, ch), ch), :], qv)

        def stage(j, st_cur, st_nxt, pt_cur, pt_prev):
            st_nxt[...] = scores(jnp.minimum(j + 1, nch - 1))
            acc = acc_sc[...] + _mm(vt_ref[0, jnp.maximum(j - 1, 0)], pt_prev[...])
            st = st_cur[...]
            m_prev = m_sc[...]
            m_new = jnp.maximum(m_prev, jnp.max(st, axis=0, keepdims=True))
            alpha = jnp.exp2(m_prev - m_new)
            pt = jnp.exp2(st - m_new)
            pt_cur[...] = pt.astype(MXU_DTYPE)
            l_sc[...] = alpha * l_sc[...] + _colsum(pt)
            acc_sc[...] = alpha * acc
            m_sc[...] = m_new

        st0[...] = scores(0)
        pt1[...] = jnp.zeros_like(pt1)

        def pair(jj, carry):
            stage(2 * jj, st0, st1, pt0, pt1)
            stage(2 * jj + 1, st1, st0, pt1, pt0)
            return carry

        lax.fori_loop(0, nch // 2, pair, 0)
        acc = acc_sc[...] + _mm(vt_ref[0, nch - 1], pt1[...])
        ot_ref[0, 0] = (acc / l_sc[...]).astype(MXU_DTYPE)
        lse_ref[0, 0] = jnp.broadcast_to(m_sc[...] + jnp.log(l_sc[...]) * LOG2E, (8, ch))

        if n_ride:
            @pl.when(last)
            def _():
                _exchange_wait(_exchange_copies((), ride_in, ride_out, *sems))

    assert nch % 2 == 0
    chunk = lambda rows: pl.BlockSpec((1, 1, rows, ch), lambda h, i: (h, i, 0, 0))
    anywhere = [pl.BlockSpec(memory_space=pl.ANY)] * n_ride
    out = pl.pallas_call(
        body, name="attn_fwd_gather" if n_ride else "attn_fwd", grid=(dm.heads, nch),
        in_specs=[pl.BlockSpec((ch, LANES), lambda h, i: (i, h)),
                  pl.BlockSpec((seq, LANES), lambda h, i: (0, h)),
                  pl.BlockSpec((1, nch, dm.vh, ch), lambda h, i: (h, 0, 0, 0))] + anywhere,
        out_specs=[chunk(dm.vh), chunk(8)] + anywhere,
        out_shape=[jax.ShapeDtypeStruct((dm.heads, nch, dm.vh, ch), MXU_DTYPE),
                   jax.ShapeDtypeStruct((dm.heads, nch, 8, ch), F32)] + _exchange_out_shapes((), common),
        scratch_shapes=[pltpu.VMEM((1, ch), F32), pltpu.VMEM((1, ch), F32), pltpu.VMEM((dm.vh, ch), F32),
                        pltpu.VMEM((ch, ch), F32), pltpu.VMEM((ch, ch), F32),
                        pltpu.VMEM((ch, ch), MXU_DTYPE), pltpu.VMEM((ch, ch), MXU_DTYPE)]
        + (_exchange_sems(n_ride) if n_ride else []),
        compiler_params=_params("arbitrary", "arbitrary") if n_ride else _params("parallel", "parallel"),
    )(q, k, vt, *common)
    return out[:2], out[2:]


def _attn_bwd(dm, q, k, vt, dot, lse, delta, blocks=(), common=()):
    seq, ch, nch = q.shape[0], dm.ch, dm.nch
    n_ride = len(blocks) + len(common)

    def body(*refs):
        k_ref, vt_ref, q_ref, dot_ref, lse_ref, delta_ref = refs[:6]
        ride_in = refs[6:6 + n_ride]
        dqt_ref, dk_ref, dvt_ref = refs[6 + n_ride:9 + n_ride]
        ride_out = refs[9 + n_ride:9 + 2 * n_ride]
        h, j = pl.program_id(0), pl.program_id(1)

        if n_ride:
            sems = refs[9 + 2 * n_ride:]

            @pl.when((h == 0) & (j == 0))
            def _():
                for cp in _exchange_copies(ride_in[:len(blocks)], ride_in[len(blocks):], ride_out, *sems):
                    cp.start()

        kv = k_ref[...]
        kt = kv.astype(F32).T.astype(MXU_DTYPE)
        vtv = vt_ref[0, 0]
        dk_ref[...] = jnp.zeros_like(dk_ref)
        dvt_ref[...] = jnp.zeros_like(dvt_ref)

        @pl.when(j == 0)
        def _():
            dqt_ref[...] = jnp.zeros_like(dqt_ref)

        def stage(i, carry):
            qi = q_ref[pl.ds(pl.multiple_of(i * ch, ch), ch), :]
            doti = dot_ref[0, i]
            pt = jnp.exp2(_mm_nt(kv, qi) - lse_ref[0, i][:1]).astype(MXU_DTYPE)
            dst = pt * (_mm_tn(vtv, doti) - delta_ref[0, i][:1]).astype(MXU_DTYPE)
            dvt_ref[0, 0] += _mm_nt(doti, pt)
            dk_ref[...] += _mm(dst, qi)
            dqt_ref[0, i] += _mm(kt, dst)
            return carry

        lax.fori_loop(0, nch, stage, 0)
        dk_ref[...] = dk_ref[...] * LN2

        if n_ride:
            @pl.when((h == dm.heads - 1) & (j == nch - 1))
            def _():
                copies = _exchange_copies(ride_in[:len(blocks)], ride_in[len(blocks):], ride_out, *sems)
                _exchange_wait(copies)

    whole = lambda rows: pl.BlockSpec((1, nch, rows, ch), lambda h, j: (h, 0, 0, 0))
    chunk = lambda rows: pl.BlockSpec((1, 1, rows, ch), lambda h, j: (h, j, 0, 0))
    anywhere = [pl.BlockSpec(memory_space=pl.ANY)] * n_ride
    out = pl.pallas_call(
        body, name="attn_bwd_exchange" if n_ride else "attn_bwd", grid=(dm.heads, nch),
        in_specs=[pl.BlockSpec((ch, LANES), lambda h, j: (j, h)), chunk(dm.vh),
                  pl.BlockSpec((seq, LANES), lambda h, j: (0, h)), whole(dm.vh), whole(8), whole(8)] + anywhere,
        out_specs=[whole(LANES), pl.BlockSpec((ch, LANES), lambda h, j: (j, h)), chunk(dm.vh)] + anywhere,
        out_shape=[jax.ShapeDtypeStruct((dm.heads, nch, LANES, ch), F32), jax.ShapeDtypeStruct(q.shape, F32),
                   jax.ShapeDtypeStruct((dm.heads, nch, dm.vh, ch), F32)] + _exchange_out_shapes(blocks, common),
        scratch_shapes=_exchange_sems(n_ride) if n_ride else [],
        compiler_params=_params("arbitrary", "arbitrary"),
    )(k, vt, q, dot, lse, delta, *blocks, *common)
    return out[:3], out[3:]


def _loss_head(y, target):
    seq, d = y.shape
    tm = _tile(seq, LOSS_TM)

    def body(y_ref, t_ref, part_ref, dy_ref):
        @pl.when(pl.program_id(0) == 0)
        def _():
            part_ref[...] = jnp.zeros_like(part_ref)

        err = y_ref[...] - t_ref[...]
        part_ref[...] += _colsum(err * err)
        dy_ref[...] = err / d

    row = lambda i: (i, 0)
    return pl.pallas_call(
        body, name="loss_head", grid=(seq // tm,),
        in_specs=[pl.BlockSpec((tm, d), row), pl.BlockSpec((tm, d), row)],
        out_specs=[pl.BlockSpec((1, d), lambda i: (0, 0)), pl.BlockSpec((tm, d), row)],
        out_shape=[jax.ShapeDtypeStruct((1, d), F32), jax.ShapeDtypeStruct((seq, d), F32)],
        compiler_params=_params("arbitrary"),
    )(y, target)


def _my_place():
    return lax.axis_index("x"), lax.axis_index("y"), lax.axis_index("c")


def _all_gather(shards):
    n = len(shards)

    def body(*refs):
        x_refs, out_refs = refs[:n], refs[n:2 * n]
        send_sems, recv_sems, local_sems = refs[2 * n:]
        x, y, c = _my_place()
        me, sibling = (x, y, c), (x, y, 1 - c)
        chips = [(1 - x, y), (x, 1 - y), (1 - x, 1 - y)]

        def slot(t, px, py, pc):
            return out_refs[t].at[4 * px + 2 * py + pc]

        def copy(t, k, block, to, src=None):
            return pltpu.make_async_remote_copy(
                src_ref=slot(t, *block) if src is None else src, dst_ref=slot(t, *block),
                send_sem=send_sems.at[t, k], recv_sem=recv_sems.at[t, k], device_id=to, device_id_type=MESH)

        mine = [pltpu.make_async_copy(x_refs[t], slot(t, *me), local_sems.at[t]) for t in range(n)]
        started = []
        for t in range(n):
            mine[t].start()
            first = [copy(t, 0, me, sibling, src=x_refs[t])]
            first += [copy(t, 1 + j, me, (*chip, c), src=x_refs[t]) for j, chip in enumerate(chips)]
            for cp in first:
                cp.start()
            started += first
        for j, chip in enumerate(chips):
            for t in range(n):
                copy(t, 1 + j, (*chip, c), me).wait_recv()
                passed = copy(t, 4 + j, (*chip, c), sibling)
                passed.start()
                started.append(passed)
        for t in range(n):
            copy(t, 0, sibling, me).wait_recv()
            for j, chip in enumerate(chips):
                copy(t, 4 + j, (*chip, 1 - c), me).wait_recv()
        for cp in started:
            cp.wait_send()
        for cp in mine:
            cp.wait()

    return pl.pallas_call(
        body, name="weights_all_gather",
        out_shape=[jax.ShapeDtypeStruct((N_DEV,) + s.shape, s.dtype) for s in shards],
        in_specs=[pl.BlockSpec(memory_space=pl.ANY)] * n,
        out_specs=[pl.BlockSpec(memory_space=pl.ANY)] * n,
        scratch_shapes=[pltpu.SemaphoreType.DMA((n, 7)), pltpu.SemaphoreType.DMA((n, 7)),
                        pltpu.SemaphoreType.DMA((n,))],
    )(*shards)


def _exchange_copies(g_refs, c_refs, out_refs, send_sems, recv_sems, local_sems):
    x, y, c = _my_place()
    me = 4 * x + 2 * y + c
    srcs = [lambda dev, r=r: r.at[dev] for r in g_refs] + [lambda dev, r=r: r for r in c_refs]
    copies = [pltpu.make_async_copy(srcs[t](me), out_refs[t].at[me], local_sems.at[t]) for t in range(len(srcs))]
    for k in range(1, N_DEV):
        px = 1 - x if k & 4 else x
        py = 1 - y if k & 2 else y
        pc = 1 - c if k & 1 else c
        for t in range(len(srcs)):
            copies.append(pltpu.make_async_remote_copy(
                src_ref=srcs[t](4 * px + 2 * py + pc), dst_ref=out_refs[t].at[me],
                send_sem=send_sems.at[t, k - 1], recv_sem=recv_sems.at[t, k - 1],
                device_id=(px, py, pc), device_id_type=MESH))
    return copies


def _exchange_wait(copies):
    n_local = len(copies) // N_DEV
    for cp in copies[n_local:]:
        cp.wait_recv()
    for cp in copies[n_local:]:
        cp.wait_send()
    for cp in copies[:n_local]:
        cp.wait()


def _exchange_out_shapes(blocks, common):
    return ([jax.ShapeDtypeStruct(b.shape, b.dtype) for b in blocks]
            + [jax.ShapeDtypeStruct((N_DEV,) + a.shape, a.dtype) for a in common])


def _exchange_sems(n):
    return [pltpu.SemaphoreType.DMA((n, 7)), pltpu.SemaphoreType.DMA((n, 7)), pltpu.SemaphoreType.DMA((n,))]


def _grad_exchange(blocks, common):
    n, nc = len(blocks), len(common)

    def body(*refs):
        copies = _exchange_copies(refs[:n], refs[n:n + nc], refs[n + nc:2 * (n + nc)], *refs[2 * (n + nc):])
        for cp in copies:
            cp.start()
        _exchange_wait(copies)

    return pl.pallas_call(
        body, name="grad_exchange",
        out_shape=_exchange_out_shapes(blocks, common),
        in_specs=[pl.BlockSpec(memory_space=pl.ANY)] * (n + nc),
        out_specs=[pl.BlockSpec(memory_space=pl.ANY)] * (n + nc),
        scratch_shapes=_exchange_sems(n + nc),
    )(*blocks, *common)


def _adamw(parts, w, m, v):
    rows, width = w.shape
    tr = _tile(rows, max(8, ADAM_BLOCK // width // 8 * 8))

    def body(p_ref, w_ref, m_ref, v_ref, g_ref, d_ref, nm_ref, nv_ref):
        g = p_ref[0].astype(F32)
        for s in range(1, N_DEV):
            g = g + p_ref[s].astype(F32)
        nm = ADAM_B1 * m_ref[...] + (1.0 - ADAM_B1) * g
        nv = ADAM_B2 * v_ref[...] + (1.0 - ADAM_B2) * (g * g)
        m_hat = nm / (1.0 - ADAM_B1 ** ADAM_STEP)
        v_hat = nv / (1.0 - ADAM_B2 ** ADAM_STEP)
        g_ref[...] = g
        d_ref[...] = -ADAM_LR * (m_hat / (jnp.sqrt(v_hat) + ADAM_EPS) + ADAM_WD * w_ref[...])
        nm_ref[...] = nm
        nv_ref[...] = nv

    row = pl.BlockSpec((tr, width), lambda i: (i, 0))
    return pl.pallas_call(
        body, name="adamw", grid=(rows // tr,),
        in_specs=[pl.BlockSpec((N_DEV, tr, width), lambda i: (0, i, 0)), row, row, row],
        out_specs=[row] * 4,
        out_shape=[jax.ShapeDtypeStruct(w.shape, F32)] * 4,
        compiler_params=_params("parallel"),
    )(parts, w, m, v)


def _pack_rows(flat_parts, multiple):
    flat = jnp.concatenate([p.reshape(-1) for p in flat_parts])
    chunk = multiple * PACK_W
    pad = (-flat.shape[0]) % chunk
    if pad:
        flat = jnp.concatenate([flat, jnp.zeros((pad,), flat.dtype)])
    return flat.reshape(-1, PACK_W)


def _unpack(packed, shapes):
    flat = packed.reshape(-1)
    out, off = [], 0
    for shape in shapes:
        size = 1
        for s in shape:
            size *= s
        out.append(flat[off:off + size].reshape(shape))
        off += size
    return out


def _to_full(name, g):
    n, l, a, b = g.shape
    if name in ROW_SHARDED:
        return jnp.transpose(g, (1, 0, 2, 3)).reshape(l, n * a, b)
    return jnp.transpose(g, (1, 2, 0, 3)).reshape(l, a, n * b)


def _to_shards(name, full):
    l, a, b = full.shape
    if name in ROW_SHARDED:
        return jnp.transpose(full.reshape(l, N_DEV, a // N_DEV, b), (1, 0, 2, 3))
    return jnp.transpose(full.reshape(l, a, N_DEV, b // N_DEV), (2, 0, 1, 3))


def _pad_heads(w, heads, real):
    lead = w.shape[:-1]
    w = w.reshape(lead + (heads, real))
    w = jnp.concatenate([w, jnp.zeros(lead + (heads, LANES - real), w.dtype)], axis=-1)
    return w.reshape(lead + (heads * LANES,))


def _unpad_heads(w, heads, real):
    lead = w.shape[:-1]
    return w.reshape(lead + (heads, LANES))[..., :real].reshape(lead + (heads * real,))


def _pad_lanes(v, before):
    l, n = v.shape
    return jnp.concatenate([jnp.zeros((l, before), v.dtype), v, jnp.zeros((l, LANES - before - n), v.dtype)], axis=1)


def kernel(x, ffn1_norm, ffn1_w_gu, ffn1_w_down, mix_norm, w_in, q_lat_norm, kv_lat_norm, w_uq, w_uk, w_uv, q_norm, k_norm, w_pool, pool_scale, w_out, ffn2_norm, ffn2_w_gu, ffn2_w_down, loss_target, m_ffn1_norm, m_ffn1_w_gu, m_ffn1_w_down, m_mix_norm, m_w_in, m_q_lat_norm, m_kv_lat_norm, m_w_uq, m_w_uk, m_w_uv, m_q_norm, m_k_norm, m_w_pool, m_pool_scale, m_w_out, m_ffn2_norm, m_ffn2_w_gu, m_ffn2_w_down, v_ffn1_norm, v_ffn1_w_gu, v_ffn1_w_down, v_mix_norm, v_w_in, v_q_lat_norm, v_kv_lat_norm, v_w_uq, v_w_uk, v_w_uv, v_q_norm, v_k_norm, v_w_pool, v_pool_scale, v_w_out, v_ffn2_norm, v_ffn2_w_gu, v_ffn2_w_down):
    given = dict(locals())
    wts = {n: given[n] for n in WEIGHTS}
    mom1 = {n: given["m_" + n] for n in WEIGHTS}
    mom2 = {n: given["v_" + n] for n in WEIGHTS}

    depth, d = ffn1_norm.shape
    seq = x.shape[1]
    dff = ffn1_w_down.shape[1] * N_DEV
    ql, kvl, head_dim = q_lat_norm.shape[1], kv_lat_norm.shape[1], q_norm.shape[1]
    heads = w_uq.shape[2] * N_DEV // head_dim
    nope = w_uk.shape[2] * N_DEV // heads
    vh = w_uv.shape[2] * N_DEV // heads
    groups, gdim = w_pool.shape[1], w_pool.shape[2]
    pool_w = groups * gdim
    assert gdim == LANES and groups == len(POOL_WINDOWS) and head_dim <= LANES and vh <= LANES
    assert d % LANES == 0 and ql % LANES == 0 and kvl % LANES == 0 and seq % HALO == 0
    dm = _Dims(d, ql, kvl, heads, head_dim, nope, vh, pool_w, seq)

    def with_padded(full):
        zpad = jnp.zeros((full["w_in"].shape[0], d, LANES), MXU_DTYPE)
        full["win_p"] = jnp.concatenate(
            [full["w_in"][..., :dm.o_pe], zpad[..., :nope], full["w_in"][..., dm.o_pe:dm.o_pe + dm.rope],
             zpad[..., :LANES - nope - dm.rope], full["w_in"][..., dm.o_pe + dm.rope:]], axis=-1)
        full["wuq_p"] = _pad_heads(full["w_uq"], heads, head_dim)
        full["wuk_p"] = _pad_heads(full["w_uk"], heads, nope)
        return full

    later = [wts[n][1:].astype(MXU_DTYPE) for n in SHARDED] if depth > 1 else []
    first = _all_gather([wts[n][:1].astype(MXU_DTYPE) for n in SHARDED])
    groups = [with_padded({n: _to_full(n, g) for n, g in zip(SHARDED, first)}), None]

    def layer_weights(l):
        group, i = (groups[0], 0) if l == 0 else (groups[1], l - 1)
        return {n: a[i] for n, a in group.items()}

    gq_p, gk_p = _pad_lanes(q_norm, 0), _pad_lanes(k_norm, 0)
    wpool_c = w_pool.astype(MXU_DTYPE)
    rc, rs1, rs2 = _rope_tables(seq, nope, dm.rope)
    row = lambda a, l: a[l][None, :]

    h = x[0]
    saved = []
    for l in range(depth):
        w = layer_weights(l)
        x0 = h
        x1 = _ffn_fwd(x0, row(ffn1_norm, l), w["ffn1_w_gu"], w["ffn1_w_down"])
        q, k, v, zp = _mixin_fwd(dm, x1, row(mix_norm, l), w["win_p"], row(q_lat_norm, l), row(kv_lat_norm, l),
                                 w["wuq_p"], w["wuk_p"], w["w_uv"], row(gq_p, l), row(gk_p, l), rc, rs1, rs2)
        (o, lse), got = _attn_fwd(dm, q, k, v, later if l == 0 else ())
        if l == 0 and later:
            groups[1] = with_padded({n: _to_full(n, g) for n, g in zip(SHARDED, got)})
        x2 = _mixout_fwd(dm, x1, zp, o, w["w_out"], wpool_c[l], row(pool_scale, l))
        h = _ffn_fwd(x2, row(ffn2_norm, l), w["ffn2_w_gu"], w["ffn2_w_down"])
        saved.append((x0, x1, x2, q, k, v, zp, o, lse))

    part, g = _loss_head(h, loss_target[0])
    loss = lax.psum(0.5 / d * jnp.sum(part), ("x", "y", "c"))

    def ffn_grads(grads, w, prefix, l, xin, gout):
        wgu, wd = w[prefix + "_w_gu"], w[prefix + "_w_down"]
        gain = row(given[prefix + "_norm"], l)
        hh, act, dg, du = _ffn_bwd_act(xin, gain, wgu, wd.T, gout)
        gin, dgain = _ffn_bwd_in(xin, gain, wgu.T, dg, du, gout)
        grads[prefix + "_norm"] = dgain[0]
        grads[prefix + "_w_gu"] = jnp.concatenate(
            [_wgrad(hh, dg, 1.0, 1024, 1408), _wgrad(hh, du, 1.0, 1024, 1408)], axis=1)
        grads[prefix + "_w_down"] = _wgrad(act, gout, 0.5, 1408, 1024)
        return gin

    received = [None] * depth
    pending = None
    for l in reversed(range(depth)):
        x0, x1, x2, q, k, v, zp, o, lse = saved[l]
        w = layer_weights(l)
        grads = {}
        g = ffn_grads(grads, w, "ffn2", l, x2, g)
        do, delta, dmixed, dwo, dwpool, dps = _mixout_bwd(dm, g, zp, o, w["w_out"], wpool_c[l], row(pool_scale, l))
        if pending is None:
            (dq, dk, dv), _ = _attn_bwd(dm, q, k, v, do, lse, delta)
        else:
            (dq, dk, dv), received[l + 1] = _attn_bwd(dm, q, k, v, do, lse, delta, *pending)
        dz, dwuq, dwuk, dwuv, dgql, dgkvl, dgq, dgk = _mla_in_bwd(
            dm, x1, row(mix_norm, l), w["win_p"], row(q_lat_norm, l), row(kv_lat_norm, l), w["wuq_p"], w["wuk_p"],
            w["w_uv"], row(gq_p, l), row(gk_p, l), rc, rs1, rs2, dq, dk, dv, dmixed)
        g, dwin, dgmix = _rms_proj_bwd(x1, row(mix_norm, l), w["win_p"], dz, g)
        grads["w_out"] = dwo
        grads["w_pool"] = dwpool
        grads["pool_scale"] = dps[0]
        grads["w_uq"] = _unpad_heads(dwuq, heads, head_dim)
        grads["w_uk"] = _unpad_heads(dwuk, heads, nope)
        grads["w_uv"] = dwuv
        grads["q_lat_norm"] = dgql[0]
        grads["kv_lat_norm"] = dgkvl[0]
        grads["q_norm"] = dgq[0, :head_dim]
        grads["k_norm"] = dgk[0, :head_dim]
        grads["w_in"] = jnp.concatenate(
            [dwin[:, :dm.o_pe], dwin[:, dm.o_pe + nope:dm.o_pe + nope + dm.rope], dwin[:, dm.o_pool:]], axis=1)
        grads["mix_norm"] = dgmix[0]
        g = ffn_grads(grads, w, "ffn1", l, x0, g)
        pending = ([_to_shards(n, grads[n][None])[:, 0].astype(WIRE_DTYPE) for n in SHARDED],
                   [_pack_rows([grads[n] for n in REPLICATED], 8)])
    received[0] = _grad_exchange(*pending)

    outs = [{}, {}, {}, {}]
    for t, n in enumerate(SHARDED):
        nl, a, b = wts[n].shape
        parts = jnp.stack([received[l][t] for l in range(depth)], axis=1)
        results = _adamw(parts.reshape(N_DEV, nl * a, b),
                         *[src[n].reshape(nl * a, b) for src in (wts, mom1, mom2)])
        for out, r in zip(outs, results):
            out[n] = r.reshape(nl, a, b)
    repl_shapes = [wts[n].shape[1:] for n in REPLICATED]
    per_layer = []
    for l in range(depth):
        results = _adamw(received[l][-1],
                         *[_pack_rows([src[n][l] for n in REPLICATED], 8) for src in (wts, mom1, mom2)])
        per_layer.append([_unpack(r, repl_shapes) for r in results])
    for i, out in enumerate(outs):
        for t, n in enumerate(REPLICATED):
            out[n] = jnp.stack([per_layer[l][i][t] for l in range(depth)])

    return (loss, g[None], *[out[n] for out in outs for n in WEIGHTS])
```
